```python
import jax, jax.numpy as jnp
from jax import lax
import numpy as np

D_MODEL = 1024
BATCH = 2
SEQ = 16384
DEPTH = 2

GRID_W = 64
CTX_LEN = 256
MIX_W = D_MODEL
NA_HEADS = 8
NA_HEAD_DIM = 64
NA_W = NA_HEADS * NA_HEAD_DIM
NA_KH = 8
NA_KW = 16
SGU_GROUPS = 8
SGU_W = MIX_W - NA_W
SGU_GROUP_DIM = SGU_W // SGU_GROUPS
SGU_CHUNK = 128
IN_COLS = 3 * NA_W + 2 * SGU_W
N_GROUPS = 4
EXPERTS_PER_GROUP = 4
N_EXPERTS = N_GROUPS * EXPERTS_PER_GROUP
TOP_K = 2
D_EXPERT = D_MODEL // 2
N_MOD = 6
EPS = 1e-6

kernel_name = "hybrid_na_sgu_hmoe_dit"


def _rms(x):
    x32 = x.astype(jnp.float32)
    return (x32 * lax.rsqrt(jnp.mean(x32 * x32, -1, keepdims=True) + EPS)).astype(x.dtype)


def _layernorm(x, g):
    x32 = x.astype(jnp.float32)
    mu = jnp.mean(x32, -1, keepdims=True)
    var = jnp.mean(jnp.square(x32 - mu), -1, keepdims=True)
    return ((x32 - mu) * lax.rsqrt(var + EPS)).astype(x.dtype) * g


def _adaln(cond, w, b):
    return jnp.split(jax.nn.silu(cond) @ w + b, N_MOD, axis=-1)


def _modulate(x, shift, scale):
    return _rms(x) * (1 + scale) + shift


def _heads(t):
    return t.reshape(t.shape[:-1] + (NA_HEADS, NA_HEAD_DIM))


def _project(h, w_in, q_gain, k_gain):
    p = h @ w_in
    q = _rms(_heads(p[..., :NA_W])) * q_gain
    k = _rms(_heads(p[..., NA_W:2 * NA_W])) * k_gain
    v = _heads(p[..., 2 * NA_W:3 * NA_W])
    u = p[..., 3 * NA_W:3 * NA_W + SGU_W]
    vs = p[..., 3 * NA_W + SGU_W:]
    return q, k, v, u, vs


def _project_kv(h, w_in, k_gain):
    p = h @ w_in[:, NA_W:3 * NA_W]
    return _rms(_heads(p[..., :NA_W])) * k_gain, _heads(p[..., NA_W:])


def _neighborhood_attention(q, k, v, k_ctx, v_ctx, rpb):
    bsz, seq, heads, dh = q.shape
    rows = seq // GRID_W
    kh = min(NA_KH, rows)
    kw = NA_KW
    scale = dh ** -0.5
    q_rows = jnp.moveaxis(q.reshape(bsz, rows, GRID_W, heads, dh), 1, 0)
    k_grid = k.reshape(bsz, rows, GRID_W, heads, dh)
    v_grid = v.reshape(bsz, rows, GRID_W, heads, dh)
    col = jnp.arange(GRID_W)
    col_start = jnp.clip(col - kw // 2, 0, GRID_W - kw)
    col_idx = col_start[:, None] + jnp.arange(kw)[None, :]
    rpb_cols = rpb[:, :, col_idx - col[:, None] + (NA_KW - 1)].astype(jnp.float32)

    def row_block(args):
        q_r, r = args
        r_start = jnp.clip(r - kh // 2, 0, rows - kh)
        k_win = lax.dynamic_slice_in_dim(k_grid, r_start, kh, axis=1)[:, :, col_idx]
        v_win = lax.dynamic_slice_in_dim(v_grid, r_start, kh, axis=1)[:, :, col_idx]
        bias = rpb_cols[:, r_start + jnp.arange(kh) - r + (NA_KH - 1)]
        s_win = jnp.einsum('bqhd,bkqwhd->bhqkw', q_r, k_win).astype(jnp.float32) * scale
        s_win = s_win + jnp.transpose(bias, (0, 2, 1, 3))[None]
        s_ctx = jnp.einsum('bqhd,bchd->bhqc', q_r, k_ctx).astype(jnp.float32) * scale
        s = jnp.concatenate([s_win.reshape(bsz, heads, GRID_W, kh * kw), s_ctx], axis=-1)
        p = jax.nn.softmax(s, axis=-1).astype(v.dtype)
        p_win = p[..., :kh * kw].reshape(bsz, heads, GRID_W, kh, kw)
        p_ctx = p[..., kh * kw:]
        return (jnp.einsum('bhqkw,bkqwhd->bqhd', p_win, v_win)
                + jnp.einsum('bhqc,bchd->bqhd', p_ctx, v_ctx))

    out = lax.map(row_block, (q_rows, jnp.arange(rows)))
    return jnp.moveaxis(out, 0, 1).reshape(bsz, seq, heads * dh)


def _context_attention(q, k, v):
    bsz, clen, heads, dh = q.shape
    s = jnp.einsum('bqhd,bkhd->bhqk', q, k).astype(jnp.float32) * dh ** -0.5
    p = jax.nn.softmax(s, axis=-1).astype(v.dtype)
    return jnp.einsum('bhqk,bkhd->bqhd', p, v).reshape(bsz, clen, heads * dh)


def _sgu(u, v, ln_g, w_s, b_s):
    bsz, length, _ = u.shape
    u = jax.nn.gelu(u)
    v = _layernorm(jax.nn.gelu(v), ln_g)
    v = v.reshape(bsz, length // SGU_CHUNK, SGU_CHUNK, SGU_GROUPS, SGU_GROUP_DIM)
    mixed = jnp.einsum('hpq,bnqhd->bnphd', w_s, v) + b_s.T[:, :, None]
    return u * mixed.reshape(bsz, length, SGU_W)


def _merge(o_na, o_sgu, out_gain, w_out):
    y = jnp.concatenate([_rms(o_na), _rms(o_sgu)], axis=-1) * out_gain
    return y @ w_out


def _hier_moe(h, rg_w, rg_b, re_w, re_b, w1, w3, w2):
    shp = h.shape
    t = h.reshape(-1, D_MODEL)
    p_g = jax.nn.softmax((t @ rg_w).astype(jnp.float32) + rg_b, axis=-1)
    p_top, g_idx = lax.top_k(p_g, 1)
    e_all = jnp.einsum('nd,gde->nge', t, re_w).astype(jnp.float32) + re_b
    e_logits = jnp.take_along_axis(e_all, g_idx[:, :, None], axis=1)[:, 0]
    e_top, e_idx = lax.top_k(jax.nn.softmax(e_logits, axis=-1), TOP_K)
    w = e_top / jnp.sum(e_top, -1, keepdims=True) * p_top
    within = jnp.sum(jax.nn.one_hot(e_idx, EXPERTS_PER_GROUP, dtype=jnp.float32) * w[..., None], axis=1)
    gate = (jax.nn.one_hot(g_idx[:, 0], N_GROUPS, dtype=jnp.float32)[:, :, None]
            * within[:, None, :]).reshape(-1, N_EXPERTS).astype(t.dtype)
    out = jnp.zeros_like(t)
    for e in range(N_EXPERTS):
        hid = jax.nn.silu(t @ w1[e]) * (t @ w3[e])
        out = out + gate[:, e:e + 1] * (hid @ w2[e])
    return out.reshape(shp)


def setup_inputs(seed: int = 0) -> dict:
    key = jax.random.key(seed)
    ks = jax.random.split(key, 22)

    def nrm(k, shape, s):
        return jax.random.normal(k, shape, jnp.float32) * s

    return {
        'x': nrm(ks[0], (BATCH, SEQ, D_MODEL), 1.0),
        'c': nrm(ks[1], (BATCH, D_MODEL), 1.0),
        'ctx': nrm(ks[2], (BATCH, CTX_LEN, D_MODEL), 1.0),
        'c_ctx': nrm(ks[3], (D_MODEL,), 1.0),
        'w_ada': nrm(ks[4], (DEPTH, D_MODEL, N_MOD * D_MODEL), 0.5 * D_MODEL ** -0.5),
        'b_ada': nrm(ks[5], (DEPTH, N_MOD * D_MODEL), 0.02),
        'w_in': nrm(ks[6], (DEPTH, D_MODEL, IN_COLS), D_MODEL ** -0.5),
        'q_gain': 1.0 + nrm(ks[7], (DEPTH, NA_HEAD_DIM), 0.02),
        'k_gain': 1.0 + nrm(ks[8], (DEPTH, NA_HEAD_DIM), 0.02),
        'rpb': nrm(ks[9], (DEPTH, NA_HEADS, 2 * NA_KH - 1, 2 * NA_KW - 1), 0.1),
        'sgu_ln': 1.0 + nrm(ks[10], (DEPTH, SGU_W), 0.02),
        'sgu_w': nrm(ks[11], (DEPTH, SGU_GROUPS, SGU_CHUNK, SGU_CHUNK), SGU_CHUNK ** -0.5),
        'sgu_b': 1.0 + nrm(ks[12], (DEPTH, SGU_GROUPS, SGU_CHUNK), 0.02),
        'out_gain': 1.0 + nrm(ks[13], (DEPTH, MIX_W), 0.02),
        'w_out': nrm(ks[14], (DEPTH, MIX_W, D_MODEL), MIX_W ** -0.5),
        'rg_w': nrm(ks[15], (DEPTH, D_MODEL, N_GROUPS), D_MODEL ** -0.5),
        'rg_b': nrm(ks[16], (DEPTH, N_GROUPS), 0.01),
        're_w': nrm(ks[17], (DEPTH, N_GROUPS, D_MODEL, EXPERTS_PER_GROUP), D_MODEL ** -0.5),
        're_b': nrm(ks[18], (DEPTH, N_GROUPS, EXPERTS_PER_GROUP), 0.01),
        'w1': nrm(ks[19], (DEPTH, N_EXPERTS, D_MODEL, D_EXPERT), D_MODEL ** -0.5),
        'w3': nrm(ks[20], (DEPTH, N_EXPERTS, D_MODEL, D_EXPERT), D_MODEL ** -0.5),
        'w2': nrm(ks[21], (DEPTH, N_EXPERTS, D_EXPERT, D_MODEL), D_EXPERT ** -0.5),
    }


def reference(x, c, ctx, c_ctx, w_ada, b_ada, w_in, q_gain, k_gain, rpb, sgu_ln, sgu_w, sgu_b,
              out_gain, w_out, rg_w, rg_b, re_w, re_b, w1, w3, w2):
    for i in range(DEPTH):
        last = i == DEPTH - 1
        sh_a, sc_a, g_a, sh_f, sc_f, g_f = _adaln(c, w_ada[i], b_ada[i])
        csh_a, csc_a, cg_a, csh_f, csc_f, cg_f = _adaln(c_ctx, w_ada[i], b_ada[i])

        h = _modulate(x, sh_a[:, None], sc_a[:, None])
        hc = _modulate(ctx, csh_a, csc_a)
        q, k, v, u, vs = _project(h, w_in[i], q_gain[i], k_gain[i])
        if last:
            kc, vc = _project_kv(hc, w_in[i], k_gain[i])
        else:
            qc, kc, vc, uc, vsc = _project(hc, w_in[i], q_gain[i], k_gain[i])
        o_na = _neighborhood_attention(q, k, v, kc, vc, rpb[i])
        o_sg = _sgu(u, vs, sgu_ln[i], sgu_w[i], sgu_b[i])
        x = x + g_a[:, None] * _merge(o_na, o_sg, out_gain[i], w_out[i])

        hf = _modulate(x, sh_f[:, None], sc_f[:, None])
        x = x + g_f[:, None] * _hier_moe(hf, rg_w[i], rg_b[i], re_w[i], re_b[i], w1[i], w3[i], w2[i])

        if not last:
            oc_na = _context_attention(qc, kc, vc)
            oc_sg = _sgu(uc, vsc, sgu_ln[i], sgu_w[i], sgu_b[i])
            ctx = ctx + cg_a * _merge(oc_na, oc_sg, out_gain[i], w_out[i])
            hcf = _modulate(ctx, csh_f, csc_f)
            ctx = ctx + cg_f * _hier_moe(hcf, rg_w[i], rg_b[i], re_w[i], re_b[i], w1[i], w3[i], w2[i])
    return x
```

```python
import functools

import jax
import jax.numpy as jnp
import numpy as np
from jax import lax
from jax.experimental import pallas as pl
from jax.experimental.pallas import tpu as pltpu

D_MODEL = 1024
GRID_W = 64
NA_HEADS = 8
NA_HEAD_DIM = 64
NA_W = NA_HEADS * NA_HEAD_DIM
NA_KH = 8
NA_KW = 16
SGU_GROUPS = 8
SGU_W = D_MODEL - NA_W
SGU_GROUP_DIM = SGU_W // SGU_GROUPS
SGU_CHUNK = 128
IN_COLS = 3 * NA_W + 2 * SGU_W
N_GROUPS = 4
EXPERTS_PER_GROUP = 4
N_EXPERTS = N_GROUPS * EXPERTS_PER_GROUP
D_EXPERT = D_MODEL // 2
N_MOD = 6
EPS = 1e-6

LANES = 128
HEAD_PAIRS = NA_W // LANES
Q_ROWS = 8
Q_BLOCK = Q_ROWS * GRID_W
WIN_ROWS = Q_ROWS + NA_KH - 1
WIN_BLOCK = WIN_ROWS * GRID_W
N_PAIRS = 6
N_CLASSES = N_GROUPS * N_PAIRS
PAIR_A = (0, 0, 0, 1, 1, 2)
PAIR_B = (1, 2, 3, 2, 3, 3)
MASKED = -1e30
VMEM_LIMIT_BYTES = 56 * 1024 * 1024

_f32 = jnp.float32
_bf16 = jnp.bfloat16


def _params(*semantics):
    return pltpu.CompilerParams(dimension_semantics=semantics, vmem_limit_bytes=VMEM_LIMIT_BYTES)


def _dot(a, b):
    return jnp.dot(a, b, preferred_element_type=_f32)


def _dot_nt(a, b):
    return lax.dot_general(a, b, (((1,), (1,)), ((), ())), preferred_element_type=_f32)


def _adaln_kernel(cond_ref, w_ref, b_ref, out_ref):
    cond = cond_ref[...]
    act = cond * jax.nn.sigmoid(cond)
    out_ref[...] = _dot(act.astype(_bf16), w_ref[...].astype(_bf16)) + b_ref[...]


def _adaln(cond, w_ada, b_ada):
    depth = w_ada.shape[0]
    cols = N_MOD * D_MODEL
    col_block = 1536
    return pl.pallas_call(
        _adaln_kernel,
        grid=(depth, cols // col_block),
        in_specs=[
            pl.BlockSpec((8, D_MODEL), lambda d, j: (0, 0)),
            pl.BlockSpec((None, D_MODEL, col_block), lambda d, j: (d, 0, j)),
            pl.BlockSpec((None, 1, col_block), lambda d, j: (d, 0, j)),
        ],
        out_specs=pl.BlockSpec((None, 8, col_block), lambda d, j: (d, 0, j)),
        out_shape=jax.ShapeDtypeStruct((depth, 8, cols), _f32),
        compiler_params=_params("arbitrary", "arbitrary"),
        name="adaln",
    )(cond, w_ada, b_ada.reshape(depth, 1, cols))


def _proj_kernel(x_ref, sh_ref, sc_ref, w_ref, avg_ref, qg_ref, kg_ref, ln_ref,
                 q_ref, k_ref, v_ref, gu_ref, vn_ref):
    x = x_ref[...]
    h = x * lax.rsqrt(jnp.mean(x * x, -1, keepdims=True) + EPS) * (1.0 + sc_ref[...]) + sh_ref[...]
    p = _dot(h.astype(_bf16), w_ref[...])
    q = p[:, :NA_W]
    k = p[:, NA_W:2 * NA_W]
    v = p[:, 2 * NA_W:3 * NA_W]
    u = p[:, 3 * NA_W:3 * NA_W + SGU_W]
    vs = p[:, 3 * NA_W + SGU_W:]
    q_ms = _dot((q * q).astype(_bf16), avg_ref[...])
    k_ms = _dot((k * k).astype(_bf16), avg_ref[...])
    qn = (q * lax.rsqrt(q_ms + EPS) * qg_ref[...]).astype(_bf16)
    kn = (k * lax.rsqrt(k_ms + EPS) * kg_ref[...]).astype(_bf16)
    vb = v.astype(_bf16)
    for hp in range(HEAD_PAIRS):
        cols = slice(hp * LANES, (hp + 1) * LANES)
        q_ref[hp] = qn[:, cols]
        k_ref[hp] = kn[:, cols]
        v_ref[hp] = vb[:, cols]
    gu_ref[...] = jax.nn.gelu(u).astype(_bf16)
    gv = jax.nn.gelu(vs)
    mu = jnp.mean(gv, -1, keepdims=True)
    cen = gv - mu
    var = jnp.mean(cen * cen, -1, keepdims=True)
    vn_ref[...] = (cen * lax.rsqrt(var + EPS) * ln_ref[...]).astype(_bf16)


def _proj(x, sh, sc, w_in, avg, qg, kg, ln_g, tm):
    bsz, length, _ = x.shape
    vec = lambda n: pl.BlockSpec((1, n), lambda b, i: (0, 0))
    mod = pl.BlockSpec((None, 1, D_MODEL), lambda b, i: (b, 0, 0))
    tok = lambda n: pl.BlockSpec((None, tm, n), lambda b, i: (b, i, 0))
    hpm = pl.BlockSpec((None, HEAD_PAIRS, tm, LANES), lambda b, i: (b, 0, i, 0))
    hpm_shape = jax.ShapeDtypeStruct((bsz, HEAD_PAIRS, length, LANES), _bf16)
    sgu_shape = jax.ShapeDtypeStruct((bsz, length, SGU_W), _bf16)
    return pl.pallas_call(
        _proj_kernel,
        grid=(bsz, length // tm),
        in_specs=[tok(D_MODEL), mod, mod,
                  pl.BlockSpec((D_MODEL, IN_COLS), lambda b, i: (0, 0)),
                  pl.BlockSpec((NA_W, NA_W), lambda b, i: (0, 0)),
                  vec(NA_W), vec(NA_W), vec(SGU_W)],
        out_specs=[hpm, hpm, hpm, tok(SGU_W), tok(SGU_W)],
        out_shape=[hpm_shape, hpm_shape, hpm_shape, sgu_shape, sgu_shape],
        compiler_params=_params("arbitrary", "arbitrary"),
        name="proj",
    )(x, sh, sc, w_in, avg, qg, kg, ln_g)


def _softmax_pv(q, keys, values, biases):
    lane = lax.broadcasted_iota(jnp.int32, q.shape, 1)
    outs = []
    for hh in range(2):
        in_head = (lane >= hh * NA_HEAD_DIM) & (lane < (hh + 1) * NA_HEAD_DIM)
        qm = jnp.where(in_head, q, jnp.zeros_like(q))
        scores = []
        for key, bias in zip(keys, biases):
            s = _dot_nt(qm, key)
            scores.append(s if bias is None else s + bias[hh])
        m = functools.reduce(jnp.maximum, [jnp.max(s, -1, keepdims=True) for s in scores])
        probs = [jnp.exp(s - m) for s in scores]
        denom = functools.reduce(lambda a, b: a + b, [jnp.sum(p, -1, keepdims=True) for p in probs])
        acc = functools.reduce(lambda a, b: a + b,
                               [_dot(p.astype(_bf16), val) for p, val in zip(probs, values)])
        outs.append(acc / denom)
    lane_o = lax.broadcasted_iota(jnp.int32, outs[0].shape, 1)
    return jnp.where(lane_o < NA_HEAD_DIM, outs[0], outs[1])


def _attn_kernel(q_ref, k_ref, v_ref, kc_ref, vc_ref, bias_ref, o_ref):
    o_ref[...] = _softmax_pv(q_ref[...], [k_ref[...], kc_ref[...]], [v_ref[...], vc_ref[...]],
                             [bias_ref, None]).astype(o_ref.dtype)


def _attention(q, k, v, kc, vc, bias):
    bsz, _, seq, _ = q.shape
    clen = kc.shape[2]
    nblk = seq // Q_BLOCK
    half = (NA_KH // 2) * GRID_W

    def win_start(i):
        return jnp.clip(i * Q_ROWS - NA_KH // 2, 0, seq // GRID_W - WIN_ROWS) * GRID_W

    def variant(i):
        return jnp.where(i == 0, 0, jnp.where(i == nblk - 1, 2, 1))

    qspec = pl.BlockSpec((None, None, Q_BLOCK, LANES), lambda hp, b, i: (b, hp, i, 0))
    wspec = pl.BlockSpec((None, None, pl.Element(WIN_BLOCK), pl.Element(LANES)),
                         lambda hp, b, i: (b, hp, win_start(i), 0))
    cspec = pl.BlockSpec((None, None, clen, LANES), lambda hp, b, i: (b, hp, 0, 0))
    bspec = pl.BlockSpec((None, None, 2, Q_BLOCK, WIN_BLOCK), lambda hp, b, i: (variant(i), hp, 0, 0, 0))
    return pl.pallas_call(
        _attn_kernel,
        grid=(HEAD_PAIRS, bsz, nblk),
        in_specs=[qspec, wspec, wspec, cspec, cspec, bspec],
        out_specs=qspec,
        out_shape=jax.ShapeDtypeStruct(q.shape, _bf16),
        compiler_params=_params("arbitrary", "arbitrary", "arbitrary"),
        name="attention",
    )(q, k, v, kc, vc, bias)


def _ctx_attn_kernel(q_ref, k_ref, v_ref, o_ref):
    o_ref[...] = _softmax_pv(q_ref[...], [k_ref[...]], [v_ref[...]], [None]).astype(o_ref.dtype)


def _ctx_attention(q, k, v):
    bsz, _, clen, _ = q.shape
    spec = pl.BlockSpec((None, None, clen, LANES), lambda b, hp: (b, hp, 0, 0))
    return pl.pallas_call(
        _ctx_attn_kernel,
        grid=(bsz, HEAD_PAIRS),
        in_specs=[spec, spec, spec],
        out_specs=spec,
        out_shape=jax.ShapeDtypeStruct(q.shape, _bf16),
        compiler_params=_params("arbitrary", "arbitrary"),
        name="ctx_attention",
    )(q, k, v)


def _window_bias(rpb, rows):
    nblk = rows // Q_ROWS
    col = np.arange(GRID_W)
    col_start = np.clip(col - NA_KW // 2, 0, GRID_W - NA_KW)
    kc = np.arange(GRID_W)
    col_ok = (kc[None, :] >= col_start[:, None]) & (kc[None, :] < col_start[:, None] + NA_KW)
    dc = np.where(col_ok, kc[None, :] - col[:, None] + NA_KW - 1, 0)
    onehot = np.zeros((2 * NA_KW - 1, GRID_W * GRID_W), np.float32)
    onehot[dc.reshape(-1), np.arange(GRID_W * GRID_W)] = col_ok.reshape(-1)
    by_col = jnp.dot(rpb.reshape(-1, 2 * NA_KW - 1), jnp.asarray(onehot), precision=lax.Precision.HIGHEST)
    by_col = by_col.reshape(NA_HEADS, 2 * NA_KH - 1, GRID_W, GRID_W)
    by_col = jnp.where(jnp.asarray(col_ok)[None, None], by_col, MASKED)
    by_col = jnp.concatenate([by_col, jnp.full((NA_HEADS, 1, GRID_W, GRID_W), MASKED, _f32)], axis=1)
    dr_idx = np.full((3, Q_ROWS, WIN_ROWS), 2 * NA_KH - 1, np.int32)
    for var, blk in enumerate((0, 1 if nblk > 2 else 0, nblk - 1)):
        r0 = blk * Q_ROWS
        ws = int(np.clip(r0 - NA_KH // 2, 0, rows - WIN_ROWS))
        for rq in range(Q_ROWS):
            r = r0 + rq
            r_start = int(np.clip(r - NA_KH // 2, 0, rows - NA_KH))
            for kr in range(WIN_ROWS):
                if r_start <= ws + kr < r_start + NA_KH:
                    dr_idx[var, rq, kr] = ws + kr - r + NA_KH - 1
    full = jnp.take(by_col, jnp.asarray(dr_idx.reshape(-1)), axis=1)
    full = full.reshape(NA_HEADS, 3, Q_ROWS, WIN_ROWS, GRID_W, GRID_W)
    full = jnp.transpose(full, (1, 0, 2, 4, 3, 5))
    return full.reshape(3, HEAD_PAIRS, 2, Q_BLOCK, WIN_BLOCK)


def _merge_kernel(x_ref, o_ref, gu_ref, vn_ref, ga_ref, shf_ref, scf_ref, ws_ref, bs_ref, og_ref,
                  wout_ref, wr_ref, br_ref, xn_ref, hf_ref, ri_ref, *, tm):
    nchunk = tm // SGU_CHUNK
    vn = vn_ref[...]
    lane = lax.broadcasted_iota(jnp.int32, (SGU_CHUNK, LANES), 1)
    low = lane < SGU_GROUP_DIM
    mixed_cols = []
    for gp in range(SGU_W // LANES):
        rhs_cols = []
        for n in range(nchunk):
            blk = vn[n * SGU_CHUNK:(n + 1) * SGU_CHUNK, gp * LANES:(gp + 1) * LANES]
            zero = jnp.zeros_like(blk)
            rhs_cols.append(jnp.concatenate([jnp.where(low, blk, zero), jnp.where(low, zero, blk)], axis=0))
        rhs = jnp.concatenate(rhs_cols, axis=1)
        mix = _dot(ws_ref[gp], rhs)
        mixed_cols.append(jnp.concatenate([mix[:, n * LANES:(n + 1) * LANES] for n in range(nchunk)], axis=0))
    mixed = jnp.concatenate(mixed_cols, axis=1)
    bias = jnp.concatenate([bs_ref[...]] * nchunk, axis=0)
    o_sg = gu_ref[...].astype(_f32) * (mixed + bias)
    o_na = jnp.concatenate([o_ref[hp].astype(_f32) for hp in range(HEAD_PAIRS)], axis=1)

    def rms(t):
        return t * lax.rsqrt(jnp.mean(t * t, -1, keepdims=True) + EPS)

    y = jnp.concatenate([rms(o_na), rms(o_sg)], axis=1) * og_ref[...]
    xn = x_ref[...] + ga_ref[...] * _dot(y.astype(_bf16), wout_ref[...])
    xn_ref[...] = xn
    hf = (rms(xn) * (1.0 + scf_ref[...]) + shf_ref[...]).astype(_bf16)
    hf_ref[...] = hf

    lg = _dot(hf, wr_ref[...]) + br_ref[...]
    lane_i = lax.broadcasted_iota(jnp.int32, lg.shape, 1)
    lane_f = lane_i.astype(_f32)
    far = float(LANES)
    is_g = lane_i < N_GROUPS
    gl = jnp.where(is_g, lg, MASKED)
    gm = jnp.max(gl, -1, keepdims=True)
    p_top = 1.0 / jnp.sum(jnp.where(is_g, jnp.exp(gl - gm), 0.0), -1, keepdims=True)
    g_idx = jnp.min(jnp.where(is_g & (gl == gm), lane_f, far), -1, keepdims=True)
    e_lo = N_GROUPS + EXPERTS_PER_GROUP * g_idx
    is_e = (lane_f >= e_lo) & (lane_f < e_lo + EXPERTS_PER_GROUP)
    el = jnp.where(is_e, lg, MASKED)
    em = jnp.max(el, -1, keepdims=True)
    ee = jnp.where(is_e, jnp.exp(el - em), 0.0)
    pe = jnp.where(is_e, ee / jnp.sum(ee, -1, keepdims=True), -1.0)
    v1 = jnp.max(pe, -1, keepdims=True)
    i1 = jnp.min(jnp.where(pe == v1, lane_f, far), -1, keepdims=True)
    pe2 = jnp.where(lane_f == i1, -1.0, pe)
    v2 = jnp.max(pe2, -1, keepdims=True)
    i2 = jnp.min(jnp.where(pe2 == v2, lane_f, far), -1, keepdims=True)
    den = v1 + v2
    w1 = v1 / den * p_top
    w2 = v2 / den * p_top
    first_low = i1 < i2
    a = jnp.minimum(i1, i2) - e_lo
    b = jnp.maximum(i1, i2) - e_lo
    pair = a * (7.0 - a) * 0.5 + (b - a - 1.0)
    cls = g_idx * float(N_PAIRS) + pair
    gate_a = jnp.where(first_low, w1, w2)
    gate_b = jnp.where(first_low, w2, w1)
    ri_ref[...] = jnp.where(lane_i == 0, cls, jnp.where(lane_i == 1, gate_a, jnp.where(lane_i == 2, gate_b, 0.0)))


def _merge(x, o_na, gu, vn, g_a, sh_f, sc_f, ws_cat, bs_tab, out_gain, w_out, w_r, b_r, tm):
    bsz, length, _ = x.shape
    mod = pl.BlockSpec((None, 1, D_MODEL), lambda b, i: (b, 0, 0))
    tok = lambda n: pl.BlockSpec((None, tm, n), lambda b, i: (b, i, 0))
    const = lambda *shape: pl.BlockSpec(shape, lambda b, i: (0,) * len(shape))
    return pl.pallas_call(
        functools.partial(_merge_kernel, tm=tm),
        grid=(bsz, length // tm),
        in_specs=[tok(D_MODEL),
                  pl.BlockSpec((None, HEAD_PAIRS, tm, LANES), lambda b, i: (b, 0, i, 0)),
                  tok(SGU_W), tok(SGU_W), mod, mod, mod,
                  const(SGU_W // LANES, SGU_CHUNK, 2 * SGU_CHUNK),
                  const(SGU_CHUNK, SGU_W),
                  const(1, D_MODEL),
                  const(D_MODEL, D_MODEL),
                  const(D_MODEL, LANES),
                  const(1, LANES)],
        out_specs=[tok(D_MODEL), tok(D_MODEL), tok(LANES)],
        out_shape=[jax.ShapeDtypeStruct((bsz, length, D_MODEL), _f32),
                   jax.ShapeDtypeStruct((bsz, length, D_MODEL), _bf16),
                   jax.ShapeDtypeStruct((bsz, length, LANES), _f32)],
        compiler_params=_params("arbitrary", "arbitrary"),
        name="merge",
    )(x, o_na, gu, vn, g_a, sh_f, sc_f, ws_cat, bs_tab, out_gain, w_out, w_r, b_r)


def _experts_kernel(ea_ref, eb_ref, nlive_ref, xs_ref, ri_ref, w13a_ref, w2a_ref, w13b_ref, w2b_ref, y_ref):
    t = pl.program_id(0)

    @pl.when(t < nlive_ref[0])
    def _():
        xs = xs_ref[...]
        ri = ri_ref[...]

        def expert(w13_ref, w2_ref):
            h = _dot(xs, w13_ref[...])
            hid = jax.nn.silu(h[:, :D_EXPERT]) * h[:, D_EXPERT:]
            return _dot(hid.astype(_bf16), w2_ref[...])

        y_ref[...] = ri[:, 1:2] * expert(w13a_ref, w2a_ref) + ri[:, 2:3] * expert(w13b_ref, w2b_ref)


def _experts(xs, ri, tile_ea, tile_eb, nlive, w13, w2, tile):
    ntiles = xs.shape[0] // tile
    wa13 = pl.BlockSpec((None, D_MODEL, 2 * D_EXPERT), lambda t, ea, eb, nl: (ea[t], 0, 0))
    wa2 = pl.BlockSpec((None, D_EXPERT, D_MODEL), lambda t, ea, eb, nl: (ea[t], 0, 0))
    wb13 = pl.BlockSpec((None, D_MODEL, 2 * D_EXPERT), lambda t, ea, eb, nl: (eb[t], 0, 0))
    wb2 = pl.BlockSpec((None, D_EXPERT, D_MODEL), lambda t, ea, eb, nl: (eb[t], 0, 0))
    grid_spec = pltpu.PrefetchScalarGridSpec(
        num_scalar_prefetch=3,
        grid=(ntiles,),
        in_specs=[pl.BlockSpec((tile, D_MODEL), lambda t, ea, eb, nl: (t, 0)),
                  pl.BlockSpec((tile, LANES), lambda t, ea, eb, nl: (t, 0)),
                  wa13, wa2, wb13, wb2],
        out_specs=pl.BlockSpec((tile, D_MODEL), lambda t, ea, eb, nl: (t, 0)),
    )
    return pl.pallas_call(
        _experts_kernel,
        grid_spec=grid_spec,
        out_shape=jax.ShapeDtypeStruct((xs.shape[0], D_MODEL), _f32),
        compiler_params=_params("arbitrary"),
        name="experts",
    )(tile_ea, tile_eb, nlive, xs, ri, w13, w2, w13, w2)


def _dispatch_plan(cls, tile):
    n = cls.shape[0]
    padded = -(-(n + N_CLASSES * (tile - 1)) // tile) * tile
    ntiles = padded // tile
    order = jnp.argsort(cls, stable=True).astype(jnp.int32)
    counts = jnp.sum((cls[:, None] == jnp.arange(N_CLASSES, dtype=jnp.int32)[None, :]).astype(jnp.int32), axis=0)
    pcounts = (counts + tile - 1) // tile * tile
    pad_end = jnp.cumsum(pcounts)
    pad_off = pad_end - pcounts
    off = jnp.cumsum(counts) - counts
    cls_sorted = cls[order]
    dest = pad_off[cls_sorted] + jnp.arange(n, dtype=jnp.int32) - off[cls_sorted]
    src = jnp.zeros((padded,), jnp.int32).at[dest].set(order)
    pos = jnp.zeros((n,), jnp.int32).at[order].set(dest)
    nlive = (pad_end[-1] // tile).astype(jnp.int32)
    tile_start = jnp.arange(ntiles, dtype=jnp.int32) * tile
    tile_cls = jnp.searchsorted(pad_end, jnp.minimum(tile_start, pad_end[-1] - 1), side="right").astype(jnp.int32)
    tile_cls = jnp.minimum(tile_cls, N_CLASSES - 1)
    grp = tile_cls // N_PAIRS
    pair = tile_cls % N_PAIRS
    tile_ea = grp * EXPERTS_PER_GROUP + jnp.asarray(PAIR_A, jnp.int32)[pair]
    tile_eb = grp * EXPERTS_PER_GROUP + jnp.asarray(PAIR_B, jnp.int32)[pair]
    return src, pos, tile_ea, tile_eb, nlive.reshape(1)


def _residual_kernel(x_ref, y_ref, g_ref, o_ref):
    o_ref[...] = x_ref[...] + g_ref[...] * y_ref[...]


def _residual(x, y, g, tm):
    bsz, length, _ = x.shape
    tok = pl.BlockSpec((None, tm, D_MODEL), lambda b, i: (b, i, 0))
    return pl.pallas_call(
        _residual_kernel,
        grid=(bsz, length // tm),
        in_specs=[tok, tok, pl.BlockSpec((None, 1, D_MODEL), lambda b, i: (b, 0, 0))],
        out_specs=tok,
        out_shape=jax.ShapeDtypeStruct(x.shape, _f32),
        compiler_params=_params("arbitrary", "arbitrary"),
        name="residual",
    )(x, y, g)


def _ffn(x_new, hf, rinfo, g_f, w13, w2, tile, tm):
    bsz, length, _ = x_new.shape
    n = bsz * length
    ri = rinfo.reshape(n, LANES)
    cls = ri[:, 0].astype(jnp.int32)
    src, pos, tile_ea, tile_eb, nlive = _dispatch_plan(cls, tile)
    xs = jnp.take(hf.reshape(n, D_MODEL), src, axis=0)
    rs = jnp.take(ri, src, axis=0)
    ys = _experts(xs, rs, tile_ea, tile_eb, nlive, w13, w2, tile)
    y = jnp.take(ys, pos, axis=0).reshape(bsz, length, D_MODEL)
    return _residual(x_new, y, g_f, tm)


def kernel(x, c, ctx, c_ctx, w_ada, b_ada, w_in, q_gain, k_gain, rpb, sgu_ln, sgu_w, sgu_b, out_gain,
           w_out, rg_w, rg_b, re_w, re_b, w1, w3, w2):
    bsz, seq, _ = x.shape
    depth = w_ada.shape[0]
    rows = seq // GRID_W
    assert seq % Q_BLOCK == 0 and rows >= WIN_ROWS and ctx.shape[1] % SGU_CHUNK == 0
    tm_x, tm_c = 512, ctx.shape[1]

    cond = jnp.zeros((8, D_MODEL), _f32).at[:bsz].set(c).at[bsz].set(c_ctx)
    mods = _adaln(cond, w_ada, b_ada).reshape(depth, 8, N_MOD, D_MODEL)

    head_id = np.arange(NA_W) // NA_HEAD_DIM
    avg = jnp.asarray((head_id[:, None] == head_id[None, :]).astype(np.float32) / NA_HEAD_DIM, _bf16)

    for i in range(depth):
        last = i == depth - 1
        mx = [mods[i, :bsz, j][:, None, :] for j in range(N_MOD)]
        mc = [jnp.broadcast_to(mods[i, bsz, j][None, None, :], (bsz, 1, D_MODEL)) for j in range(N_MOD)]
        w_in_b = w_in[i].astype(_bf16)
        qg = (jnp.tile(q_gain[i], NA_HEADS) * NA_HEAD_DIM ** -0.5)[None, :]
        kg = jnp.tile(k_gain[i], NA_HEADS)[None, :]
        ln_g = sgu_ln[i][None, :]
        ws_cat = jnp.concatenate([sgu_w[i, 0::2], sgu_w[i, 1::2]], axis=2).astype(_bf16)
        bs_tab = jnp.repeat(sgu_b[i].T, SGU_GROUP_DIM, axis=1)
        og = out_gain[i][None, :]
        w_out_b = w_out[i].astype(_bf16)
        w_r = jnp.concatenate([rg_w[i], jnp.transpose(re_w[i], (1, 0, 2)).reshape(D_MODEL, N_EXPERTS),
                               jnp.zeros((D_MODEL, LANES - N_GROUPS - N_EXPERTS), _f32)], axis=1).astype(_bf16)
        b_r = jnp.concatenate([rg_b[i], re_b[i].reshape(-1),
                               jnp.zeros((LANES - N_GROUPS - N_EXPERTS,), _f32)])[None, :]
        w13 = jnp.concatenate([w1[i], w3[i]], axis=2).astype(_bf16)
        w2_b = w2[i].astype(_bf16)
        bias = _window_bias(rpb[i], rows)

        q, k, v, gu, vn = _proj(x, mx[0], mx[1], w_in_b, avg, qg, kg, ln_g, tm_x)
        qc, kc, vc, guc, vnc = _proj(ctx, mc[0], mc[1], w_in_b, avg, qg, kg, ln_g, tm_c)
        o_na = _attention(q, k, v, kc, vc, bias)
        x_new, hf, rinfo = _merge(x, o_na, gu, vn, mx[2], mx[3], mx[4], ws_cat, bs_tab, og, w_out_b,
                                  w_r, b_r, tm_x)
        x = _ffn(x_new, hf, rinfo, mx[5], w13, w2_b, 256, tm_x)
        if not last:
            oc_na = _ctx_attention(qc, kc, vc)
            c_new, hfc, rinfoc = _merge(ctx, oc_na, guc, vnc, mc[2], mc[3], mc[4], ws_cat, bs_tab, og,
                                        w_out_b, w_r, b_r, tm_c)
            ctx = _ffn(c_new, hfc, rinfoc, mc[5], w13, w2_b, 64, tm_c)
    return x
```

```python
import functools

import jax
import jax.numpy as jnp
import numpy as np
from jax import lax
from jax.experimental import pallas as pl
from jax.experimental.pallas import tpu as pltpu

D_MODEL = 1024
GRID_W = 64
NA_HEADS = 8
NA_HEAD_DIM = 64
NA_W = NA_HEADS * NA_HEAD_DIM
NA_KH = 8
NA_KW = 16
SGU_GROUPS = 8
SGU_W = D_MODEL - NA_W
SGU_GROUP_DIM = SGU_W // SGU_GROUPS
SGU_CHUNK = 128
IN_COLS = 3 * NA_W + 2 * SGU_W
N_GROUPS = 4
EXPERTS_PER_GROUP = 4
N_EXPERTS = N_GROUPS * EXPERTS_PER_GROUP
D_EXPERT = D_MODEL // 2
N_MOD = 6
EPS = 1e-6

LANES = 128
SUBLANES = 8
HEAD_PAIRS = NA_W // LANES
Q_ROWS = 8
Q_BLOCK = Q_ROWS * GRID_W
WIN_ROWS = Q_ROWS + NA_KH - 1
WIN_BLOCK = WIN_ROWS * GRID_W
N_DR = 2 * NA_KH
N_PAIRS = 6
N_CLASSES = N_GROUPS * N_PAIRS
PAIR_A = (0, 0, 0, 1, 1, 2)
PAIR_B = (1, 2, 3, 2, 3, 3)
ROW_W = D_MODEL + LANES
MASKED = -1e30
VMEM_LIMIT_BYTES = 56 * 1024 * 1024

_f32 = jnp.float32
_bf16 = jnp.bfloat16


def _params(*semantics):
    return pltpu.CompilerParams(dimension_semantics=semantics, vmem_limit_bytes=VMEM_LIMIT_BYTES)


def _dot(a, b):
    return jnp.dot(a, b, preferred_element_type=_f32)


def _dot_nt(a, b):
    return lax.dot_general(a, b, (((1,), (1,)), ((), ())), preferred_element_type=_f32)


def _adaln_kernel(cond_ref, w_ref, b_ref, out_ref):
    cond = cond_ref[...]
    act = cond * jax.nn.sigmoid(cond)
    out_ref[...] = _dot(act.astype(_bf16), w_ref[...].astype(_bf16)) + b_ref[...]


def _adaln(cond, w_ada, b_ada):
    depth = w_ada.shape[0]
    cols = N_MOD * D_MODEL
    col_block = 1536
    return pl.pallas_call(
        _adaln_kernel,
        grid=(depth, cols // col_block),
        in_specs=[
            pl.BlockSpec((SUBLANES, D_MODEL), lambda d, j: (0, 0)),
            pl.BlockSpec((None, D_MODEL, col_block), lambda d, j: (d, 0, j)),
            pl.BlockSpec((None, 1, col_block), lambda d, j: (d, 0, j)),
        ],
        out_specs=pl.BlockSpec((None, SUBLANES, col_block), lambda d, j: (d, 0, j)),
        out_shape=jax.ShapeDtypeStruct((depth, SUBLANES, cols), _f32),
        compiler_params=_params("arbitrary", "arbitrary"),
        name="adaln",
    )(cond, w_ada, b_ada.reshape(depth, 1, cols))


def _proj_kernel(x_ref, sh_ref, sc_ref, w_ref, avg_ref, qg_ref, kg_ref, ln_ref,
                 q_ref, k_ref, v_ref, gu_ref, vn_ref):
    x = x_ref[...]
    h = x * lax.rsqrt(jnp.mean(x * x, -1, keepdims=True) + EPS) * (1.0 + sc_ref[...]) + sh_ref[...]
    p = _dot(h.astype(_bf16), w_ref[...])
    q = p[:, :NA_W]
    k = p[:, NA_W:2 * NA_W]
    v = p[:, 2 * NA_W:3 * NA_W]
    u = p[:, 3 * NA_W:3 * NA_W + SGU_W]
    vs = p[:, 3 * NA_W + SGU_W:]
    q_ms = _dot((q * q).astype(_bf16), avg_ref[...])
    k_ms = _dot((k * k).astype(_bf16), avg_ref[...])
    qn = (q * lax.rsqrt(q_ms + EPS) * qg_ref[...]).astype(_bf16)
    kn = (k * lax.rsqrt(k_ms + EPS) * kg_ref[...]).astype(_bf16)
    vb = v.astype(_bf16)
    for hp in range(HEAD_PAIRS):
        cols = slice(hp * LANES, (hp + 1) * LANES)
        q_ref[hp] = qn[:, cols]
        k_ref[hp] = kn[:, cols]
        v_ref[hp] = vb[:, cols]
    gu_ref[...] = jax.nn.gelu(u).astype(_bf16)
    gv = jax.nn.gelu(vs)
    mu = jnp.mean(gv, -1, keepdims=True)
    cen = gv - mu
    var = jnp.mean(cen * cen, -1, keepdims=True)
    vn_ref[...] = (cen * lax.rsqrt(var + EPS) * ln_ref[...]).astype(_bf16)


def _proj(x, sh, sc, w_in, avg, qg, kg, ln_g, tm):
    bsz, length, _ = x.shape
    vec = lambda n: pl.BlockSpec((1, n), lambda b, i: (0, 0))
    mod = pl.BlockSpec((None, 1, D_MODEL), lambda b, i: (b, 0, 0))
    tok = lambda n: pl.BlockSpec((None, tm, n), lambda b, i: (b, i, 0))
    hpm = pl.BlockSpec((None, HEAD_PAIRS, tm, LANES), lambda b, i: (b, 0, i, 0))
    hpm_shape = jax.ShapeDtypeStruct((bsz, HEAD_PAIRS, length, LANES), _bf16)
    sgu_shape = jax.ShapeDtypeStruct((bsz, length, SGU_W), _bf16)
    return pl.pallas_call(
        _proj_kernel,
        grid=(bsz, length // tm),
        in_specs=[tok(D_MODEL), mod, mod,
                  pl.BlockSpec((D_MODEL, IN_COLS), lambda b, i: (0, 0)),
                  pl.BlockSpec((NA_W, NA_W), lambda b, i: (0, 0)),
                  vec(NA_W), vec(NA_W), vec(SGU_W)],
        out_specs=[hpm, hpm, hpm, tok(SGU_W), tok(SGU_W)],
        out_shape=[hpm_shape, hpm_shape, hpm_shape, sgu_shape, sgu_shape],
        compiler_params=_params("arbitrary", "arbitrary"),
        name="proj",
    )(x, sh, sc, w_in, avg, qg, kg, ln_g)


def _softmax_pv(q, keys, values, biases):
    lane = lax.broadcasted_iota(jnp.int32, q.shape, 1)
    outs = []
    for hh in range(2):
        in_head = (lane >= hh * NA_HEAD_DIM) & (lane < (hh + 1) * NA_HEAD_DIM)
        qm = jnp.where(in_head, q, jnp.zeros_like(q))
        scores = []
        for key, bias in zip(keys, biases):
            s = _dot_nt(qm, key)
            scores.append(s if bias is None else s + bias[hh])
        m = functools.reduce(jnp.maximum, [jnp.max(s, -1, keepdims=True) for s in scores])
        probs = [jnp.exp(s - m) for s in scores]
        denom = functools.reduce(lambda a, b: a + b, [jnp.sum(p, -1, keepdims=True) for p in probs])
        acc = functools.reduce(lambda a, b: a + b,
                               [_dot(p.astype(_bf16), val) for p, val in zip(probs, values)])
        outs.append(acc / denom)
    lane_o = lax.broadcasted_iota(jnp.int32, outs[0].shape, 1)
    return jnp.where(lane_o < NA_HEAD_DIM, outs[0], outs[1])


def _attn_kernel(dr_ref, q_ref, k_ref, v_ref, kc_ref, vc_ref, bycol_ref, o_ref, bias_ref, *, nblk):
    i = pl.program_id(2)

    @pl.when((i == 0) | (i == 1) | (i == nblk - 1))
    def _():
        var = jnp.where(i == 0, 0, jnp.where(i == nblk - 1, 2, 1))
        for hh in range(2):
            for rq in range(Q_ROWS):
                for kr in range(WIN_ROWS):
                    idx = dr_ref[var * (Q_ROWS * WIN_ROWS) + rq * WIN_ROWS + kr]
                    lo = (kr % 2) * GRID_W
                    bias_ref[hh, rq * GRID_W:(rq + 1) * GRID_W, kr * GRID_W:(kr + 1) * GRID_W] = (
                        bycol_ref[hh, idx, :, lo:lo + GRID_W])

    o_ref[...] = _softmax_pv(q_ref[...], [k_ref[...], kc_ref[...]], [v_ref[...], vc_ref[...]],
                             [bias_ref, None]).astype(o_ref.dtype)


def _attention(q, k, v, kc, vc, bycol, dr_idx):
    bsz, _, seq, _ = q.shape
    clen = kc.shape[2]
    nblk = seq // Q_BLOCK

    def win_start(i):
        return jnp.clip(i * Q_ROWS - NA_KH // 2, 0, seq // GRID_W - WIN_ROWS) * GRID_W

    qspec = pl.BlockSpec((None, None, Q_BLOCK, LANES), lambda hp, b, i, dr: (b, hp, i, 0))
    wspec = pl.BlockSpec((None, None, pl.Element(WIN_BLOCK), pl.Element(LANES)),
                         lambda hp, b, i, dr: (b, hp, win_start(i), 0))
    cspec = pl.BlockSpec((None, None, clen, LANES), lambda hp, b, i, dr: (b, hp, 0, 0))
    tspec = pl.BlockSpec((None, 2, N_DR, GRID_W, LANES), lambda hp, b, i, dr: (hp, 0, 0, 0, 0))
    grid_spec = pltpu.PrefetchScalarGridSpec(
        num_scalar_prefetch=1,
        grid=(HEAD_PAIRS, bsz, nblk),
        in_specs=[qspec, wspec, wspec, cspec, cspec, tspec],
        out_specs=qspec,
        scratch_shapes=[pltpu.VMEM((2, Q_BLOCK, WIN_BLOCK), _f32)],
    )
    return pl.pallas_call(
        functools.partial(_attn_kernel, nblk=nblk),
        grid_spec=grid_spec,
        out_shape=jax.ShapeDtypeStruct(q.shape, _bf16),
        compiler_params=_params("arbitrary", "arbitrary", "arbitrary"),
        name="attention",
    )(dr_idx, q, k, v, kc, vc, bycol)


def _ctx_attn_kernel(q_ref, k_ref, v_ref, o_ref):
    o_ref[...] = _softmax_pv(q_ref[...], [k_ref[...]], [v_ref[...]], [None]).astype(o_ref.dtype)


def _ctx_attention(q, k, v):
    bsz, _, clen, _ = q.shape
    spec = pl.BlockSpec((None, None, clen, LANES), lambda b, hp: (b, hp, 0, 0))
    return pl.pallas_call(
        _ctx_attn_kernel,
        grid=(bsz, HEAD_PAIRS),
        in_specs=[spec, spec, spec],
        out_specs=spec,
        out_shape=jax.ShapeDtypeStruct(q.shape, _bf16),
        compiler_params=_params("arbitrary", "arbitrary"),
        name="ctx_attention",
    )(q, k, v)


def _window_row_table(rows):
    nblk = rows // Q_ROWS
    dr_idx = np.full((3, Q_ROWS, WIN_ROWS), N_DR - 1, np.int32)
    for var, blk in enumerate((0, 1, nblk - 1)):
        r0 = blk * Q_ROWS
        ws = int(np.clip(r0 - NA_KH // 2, 0, rows - WIN_ROWS))
        for rq in range(Q_ROWS):
            r = r0 + rq
            r_start = int(np.clip(r - NA_KH // 2, 0, rows - NA_KH))
            for kr in range(WIN_ROWS):
                if r_start <= ws + kr < r_start + NA_KH:
                    dr_idx[var, rq, kr] = ws + kr - r + NA_KH - 1
    return jnp.asarray(dr_idx.reshape(-1))


def _window_col_table(rpb):
    col = np.arange(GRID_W)
    col_start = np.clip(col - NA_KW // 2, 0, GRID_W - NA_KW)
    col_ok = (col[None, :] >= col_start[:, None]) & (col[None, :] < col_start[:, None] + NA_KW)
    dc = np.where(col_ok, col[None, :] - col[:, None] + NA_KW - 1, 0)
    onehot = np.zeros((2 * NA_KW - 1, GRID_W * GRID_W), np.float32)
    onehot[dc.reshape(-1), np.arange(GRID_W * GRID_W)] = col_ok.reshape(-1)
    by_col = jnp.dot(rpb.reshape(-1, 2 * NA_KW - 1), jnp.asarray(onehot), precision=lax.Precision.HIGHEST)
    by_col = by_col.reshape(NA_HEADS, 2 * NA_KH - 1, GRID_W, GRID_W)
    by_col = jnp.where(jnp.asarray(col_ok)[None, None], by_col, MASKED)
    by_col = jnp.concatenate([by_col, jnp.full((NA_HEADS, 1, GRID_W, GRID_W), MASKED, _f32)], axis=1)
    by_col = jnp.concatenate([by_col, by_col], axis=-1)
    return by_col.reshape(HEAD_PAIRS, 2, N_DR, GRID_W, LANES)


def _merge_kernel(x_ref, o_ref, gu_ref, vn_ref, ga_ref, shf_ref, scf_ref, ws_ref, bs_ref, og_ref,
                  wout_ref, wr_ref, br_ref, tri_ref, xn_ref, row_ref, meta_ref, cnt_ref, *, tm):
    first = (pl.program_id(0) == 0) & (pl.program_id(1) == 0)

    @pl.when(first)
    def _():
        cnt_ref[...] = jnp.zeros_like(cnt_ref)

    nchunk = tm // SGU_CHUNK
    vn = vn_ref[...]
    lane = lax.broadcasted_iota(jnp.int32, (SGU_CHUNK, LANES), 1)
    low = lane < SGU_GROUP_DIM
    mixed_cols = []
    for gp in range(SGU_W // LANES):
        rhs_cols = []
        for n in range(nchunk):
            blk = vn[n * SGU_CHUNK:(n + 1) * SGU_CHUNK, gp * LANES:(gp + 1) * LANES]
            zero = jnp.zeros_like(blk)
            rhs_cols.append(jnp.concatenate([jnp.where(low, blk, zero), jnp.where(low, zero, blk)], axis=0))
        rhs = jnp.concatenate(rhs_cols, axis=1)
        mix = _dot(ws_ref[gp], rhs)
        mixed_cols.append(jnp.concatenate([mix[:, n * LANES:(n + 1) * LANES] for n in range(nchunk)], axis=0))
    mixed = jnp.concatenate(mixed_cols, axis=1)
    bias = jnp.concatenate([bs_ref[...]] * nchunk, axis=0)
    o_sg = gu_ref[...].astype(_f32) * (mixed + bias)
    o_na = jnp.concatenate([o_ref[hp].astype(_f32) for hp in range(HEAD_PAIRS)], axis=1)

    def rms(t):
        return t * lax.rsqrt(jnp.mean(t * t, -1, keepdims=True) + EPS)

    y = jnp.concatenate([rms(o_na), rms(o_sg)], axis=1) * og_ref[...]
    xn = x_ref[...] + ga_ref[...] * _dot(y.astype(_bf16), wout_ref[...])
    xn_ref[...] = xn
    hf = rms(xn) * (1.0 + scf_ref[...]) + shf_ref[...]
    row_ref[:, :D_MODEL] = hf

    lg = _dot(hf.astype(_bf16), wr_ref[...]) + br_ref[...]
    lane_i = lax.broadcasted_iota(jnp.int32, lg.shape, 1)
    lane_f = lane_i.astype(_f32)
    far = float(LANES)
    is_g = lane_i < N_GROUPS
    gl = jnp.where(is_g, lg, MASKED)
    gm = jnp.max(gl, -1, keepdims=True)
    p_top = 1.0 / jnp.sum(jnp.where(is_g, jnp.exp(gl - gm), 0.0), -1, keepdims=True)
    g_idx = jnp.min(jnp.where(is_g & (gl == gm), lane_f, far), -1, keepdims=True)
    e_lo = N_GROUPS + EXPERTS_PER_GROUP * g_idx
    is_e = (lane_f >= e_lo) & (lane_f < e_lo + EXPERTS_PER_GROUP)
    el = jnp.where(is_e, lg, MASKED)
    em = jnp.max(el, -1, keepdims=True)
    ee = jnp.where(is_e, jnp.exp(el - em), 0.0)
    pe = jnp.where(is_e, ee / jnp.sum(ee, -1, keepdims=True), -1.0)
    v1 = jnp.max(pe, -1, keepdims=True)
    i1 = jnp.min(jnp.where(pe == v1, lane_f, far), -1, keepdims=True)
    pe2 = jnp.where(lane_f == i1, -1.0, pe)
    v2 = jnp.max(pe2, -1, keepdims=True)
    i2 = jnp.min(jnp.where(pe2 == v2, lane_f, far), -1, keepdims=True)
    den = v1 + v2
    w1 = v1 / den * p_top
    w2 = v2 / den * p_top
    first_low = i1 < i2
    a = jnp.minimum(i1, i2) - e_lo
    b = jnp.maximum(i1, i2) - e_lo
    pair = a * (7.0 - a) * 0.5 + (b - a - 1.0)
    cls = g_idx * float(N_PAIRS) + pair
    gate_a = jnp.where(first_low, w1, w2)
    gate_b = jnp.where(first_low, w2, w1)
    row_ref[:, D_MODEL:] = jnp.where(lane_i == 0, gate_a, jnp.where(lane_i == 1, gate_b, 0.0))

    onehot = lane_f == cls
    before = _dot(tri_ref[...], jnp.where(onehot, 1.0, 0.0).astype(_bf16))
    rank = jnp.sum(jnp.where(onehot, before + cnt_ref[...], 0.0), -1, keepdims=True)
    cnt_ref[...] += jnp.sum(jnp.where(onehot, 1.0, 0.0), 0, keepdims=True)
    meta = jnp.where(lane_i == 0, cls, jnp.where(lane_i == 1, rank, 0.0))
    meta_ref[...] = jnp.transpose(meta)[:SUBLANES, :]


def _merge(x, o_na, gu, vn, g_a, sh_f, sc_f, ws_cat, bs_tab, out_gain, w_out, w_r, b_r, tri, tm):
    bsz, length, _ = x.shape
    mod = pl.BlockSpec((None, 1, D_MODEL), lambda b, i: (b, 0, 0))
    tok = lambda n: pl.BlockSpec((None, tm, n), lambda b, i: (b, i, 0))
    const = lambda *shape: pl.BlockSpec(shape, lambda b, i: (0,) * len(shape))
    return pl.pallas_call(
        functools.partial(_merge_kernel, tm=tm),
        grid=(bsz, length // tm),
        in_specs=[tok(D_MODEL),
                  pl.BlockSpec((None, HEAD_PAIRS, tm, LANES), lambda b, i: (b, 0, i, 0)),
                  tok(SGU_W), tok(SGU_W), mod, mod, mod,
                  const(SGU_W // LANES, SGU_CHUNK, 2 * SGU_CHUNK),
                  const(SGU_CHUNK, SGU_W),
                  const(1, D_MODEL),
                  const(D_MODEL, D_MODEL),
                  const(D_MODEL, LANES),
                  const(1, LANES),
                  const(tm, tm)],
        out_specs=[tok(D_MODEL), tok(ROW_W),
                   pl.BlockSpec((None, SUBLANES, tm), lambda b, i: (b, 0, i)),
                   const(1, LANES)],
        out_shape=[jax.ShapeDtypeStruct((bsz, length, D_MODEL), _f32),
                   jax.ShapeDtypeStruct((bsz, length, ROW_W), _f32),
                   jax.ShapeDtypeStruct((bsz, SUBLANES, length), _f32),
                   jax.ShapeDtypeStruct((1, LANES), _f32)],
        compiler_params=_params("arbitrary", "arbitrary"),
        name="merge",
    )(x, o_na, gu, vn, g_a, sh_f, sc_f, ws_cat, bs_tab, out_gain, w_out, w_r, b_r, tri)


def _dispatch_kernel(pos_ref, rows_ref, init_ref, xs_ref, sem, *, td):
    del init_ref

    def issue(j, carry):
        pltpu.make_async_copy(rows_ref.at[pl.ds(j, 1)], xs_ref.at[pl.ds(pos_ref[0, j], 1)], sem).start()
        return carry

    lax.fori_loop(0, td, issue, 0, unroll=8)
    pltpu.make_async_copy(rows_ref, xs_ref.at[pl.ds(0, td)], sem).wait()


def _dispatch(rows, pos, padded, td):
    n = rows.shape[0]
    return pl.pallas_call(
        functools.partial(_dispatch_kernel, td=td),
        grid=(n // td,),
        in_specs=[pl.BlockSpec((None, 1, td), lambda i: (i, 0, 0), memory_space=pltpu.SMEM),
                  pl.BlockSpec((td, ROW_W), lambda i: (i, 0)),
                  pl.BlockSpec(memory_space=pl.ANY)],
        out_specs=pl.BlockSpec(memory_space=pl.ANY),
        out_shape=jax.ShapeDtypeStruct((padded, ROW_W), _f32),
        scratch_shapes=[pltpu.SemaphoreType.DMA(())],
        input_output_aliases={2: 0},
        compiler_params=_params("arbitrary"),
        name="dispatch",
    )(pos.reshape(n // td, 1, td), rows, jnp.zeros((padded, ROW_W), _f32))


def _combine_kernel(pos_ref, x_ref, g_ref, ys_ref, o_ref, buf_ref, sem, *, tc):
    def issue(j, carry):
        pltpu.make_async_copy(ys_ref.at[pl.ds(pos_ref[0, j], 1)], buf_ref.at[pl.ds(j, 1)], sem).start()
        return carry

    lax.fori_loop(0, tc, issue, 0, unroll=8)
    pltpu.make_async_copy(ys_ref.at[pl.ds(0, tc)], buf_ref, sem).wait()
    o_ref[...] = x_ref[...] + g_ref[...] * buf_ref[...]


def _combine(x, ys, pos, g, tc):
    bsz, length, _ = x.shape
    per_batch = length // tc
    tok = pl.BlockSpec((None, tc, D_MODEL), lambda b, i: (b, i, 0))
    return pl.pallas_call(
        functools.partial(_combine_kernel, tc=tc),
        grid=(bsz, per_batch),
        in_specs=[pl.BlockSpec((None, 1, tc), lambda b, i: (b * per_batch + i, 0, 0), memory_space=pltpu.SMEM),
                  tok,
                  pl.BlockSpec((None, 1, D_MODEL), lambda b, i: (b, 0, 0)),
                  pl.BlockSpec(memory_space=pl.ANY)],
        out_specs=tok,
        out_shape=jax.ShapeDtypeStruct(x.shape, _f32),
        scratch_shapes=[pltpu.VMEM((tc, D_MODEL), _f32), pltpu.SemaphoreType.DMA(())],
        compiler_params=_params("arbitrary", "arbitrary"),
        name="combine",
    )(pos.reshape(bsz * per_batch, 1, tc), x, g, ys)


def _experts_kernel(ea_ref, eb_ref, chg_ref, nlive_ref, xs_ref, w1a_ref, w3a_ref, w2a_ref,
                    w1b_ref, w3b_ref, w2b_ref, y_ref, c1a, c3a, c2a, c1b, c3b, c2b):
    t = pl.program_id(0)
    live = t < nlive_ref[0]

    @pl.when(live & (chg_ref[t] == 1))
    def _():
        for src, dst in ((w1a_ref, c1a), (w3a_ref, c3a), (w2a_ref, c2a),
                         (w1b_ref, c1b), (w3b_ref, c3b), (w2b_ref, c2b)):
            dst[...] = src[...].astype(_bf16)

    @pl.when(live)
    def _():
        xs = xs_ref[:, :D_MODEL].astype(_bf16)
        gates = xs_ref[:, D_MODEL:]

        def expert(w1_ref, w3_ref, w2_ref):
            hid = jax.nn.silu(_dot(xs, w1_ref[...])) * _dot(xs, w3_ref[...])
            return _dot(hid.astype(_bf16), w2_ref[...])

        y_ref[...] = gates[:, 0:1] * expert(c1a, c3a, c2a) + gates[:, 1:2] * expert(c1b, c3b, c2b)

    @pl.when(jnp.logical_not(live))
    def _():
        y_ref[...] = jnp.zeros_like(y_ref)


def _experts(xs, tile_ea, tile_eb, chg, nlive, w1, w3, w2, layer, tile):
    ntiles = xs.shape[0] // tile
    up_a = pl.BlockSpec((None, None, D_MODEL, D_EXPERT), lambda t, ea, eb, cg, nl: (layer, ea[t], 0, 0))
    dn_a = pl.BlockSpec((None, None, D_EXPERT, D_MODEL), lambda t, ea, eb, cg, nl: (layer, ea[t], 0, 0))
    up_b = pl.BlockSpec((None, None, D_MODEL, D_EXPERT), lambda t, ea, eb, cg, nl: (layer, eb[t], 0, 0))
    dn_b = pl.BlockSpec((None, None, D_EXPERT, D_MODEL), lambda t, ea, eb, cg, nl: (layer, eb[t], 0, 0))
    grid_spec = pltpu.PrefetchScalarGridSpec(
        num_scalar_prefetch=4,
        grid=(ntiles,),
        in_specs=[pl.BlockSpec((tile, ROW_W), lambda t, ea, eb, cg, nl: (t, 0)),
                  up_a, up_a, dn_a, up_b, up_b, dn_b],
        out_specs=pl.BlockSpec((tile, D_MODEL), lambda t, ea, eb, cg, nl: (t, 0)),
        scratch_shapes=[pltpu.VMEM((D_MODEL, D_EXPERT), _bf16), pltpu.VMEM((D_MODEL, D_EXPERT), _bf16),
                        pltpu.VMEM((D_EXPERT, D_MODEL), _bf16)] * 2,
    )
    return pl.pallas_call(
        _experts_kernel,
        grid_spec=grid_spec,
        out_shape=jax.ShapeDtypeStruct((xs.shape[0], D_MODEL), _f32),
        compiler_params=_params("arbitrary"),
        name="experts",
    )(tile_ea, tile_eb, chg, nlive, xs, w1, w3, w2, w1, w3, w2)


def _dispatch_plan(cls, rank, counts, tile):
    n = cls.shape[0]
    padded = -(-(n + N_CLASSES * (tile - 1)) // tile) * tile
    ntiles = padded // tile
    pcounts = (counts + tile - 1) // tile * tile
    pad_end = jnp.cumsum(pcounts)
    pad_off = pad_end - pcounts
    pos = pad_off[cls] + rank
    nlive = pad_end[-1] // tile
    tile_start = jnp.arange(ntiles, dtype=jnp.int32) * tile
    tile_cls = jnp.searchsorted(pad_end, jnp.minimum(tile_start, pad_end[-1] - 1), side="right").astype(jnp.int32)
    tile_cls = jnp.minimum(tile_cls, N_CLASSES - 1)
    chg = jnp.concatenate([jnp.ones((1,), jnp.int32), (tile_cls[1:] != tile_cls[:-1]).astype(jnp.int32)])
    grp = tile_cls // N_PAIRS
    pair = tile_cls % N_PAIRS
    tile_ea = grp * EXPERTS_PER_GROUP + jnp.asarray(PAIR_A, jnp.int32)[pair]
    tile_eb = grp * EXPERTS_PER_GROUP + jnp.asarray(PAIR_B, jnp.int32)[pair]
    return pos.astype(jnp.int32), padded, tile_ea, tile_eb, chg, nlive.astype(jnp.int32).reshape(1)


def _ffn(x_new, rows, meta, counts, g_f, w1, w3, w2, layer, tile, tdma):
    bsz, length, _ = x_new.shape
    n = bsz * length
    cls = meta[:, 0, :].reshape(n).astype(jnp.int32)
    rank = meta[:, 1, :].reshape(n).astype(jnp.int32)
    cnt = counts[0, :N_CLASSES].astype(jnp.int32)
    pos, padded, tile_ea, tile_eb, chg, nlive = _dispatch_plan(cls, rank, cnt, tile)
    xs = _dispatch(rows.reshape(n, ROW_W), pos, padded, tdma)
    ys = _experts(xs, tile_ea, tile_eb, chg, nlive, w1, w3, w2, layer, tile)
    return _combine(x_new, ys, pos, g_f, tdma)


def kernel(x, c, ctx, c_ctx, w_ada, b_ada, w_in, q_gain, k_gain, rpb, sgu_ln, sgu_w, sgu_b, out_gain,
           w_out, rg_w, rg_b, re_w, re_b, w1, w3, w2):
    bsz, seq, _ = x.shape
    depth = w_ada.shape[0]
    rows = seq // GRID_W
    assert seq % Q_BLOCK == 0 and rows // Q_ROWS >= 3 and ctx.shape[1] % SGU_CHUNK == 0
    tm_x, tm_c = 512, ctx.shape[1]

    cond = jnp.zeros((SUBLANES, D_MODEL), _f32).at[:bsz].set(c).at[bsz].set(c_ctx)
    mods = _adaln(cond, w_ada, b_ada).reshape(depth, SUBLANES, N_MOD, D_MODEL)

    head_id = np.arange(NA_W) // NA_HEAD_DIM
    avg = jnp.asarray((head_id[:, None] == head_id[None, :]).astype(np.float32) / NA_HEAD_DIM, _bf16)
    dr_idx = _window_row_table(rows)
    tri = {tm: jnp.asarray(np.tril(np.ones((tm, tm), np.float32), -1), _bf16) for tm in {tm_x, tm_c}}

    for i in range(depth):
        last = i == depth - 1
        mx = [mods[i, :bsz, j][:, None, :] for j in range(N_MOD)]
        mc = [jnp.broadcast_to(mods[i, bsz, j][None, None, :], (bsz, 1, D_MODEL)) for j in range(N_MOD)]
        w_in_b = w_in[i].astype(_bf16)
        qg = (jnp.tile(q_gain[i], NA_HEADS) * NA_HEAD_DIM ** -0.5)[None, :]
        kg = jnp.tile(k_gain[i], NA_HEADS)[None, :]
        ln_g = sgu_ln[i][None, :]
        ws_cat = jnp.concatenate([sgu_w[i, 0::2], sgu_w[i, 1::2]], axis=2).astype(_bf16)
        bs_tab = jnp.repeat(sgu_b[i].T, SGU_GROUP_DIM, axis=1)
        og = out_gain[i][None, :]
        w_out_b = w_out[i].astype(_bf16)
        w_r = jnp.concatenate([rg_w[i], jnp.transpose(re_w[i], (1, 0, 2)).reshape(D_MODEL, N_EXPERTS),
                               jnp.zeros((D_MODEL, LANES - N_GROUPS - N_EXPERTS), _f32)], axis=1).astype(_bf16)
        b_r = jnp.concatenate([rg_b[i], re_b[i].reshape(-1),
                               jnp.zeros((LANES - N_GROUPS - N_EXPERTS,), _f32)])[None, :]
        bycol = _window_col_table(rpb[i])

        q, k, v, gu, vn = _proj(x, mx[0], mx[1], w_in_b, avg, qg, kg, ln_g, tm_x)
        qc, kc, vc, guc, vnc = _proj(ctx, mc[0], mc[1], w_in_b, avg, qg, kg, ln_g, tm_c)
        o_na = _attention(q, k, v, kc, vc, bycol, dr_idx)
        x_new, xrows, meta, counts = _merge(x, o_na, gu, vn, mx[2], mx[3], mx[4], ws_cat, bs_tab, og,
                                            w_out_b, w_r, b_r, tri[tm_x], tm_x)
        x = _ffn(x_new, xrows, meta, counts, mx[5], w1, w3, w2, i, 256, 1024)
        if not last:
            oc_na = _ctx_attention(qc, kc, vc)
            c_new, crows, cmeta, ccounts = _merge(ctx, oc_na, guc, vnc, mc[2], mc[3], mc[4], ws_cat, bs_tab,
                                                  og, w_out_b, w_r, b_r, tri[tm_c], tm_c)
            ctx = _ffn(c_new, crows, cmeta, ccounts, mc[5], w1, w3, w2, i, 64, tm_c)
    return x
```

```python
import functools

import jax
import jax.numpy as jnp
import numpy as np
from jax import lax
from jax.experimental import pallas as pl
from jax.experimental.pallas import tpu as pltpu

D_MODEL = 1024
GRID_W = 64
NA_HEADS = 8
NA_HEAD_DIM = 64
NA_W = NA_HEADS * NA_HEAD_DIM
NA_KH = 8
NA_KW = 16
SGU_GROUPS = 8
SGU_W = D_MODEL - NA_W
SGU_GROUP_DIM = SGU_W // SGU_GROUPS
SGU_CHUNK = 128
IN_COLS = 3 * NA_W + 2 * SGU_W
N_GROUPS = 4
EXPERTS_PER_GROUP = 4
N_EXPERTS = N_GROUPS * EXPERTS_PER_GROUP
D_EXPERT = D_MODEL // 2
N_MOD = 6
EPS = 1e-6

LANES = 128
SUBLANES = 8
HEAD_PAIRS = NA_W // LANES
Q_ROWS = 8
Q_BLOCK = Q_ROWS * GRID_W
WIN_ROWS = Q_ROWS + NA_KH - 1
WIN_BLOCK = WIN_ROWS * GRID_W
PAIR_ROWS = 2
PAIR_BLOCK = PAIR_ROWS * GRID_W
PAIR_WIN_ROWS = PAIR_ROWS + NA_KH - 1
PAIR_WIN = PAIR_WIN_ROWS * GRID_W
PAIRS_IN_FLIGHT = 2
N_DR = 2 * NA_KH
N_PAIRS = 6
N_CLASSES = N_GROUPS * N_PAIRS
PAIR_A = (0, 0, 1, 1, 0, 2)
PAIR_B = (1, 2, 2, 3, 3, 3)
PAIR_OF_LEX = (0, 1, 4, 2, 3, 5)
ROW_W = D_MODEL + LANES
MASKED = -1e30
VMEM_LIMIT_BYTES = 56 * 1024 * 1024

_f32 = jnp.float32
_bf16 = jnp.bfloat16


def _params(*semantics):
    return pltpu.CompilerParams(dimension_semantics=semantics, vmem_limit_bytes=VMEM_LIMIT_BYTES)


def _dot(a, b):
    return jnp.dot(a, b, preferred_element_type=_f32)


def _dot_nt(a, b):
    return lax.dot_general(a, b, (((1,), (1,)), ((), ())), preferred_element_type=_f32)


def _adaln_kernel(cond_ref, w_ref, b_ref, out_ref):
    cond = cond_ref[...]
    act = cond * jax.nn.sigmoid(cond)
    out_ref[...] = _dot(act.astype(_bf16), w_ref[...].astype(_bf16)) + b_ref[...]


def _adaln(cond, w_ada, b_ada):
    depth = w_ada.shape[0]
    cols = N_MOD * D_MODEL
    col_block = 1536
    return pl.pallas_call(
        _adaln_kernel,
        grid=(depth, cols // col_block),
        in_specs=[
            pl.BlockSpec((SUBLANES, D_MODEL), lambda d, j: (0, 0)),
            pl.BlockSpec((None, D_MODEL, col_block), lambda d, j: (d, 0, j)),
            pl.BlockSpec((None, 1, col_block), lambda d, j: (d, 0, j)),
        ],
        out_specs=pl.BlockSpec((None, SUBLANES, col_block), lambda d, j: (d, 0, j)),
        out_shape=jax.ShapeDtypeStruct((depth, SUBLANES, cols), _f32),
        compiler_params=_params("arbitrary", "arbitrary"),
        name="adaln",
    )(cond, w_ada, b_ada.reshape(depth, 1, cols))


def _proj_kernel(x_ref, sh_ref, sc_ref, w_ref, avg_ref, qg_ref, kg_ref, ln_ref,
                 q_ref, k_ref, v_ref, gu_ref, vn_ref):
    x = x_ref[...]
    h = x * lax.rsqrt(jnp.mean(x * x, -1, keepdims=True) + EPS) * (1.0 + sc_ref[...]) + sh_ref[...]
    p = _dot(h.astype(_bf16), w_ref[...])
    q = p[:, :NA_W]
    k = p[:, NA_W:2 * NA_W]
    v = p[:, 2 * NA_W:3 * NA_W]
    u = p[:, 3 * NA_W:3 * NA_W + SGU_W]
    vs = p[:, 3 * NA_W + SGU_W:]
    q_ms = _dot((q * q).astype(_bf16), avg_ref[...])
    k_ms = _dot((k * k).astype(_bf16), avg_ref[...])
    qn = (q * lax.rsqrt(q_ms + EPS) * qg_ref[...]).astype(_bf16)
    kn = (k * lax.rsqrt(k_ms + EPS) * kg_ref[...]).astype(_bf16)
    vb = v.astype(_bf16)
    for hp in range(HEAD_PAIRS):
        cols = slice(hp * LANES, (hp + 1) * LANES)
        q_ref[hp] = qn[:, cols]
        k_ref[hp] = kn[:, cols]
        v_ref[hp] = vb[:, cols]
    gu_ref[...] = jax.nn.gelu(u).astype(_bf16)
    gv = jax.nn.gelu(vs)
    mu = jnp.mean(gv, -1, keepdims=True)
    cen = gv - mu
    var = jnp.mean(cen * cen, -1, keepdims=True)
    vn_ref[...] = (cen * lax.rsqrt(var + EPS) * ln_ref[...]).astype(_bf16)


def _proj(x, sh, sc, w_in, avg, qg, kg, ln_g, tm):
    bsz, length, _ = x.shape
    vec = lambda n: pl.BlockSpec((1, n), lambda b, i: (0, 0))
    mod = pl.BlockSpec((None, 1, D_MODEL), lambda b, i: (b, 0, 0))
    tok = lambda n: pl.BlockSpec((None, tm, n), lambda b, i: (b, i, 0))
    hpm = pl.BlockSpec((None, HEAD_PAIRS, tm, LANES), lambda b, i: (b, 0, i, 0))
    hpm_shape = jax.ShapeDtypeStruct((bsz, HEAD_PAIRS, length, LANES), _bf16)
    sgu_shape = jax.ShapeDtypeStruct((bsz, length, SGU_W), _bf16)
    return pl.pallas_call(
        _proj_kernel,
        grid=(bsz, length // tm),
        in_specs=[tok(D_MODEL), mod, mod,
                  pl.BlockSpec((D_MODEL, IN_COLS), lambda b, i: (0, 0)),
                  pl.BlockSpec((NA_W, NA_W), lambda b, i: (0, 0)),
                  vec(NA_W), vec(NA_W), vec(SGU_W)],
        out_specs=[hpm, hpm, hpm, tok(SGU_W), tok(SGU_W)],
        out_shape=[hpm_shape, hpm_shape, hpm_shape, sgu_shape, sgu_shape],
        compiler_params=_params("arbitrary", "arbitrary"),
        name="proj",
    )(x, sh, sc, w_in, avg, qg, kg, ln_g)


def _attend(blocks):
    add = lambda a, b: a + b
    m_rows = blocks[0][0].shape[0]
    lane = lax.broadcasted_iota(jnp.int32, blocks[0][0].shape, 1)
    low = lane < NA_HEAD_DIM
    stacked = [jnp.concatenate([jnp.where(low, q, jnp.zeros_like(q)), jnp.where(low, jnp.zeros_like(q), q)], axis=0)
               for q, _, _, _ in blocks]
    scores = [[_dot_nt(qs, key) if bias is None else _dot_nt(qs, key) + bias[...]
               for key, bias in zip(keys, biases)] for qs, (_, keys, _, biases) in zip(stacked, blocks)]
    maxes = [functools.reduce(jnp.maximum, [jnp.max(s, -1, keepdims=True) for s in ss]) for ss in scores]
    probs = [[jnp.exp(s - m) for s in ss] for ss, m in zip(scores, maxes)]
    dens = [functools.reduce(add, [jnp.sum(p, -1, keepdims=True) for p in ps]) for ps in probs]
    accs = [functools.reduce(add, [_dot(p.astype(_bf16), val) for p, val in zip(ps, blk[2])])
            for ps, blk in zip(probs, blocks)]
    outs = [acc / den for acc, den in zip(accs, dens)]
    return [jnp.where(low, o2[:m_rows], o2[m_rows:]) for o2 in outs]


def _attn_kernel(off_ref, var_ref, q_ref, k_ref, v_ref, kc_ref, vc_ref, bycol_ref, o_ref, bias_ref, *, pieces):
    i = pl.program_id(2)

    @pl.when(i == 0)
    def _():
        for var, piece in enumerate(pieces):
            for hh in range(2):
                for j in range(PAIR_ROWS):
                    for kr in range(PAIR_WIN_ROWS):
                        lo = (kr % 2) * GRID_W
                        row = hh * PAIR_BLOCK + j * GRID_W
                        bias_ref[var, row:row + GRID_W, kr * GRID_W:(kr + 1) * GRID_W] = (
                            bycol_ref[hh, piece[j][kr], :, lo:lo + GRID_W])

    kc = kc_ref[...]
    vc = vc_ref[...]
    npair = Q_ROWS // PAIR_ROWS
    for g0 in range(0, npair, PAIRS_IN_FLIGHT):
        blocks = []
        for g in range(g0, g0 + PAIRS_IN_FLIGHT):
            start = pl.multiple_of(off_ref[i * npair + g] * GRID_W, GRID_W)
            var = var_ref[i * npair + g]
            blocks.append((q_ref[g * PAIR_BLOCK:(g + 1) * PAIR_BLOCK, :],
                           [k_ref[pl.ds(start, PAIR_WIN), :], kc],
                           [v_ref[pl.ds(start, PAIR_WIN), :], vc],
                           [bias_ref.at[var], None]))
        for g, out in zip(range(g0, g0 + PAIRS_IN_FLIGHT), _attend(blocks)):
            o_ref[g * PAIR_BLOCK:(g + 1) * PAIR_BLOCK, :] = out.astype(o_ref.dtype)


def _attention(q, k, v, kc, vc, bycol, plan):
    off_tab, var_tab, pieces = plan
    bsz, _, seq, _ = q.shape
    clen = kc.shape[2]
    nblk = seq // Q_BLOCK

    def win_start(i):
        return jnp.clip(i * Q_ROWS - NA_KH // 2, 0, seq // GRID_W - WIN_ROWS) * GRID_W

    qspec = pl.BlockSpec((None, None, Q_BLOCK, LANES), lambda hp, b, i, *_: (b, hp, i, 0))
    wspec = pl.BlockSpec((None, None, pl.Element(WIN_BLOCK), pl.Element(LANES)),
                         lambda hp, b, i, *_: (b, hp, win_start(i), 0))
    cspec = pl.BlockSpec((None, None, clen, LANES), lambda hp, b, i, *_: (b, hp, 0, 0))
    tspec = pl.BlockSpec((None, 2, N_DR, GRID_W, LANES), lambda hp, b, i, *_: (hp, 0, 0, 0, 0))
    grid_spec = pltpu.PrefetchScalarGridSpec(
        num_scalar_prefetch=2,
        grid=(HEAD_PAIRS, bsz, nblk),
        in_specs=[qspec, wspec, wspec, cspec, cspec, tspec],
        out_specs=qspec,
        scratch_shapes=[pltpu.VMEM((len(pieces), 2 * PAIR_BLOCK, PAIR_WIN), _f32)],
    )
    return pl.pallas_call(
        functools.partial(_attn_kernel, pieces=pieces),
        grid_spec=grid_spec,
        out_shape=jax.ShapeDtypeStruct(q.shape, _bf16),
        compiler_params=_params("arbitrary", "arbitrary", "arbitrary"),
        name="attention",
    )(off_tab, var_tab, q, k, v, kc, vc, bycol)


def _ctx_attn_kernel(q_ref, k_ref, v_ref, o_ref):
    o_ref[...] = _attend([(q_ref[...], [k_ref[...]], [v_ref[...]], [None])])[0].astype(o_ref.dtype)


def _ctx_attention(q, k, v):
    bsz, _, clen, _ = q.shape
    spec = pl.BlockSpec((None, None, clen, LANES), lambda b, hp: (b, hp, 0, 0))
    return pl.pallas_call(
        _ctx_attn_kernel,
        grid=(bsz, HEAD_PAIRS),
        in_specs=[spec, spec, spec],
        out_specs=spec,
        out_shape=jax.ShapeDtypeStruct(q.shape, _bf16),
        compiler_params=_params("arbitrary", "arbitrary"),
        name="ctx_attention",
    )(q, k, v)


def _window_plan(rows):
    patterns, off_tab, var_tab = {}, [], []
    for r in range(0, rows, PAIR_ROWS):
        ws = int(np.clip(r - NA_KH // 2, 0, rows - PAIR_WIN_ROWS))
        r0 = r // Q_ROWS * Q_ROWS
        block_ws = int(np.clip(r0 - NA_KH // 2, 0, rows - WIN_ROWS))
        assert 0 <= ws - block_ws <= WIN_ROWS - PAIR_WIN_ROWS
        piece = []
        for j in range(PAIR_ROWS):
            r_start = int(np.clip(r + j - NA_KH // 2, 0, rows - NA_KH))
            assert ws <= r_start and r_start + NA_KH <= ws + PAIR_WIN_ROWS
            piece.append(tuple(ws + kr - (r + j) + NA_KH - 1 if r_start <= ws + kr < r_start + NA_KH else N_DR - 1
                               for kr in range(PAIR_WIN_ROWS)))
        var_tab.append(patterns.setdefault(tuple(piece), len(patterns)))
        off_tab.append(ws - block_ws)
    return jnp.asarray(off_tab, jnp.int32), jnp.asarray(var_tab, jnp.int32), tuple(patterns)


def _window_col_table(rpb):
    col = np.arange(GRID_W)
    col_start = np.clip(col - NA_KW // 2, 0, GRID_W - NA_KW)
    col_ok = (col[None, :] >= col_start[:, None]) & (col[None, :] < col_start[:, None] + NA_KW)
    dc = np.where(col_ok, col[None, :] - col[:, None] + NA_KW - 1, 0)
    onehot = np.zeros((2 * NA_KW - 1, GRID_W * GRID_W), np.float32)
    onehot[dc.reshape(-1), np.arange(GRID_W * GRID_W)] = col_ok.reshape(-1)
    by_col = jnp.dot(rpb.reshape(-1, 2 * NA_KW - 1), jnp.asarray(onehot), precision=lax.Precision.HIGHEST)
    by_col = by_col.reshape(NA_HEADS, 2 * NA_KH - 1, GRID_W, GRID_W)
    by_col = jnp.where(jnp.asarray(col_ok)[None, None], by_col, MASKED)
    by_col = jnp.concatenate([by_col, jnp.full((NA_HEADS, 1, GRID_W, GRID_W), MASKED, _f32)], axis=1)
    by_col = jnp.concatenate([by_col, by_col], axis=-1)
    return by_col.reshape(HEAD_PAIRS, 2, N_DR, GRID_W, LANES)


def _merge_kernel(x_ref, o_ref, gu_ref, vn_ref, ga_ref, shf_ref, scf_ref, ws_ref, bs_ref, og_ref,
                  wout_ref, wr_ref, br_ref, tri_ref, xn_ref, row_ref, meta_ref, cnt_ref, *, tm):
    first = (pl.program_id(0) == 0) & (pl.program_id(1) == 0)

    @pl.when(first)
    def _():
        cnt_ref[...] = jnp.zeros_like(cnt_ref)

    nchunk = tm // SGU_CHUNK
    vn = vn_ref[...]
    lane = lax.broadcasted_iota(jnp.int32, (SGU_CHUNK, LANES), 1)
    low = lane < SGU_GROUP_DIM
    mixed_cols = []
    for gp in range(SGU_W // LANES):
        rhs_cols = []
        for n in range(nchunk):
            blk = vn[n * SGU_CHUNK:(n + 1) * SGU_CHUNK, gp * LANES:(gp + 1) * LANES]
            zero = jnp.zeros_like(blk)
            rhs_cols.append(jnp.concatenate([jnp.where(low, blk, zero), jnp.where(low, zero, blk)], axis=0))
        rhs = jnp.concatenate(rhs_cols, axis=1)
        mix = _dot(ws_ref[gp], rhs)
        mixed_cols.append(jnp.concatenate([mix[:, n * LANES:(n + 1) * LANES] for n in range(nchunk)], axis=0))
    mixed = jnp.concatenate(mixed_cols, axis=1)
    bias = jnp.concatenate([bs_ref[...]] * nchunk, axis=0)
    o_sg = gu_ref[...].astype(_f32) * (mixed + bias)
    o_na = jnp.concatenate([o_ref[hp].astype(_f32) for hp in range(HEAD_PAIRS)], axis=1)

    def rms(t):
        return t * lax.rsqrt(jnp.mean(t * t, -1, keepdims=True) + EPS)

    y = jnp.concatenate([rms(o_na), rms(o_sg)], axis=1) * og_ref[...]
    xn = x_ref[...] + ga_ref[...] * _dot(y.astype(_bf16), wout_ref[...])
    xn_ref[...] = xn
    hf = rms(xn) * (1.0 + scf_ref[...]) + shf_ref[...]
    row_ref[:, :D_MODEL] = hf

    lg = _dot(hf.astype(_bf16), wr_ref[...]) + br_ref[...]
    lane_i = lax.broadcasted_iota(jnp.int32, lg.shape, 1)
    lane_f = lane_i.astype(_f32)
    far = float(LANES)
    is_g = lane_i < N_GROUPS
    gl = jnp.where(is_g, lg, MASKED)
    gm = jnp.max(gl, -1, keepdims=True)
    p_top = 1.0 / jnp.sum(jnp.where(is_g, jnp.exp(gl - gm), 0.0), -1, keepdims=True)
    g_idx = jnp.min(jnp.where(is_g & (gl == gm), lane_f, far), -1, keepdims=True)
    e_lo = N_GROUPS + EXPERTS_PER_GROUP * g_idx
    is_e = (lane_f >= e_lo) & (lane_f < e_lo + EXPERTS_PER_GROUP)
    el = jnp.where(is_e, lg, MASKED)
    em = jnp.max(el, -1, keepdims=True)
    ee = jnp.where(is_e, jnp.exp(el - em), 0.0)
    pe = jnp.where(is_e, ee / jnp.sum(ee, -1, keepdims=True), -1.0)
    v1 = jnp.max(pe, -1, keepdims=True)
    i1 = jnp.min(jnp.where(pe == v1, lane_f, far), -1, keepdims=True)
    pe2 = jnp.where(lane_f == i1, -1.0, pe)
    v2 = jnp.max(pe2, -1, keepdims=True)
    i2 = jnp.min(jnp.where(pe2 == v2, lane_f, far), -1, keepdims=True)
    den = v1 + v2
    w1 = v1 / den * p_top
    w2 = v2 / den * p_top
    first_low = i1 < i2
    a = jnp.minimum(i1, i2) - e_lo
    b = jnp.maximum(i1, i2) - e_lo
    lex = a * (7.0 - a) * 0.5 + (b - a - 1.0)
    pair = lex
    for src, dst in enumerate(PAIR_OF_LEX):
        if src != dst:
            pair = jnp.where(lex == float(src), float(dst), pair)
    cls = g_idx * float(N_PAIRS) + pair
    gate_a = jnp.where(first_low, w1, w2)
    gate_b = jnp.where(first_low, w2, w1)
    row_ref[:, D_MODEL:] = jnp.where(lane_i == 0, gate_a, jnp.where(lane_i == 1, gate_b, 0.0))

    onehot = lane_f == cls
    before = _dot(tri_ref[...], jnp.where(onehot, 1.0, 0.0).astype(_bf16))
    rank = jnp.sum(jnp.where(onehot, before + cnt_ref[...], 0.0), -1, keepdims=True)
    cnt_ref[...] += jnp.sum(jnp.where(onehot, 1.0, 0.0), 0, keepdims=True)
    meta = jnp.where(lane_i == 0, cls, jnp.where(lane_i == 1, rank, 0.0))
    meta_ref[...] = jnp.transpose(meta)[:SUBLANES, :]


def _merge(x, o_na, gu, vn, g_a, sh_f, sc_f, ws_cat, bs_tab, out_gain, w_out, w_r, b_r, tri, tm):
    bsz, length, _ = x.shape
    mod = pl.BlockSpec((None, 1, D_MODEL), lambda b, i: (b, 0, 0))
    tok = lambda n: pl.BlockSpec((None, tm, n), lambda b, i: (b, i, 0))
    const = lambda *shape: pl.BlockSpec(shape, lambda b, i: (0,) * len(shape))
    return pl.pallas_call(
        functools.partial(_merge_kernel, tm=tm),
        grid=(bsz, length // tm),
        in_specs=[tok(D_MODEL),
                  pl.BlockSpec((None, HEAD_PAIRS, tm, LANES), lambda b, i: (b, 0, i, 0)),
                  tok(SGU_W), tok(SGU_W), mod, mod, mod,
                  const(SGU_W // LANES, SGU_CHUNK, 2 * SGU_CHUNK),
                  const(SGU_CHUNK, SGU_W),
                  const(1, D_MODEL),
                  const(D_MODEL, D_MODEL),
                  const(D_MODEL, LANES),
                  const(1, LANES),
                  const(tm, tm)],
        out_specs=[tok(D_MODEL), tok(ROW_W),
                   pl.BlockSpec((None, SUBLANES, tm), lambda b, i: (b, 0, i)),
                   const(1, LANES)],
        out_shape=[jax.ShapeDtypeStruct((bsz, length, D_MODEL), _f32),
                   jax.ShapeDtypeStruct((bsz, length, ROW_W), _f32),
                   jax.ShapeDtypeStruct((bsz, SUBLANES, length), _f32),
                   jax.ShapeDtypeStruct((1, LANES), _f32)],
        compiler_params=_params("arbitrary", "arbitrary"),
        name="merge",
    )(x, o_na, gu, vn, g_a, sh_f, sc_f, ws_cat, bs_tab, out_gain, w_out, w_r, b_r, tri)


def _dispatch_kernel(pos_ref, rows_ref, init_ref, xs_ref, sem, *, td):
    del init_ref

    def issue(j, carry):
        pltpu.make_async_copy(rows_ref.at[pl.ds(j, 1)], xs_ref.at[pl.ds(pos_ref[0, j], 1)], sem).start()
        return carry

    lax.fori_loop(0, td, issue, 0, unroll=8)
    pltpu.make_async_copy(rows_ref, xs_ref.at[pl.ds(0, td)], sem).wait()


def _dispatch(rows, pos, padded, td):
    n = rows.shape[0]
    return pl.pallas_call(
        functools.partial(_dispatch_kernel, td=td),
        grid=(n // td,),
        in_specs=[pl.BlockSpec((None, 1, td), lambda i: (i, 0, 0), memory_space=pltpu.SMEM),
                  pl.BlockSpec((td, ROW_W), lambda i: (i, 0)),
                  pl.BlockSpec(memory_space=pl.ANY)],
        out_specs=pl.BlockSpec(memory_space=pl.ANY),
        out_shape=jax.ShapeDtypeStruct((padded, ROW_W), _f32),
        scratch_shapes=[pltpu.SemaphoreType.DMA(())],
        input_output_aliases={2: 0},
        compiler_params=_params("arbitrary"),
        name="dispatch",
    )(pos.reshape(n // td, 1, td), rows, jnp.zeros((padded, ROW_W), _f32))


def _combine_kernel(pos_ref, x_ref, g_ref, ys_ref, o_ref, buf_ref, sem, *, tc):
    def issue(j, carry):
        pltpu.make_async_copy(ys_ref.at[pl.ds(pos_ref[0, j], 1)], buf_ref.at[pl.ds(j, 1)], sem).start()
        return carry

    lax.fori_loop(0, tc, issue, 0, unroll=8)
    pltpu.make_async_copy(ys_ref.at[pl.ds(0, tc)], buf_ref, sem).wait()
    o_ref[...] = x_ref[...] + g_ref[...] * buf_ref[...]


def _combine(x, ys, pos, g, tc):
    bsz, length, _ = x.shape
    per_batch = length // tc
    tok = pl.BlockSpec((None, tc, D_MODEL), lambda b, i: (b, i, 0))
    return pl.pallas_call(
        functools.partial(_combine_kernel, tc=tc),
        grid=(bsz, per_batch),
        in_specs=[pl.BlockSpec((None, 1, tc), lambda b, i: (b * per_batch + i, 0, 0), memory_space=pltpu.SMEM),
                  tok,
                  pl.BlockSpec((None, 1, D_MODEL), lambda b, i: (b, 0, 0)),
                  pl.BlockSpec(memory_space=pl.ANY)],
        out_specs=tok,
        out_shape=jax.ShapeDtypeStruct(x.shape, _f32),
        scratch_shapes=[pltpu.VMEM((tc, D_MODEL), _f32), pltpu.SemaphoreType.DMA(())],
        compiler_params=_params("arbitrary", "arbitrary"),
        name="combine",
    )(pos.reshape(bsz * per_batch, 1, tc), x, g, ys)


def _experts_kernel(ea_ref, eb_ref, chg_ref, nlive_ref, xs_ref, w1a_ref, w3a_ref, w2a_ref,
                    w1b_ref, w3b_ref, w2b_ref, y_ref, c1a, c3a, c2a, c1b, c3b, c2b):
    t = pl.program_id(0)
    live = t < nlive_ref[0]

    @pl.when(live & ((chg_ref[t] & 1) != 0))
    def _():
        for src, dst in ((w1a_ref, c1a), (w3a_ref, c3a), (w2a_ref, c2a)):
            dst[...] = src[...].astype(_bf16)

    @pl.when(live & ((chg_ref[t] & 2) != 0))
    def _():
        for src, dst in ((w1b_ref, c1b), (w3b_ref, c3b), (w2b_ref, c2b)):
            dst[...] = src[...].astype(_bf16)

    @pl.when(live)
    def _():
        xs = xs_ref[:, :D_MODEL].astype(_bf16)
        gates = xs_ref[:, D_MODEL:]

        def expert(w1_ref, w3_ref, w2_ref):
            hid = jax.nn.silu(_dot(xs, w1_ref[...])) * _dot(xs, w3_ref[...])
            return _dot(hid.astype(_bf16), w2_ref[...])

        y_ref[...] = gates[:, 0:1] * expert(c1a, c3a, c2a) + gates[:, 1:2] * expert(c1b, c3b, c2b)

    @pl.when(jnp.logical_not(live))
    def _():
        y_ref[...] = jnp.zeros_like(y_ref)


def _experts(xs, tile_ea, tile_eb, chg, nlive, w1, w3, w2, layer, tile):
    ntiles = xs.shape[0] // tile
    up_a = pl.BlockSpec((None, None, D_MODEL, D_EXPERT), lambda t, ea, eb, cg, nl: (layer, ea[t], 0, 0))
    dn_a = pl.BlockSpec((None, None, D_EXPERT, D_MODEL), lambda t, ea, eb, cg, nl: (layer, ea[t], 0, 0))
    up_b = pl.BlockSpec((None, None, D_MODEL, D_EXPERT), lambda t, ea, eb, cg, nl: (layer, eb[t], 0, 0))
    dn_b = pl.BlockSpec((None, None, D_EXPERT, D_MODEL), lambda t, ea, eb, cg, nl: (layer, eb[t], 0, 0))
    grid_spec = pltpu.PrefetchScalarGridSpec(
        num_scalar_prefetch=4,
        grid=(ntiles,),
        in_specs=[pl.BlockSpec((tile, ROW_W), lambda t, ea, eb, cg, nl: (t, 0)),
                  up_a, up_a, dn_a, up_b, up_b, dn_b],
        out_specs=pl.BlockSpec((tile, D_MODEL), lambda t, ea, eb, cg, nl: (t, 0)),
        scratch_shapes=[pltpu.VMEM((D_MODEL, D_EXPERT), _bf16), pltpu.VMEM((D_MODEL, D_EXPERT), _bf16),
                        pltpu.VMEM((D_EXPERT, D_MODEL), _bf16)] * 2,
    )
    return pl.pallas_call(
        _experts_kernel,
        grid_spec=grid_spec,
        out_shape=jax.ShapeDtypeStruct((xs.shape[0], D_MODEL), _f32),
        compiler_params=_params("arbitrary"),
        name="experts",
    )(tile_ea, tile_eb, chg, nlive, xs, w1, w3, w2, w1, w3, w2)


def _dispatch_plan(cls, rank, counts, tile):
    n = cls.shape[0]
    padded = -(-(n + N_CLASSES * (tile - 1)) // tile) * tile
    ntiles = padded // tile
    pcounts = (counts + tile - 1) // tile * tile
    pad_end = jnp.cumsum(pcounts)
    pad_off = pad_end - pcounts
    pos = pad_off[cls] + rank
    nlive = pad_end[-1] // tile
    tile_start = jnp.arange(ntiles, dtype=jnp.int32) * tile
    tile_cls = jnp.sum((pad_end[None, :] <= jnp.minimum(tile_start, pad_end[-1] - 1)[:, None]).astype(jnp.int32), axis=1)
    tile_cls = jnp.minimum(tile_cls, N_CLASSES - 1)
    grp = tile_cls // N_PAIRS
    pair = tile_cls % N_PAIRS
    tile_ea = grp * EXPERTS_PER_GROUP + jnp.asarray(PAIR_A, jnp.int32)[pair]
    tile_eb = grp * EXPERTS_PER_GROUP + jnp.asarray(PAIR_B, jnp.int32)[pair]
    differs = lambda e: jnp.concatenate([jnp.ones((1,), jnp.int32), (e[1:] != e[:-1]).astype(jnp.int32)])
    chg = differs(tile_ea) + 2 * differs(tile_eb)
    return pos.astype(jnp.int32), padded, tile_ea, tile_eb, chg, nlive.astype(jnp.int32).reshape(1)


def _ffn(x_new, rows, meta, counts, g_f, w1, w3, w2, layer, tile, tdma):
    bsz, length, _ = x_new.shape
    n = bsz * length
    cls = meta[:, 0, :].reshape(n).astype(jnp.int32)
    rank = meta[:, 1, :].reshape(n).astype(jnp.int32)
    cnt = counts[0, :N_CLASSES].astype(jnp.int32)
    pos, padded, tile_ea, tile_eb, chg, nlive = _dispatch_plan(cls, rank, cnt, tile)
    xs = _dispatch(rows.reshape(n, ROW_W), pos, padded, tdma)
    ys = _experts(xs, tile_ea, tile_eb, chg, nlive, w1, w3, w2, layer, tile)
    return _combine(x_new, ys, pos, g_f, tdma)


def kernel(x, c, ctx, c_ctx, w_ada, b_ada, w_in, q_gain, k_gain, rpb, sgu_ln, sgu_w, sgu_b, out_gain,
           w_out, rg_w, rg_b, re_w, re_b, w1, w3, w2):
    bsz, seq, _ = x.shape
    depth = w_ada.shape[0]
    rows = seq // GRID_W
    assert seq % Q_BLOCK == 0 and rows // Q_ROWS >= 3 and ctx.shape[1] % SGU_CHUNK == 0
    tm_x, tm_c = 512, ctx.shape[1]

    cond = jnp.zeros((SUBLANES, D_MODEL), _f32).at[:bsz].set(c).at[bsz].set(c_ctx)
    mods = _adaln(cond, w_ada, b_ada).reshape(depth, SUBLANES, N_MOD, D_MODEL)

    head_id = np.arange(NA_W) // NA_HEAD_DIM
    avg = jnp.asarray((head_id[:, None] == head_id[None, :]).astype(np.float32) / NA_HEAD_DIM, _bf16)
    win_plan = _window_plan(rows)
    tri = {tm: jnp.asarray(np.tril(np.ones((tm, tm), np.float32), -1), _bf16) for tm in {tm_x, tm_c}}

    for i in range(depth):
        last = i == depth - 1
        mx = [mods[i, :bsz, j][:, None, :] for j in range(N_MOD)]
        mc = [jnp.broadcast_to(mods[i, bsz, j][None, None, :], (bsz, 1, D_MODEL)) for j in range(N_MOD)]
        w_in_b = w_in[i].astype(_bf16)
        qg = (jnp.tile(q_gain[i], NA_HEADS) * NA_HEAD_DIM ** -0.5)[None, :]
        kg = jnp.tile(k_gain[i], NA_HEADS)[None, :]
        ln_g = sgu_ln[i][None, :]
        ws_cat = jnp.concatenate([sgu_w[i, 0::2], sgu_w[i, 1::2]], axis=2).astype(_bf16)
        bs_tab = jnp.repeat(sgu_b[i].T, SGU_GROUP_DIM, axis=1)
        og = out_gain[i][None, :]
        w_out_b = w_out[i].astype(_bf16)
        w_r = jnp.concatenate([rg_w[i], jnp.transpose(re_w[i], (1, 0, 2)).reshape(D_MODEL, N_EXPERTS),
                               jnp.zeros((D_MODEL, LANES - N_GROUPS - N_EXPERTS), _f32)], axis=1).astype(_bf16)
        b_r = jnp.concatenate([rg_b[i], re_b[i].reshape(-1),
                               jnp.zeros((LANES - N_GROUPS - N_EXPERTS,), _f32)])[None, :]
        bycol = _window_col_table(rpb[i])

        q, k, v, gu, vn = _proj(x, mx[0], mx[1], w_in_b, avg, qg, kg, ln_g, tm_x)
        qc, kc, vc, guc, vnc = _proj(ctx, mc[0], mc[1], w_in_b, avg, qg, kg, ln_g, tm_c)
        o_na = _attention(q, k, v, kc, vc, bycol, win_plan)
        x_new, xrows, meta, counts = _merge(x, o_na, gu, vn, mx[2], mx[3], mx[4], ws_cat, bs_tab, og,
                                            w_out_b, w_r, b_r, tri[tm_x], tm_x)
        x = _ffn(x_new, xrows, meta, counts, mx[5], w1, w3, w2, i, 256, 1024)
        if not last:
            oc_na = _ctx_attention(qc, kc, vc)
            c_new, crows, cmeta, ccounts = _merge(ctx, oc_na, guc, vnc, mc[2], mc[3], mc[4], ws_cat, bs_tab,
                                                  og, w_out_b, w_r, b_r, tri[tm_c], tm_c)
            ctx = _ffn(c_new, crows, cmeta, ccounts, mc[5], w1, w3, w2, i, 64, tm_c)
    return x
```

```python
import functools

import jax
import jax.numpy as jnp
import numpy as np
from jax import lax
from jax.experimental import pallas as pl
from jax.experimental.pallas import tpu as pltpu

D_MODEL = 1024
GRID_W = 64
NA_HEADS = 8
NA_HEAD_DIM = 64
NA_W = NA_HEADS * NA_HEAD_DIM
NA_KH = 8
NA_KW = 16
SGU_GROUPS = 8
SGU_W = D_MODEL - NA_W
SGU_GROUP_DIM = SGU_W // SGU_GROUPS
SGU_CHUNK = 128
IN_COLS = 3 * NA_W + 2 * SGU_W
N_GROUPS = 4
EXPERTS_PER_GROUP = 4
N_EXPERTS = N_GROUPS * EXPERTS_PER_GROUP
D_EXPERT = D_MODEL // 2
N_MOD = 6
EPS = 1e-6

LANES = 128
SUBLANES = 8
HEAD_PAIRS = NA_W // LANES
Q_ROWS = 8
Q_BLOCK = Q_ROWS * GRID_W
WIN_ROWS = Q_ROWS + NA_KH - 1
WIN_BLOCK = WIN_ROWS * GRID_W
PAIR_ROWS = 2
PAIR_BLOCK = PAIR_ROWS * GRID_W
PAIR_WIN_ROWS = PAIR_ROWS + NA_KH - 1
PAIR_WIN = PAIR_WIN_ROWS * GRID_W
PAIRS_IN_FLIGHT = 2
N_DR = 2 * NA_KH
N_PAIRS = 6
N_CLASSES = N_GROUPS * N_PAIRS
PAIR_A = (0, 0, 1, 1, 0, 2)
PAIR_B = (1, 2, 2, 3, 3, 3)
PAIR_OF_LEX = (0, 1, 4, 2, 3, 5)
ROW_W = D_MODEL + LANES
MASKED = -1e30
VMEM_LIMIT_BYTES = 56 * 1024 * 1024

_f32 = jnp.float32
_bf16 = jnp.bfloat16


def _params(*semantics):
    return pltpu.CompilerParams(dimension_semantics=semantics, vmem_limit_bytes=VMEM_LIMIT_BYTES)


def _dot(a, b):
    return jnp.dot(a, b, preferred_element_type=_f32)


def _dot_nt(a, b):
    return lax.dot_general(a, b, (((1,), (1,)), ((), ())), preferred_element_type=_f32)


def _adaln_kernel(cond_ref, w_ref, b_ref, out_ref):
    cond = cond_ref[...]
    act = cond * jax.nn.sigmoid(cond)
    out_ref[...] = _dot(act.astype(_bf16), w_ref[...].astype(_bf16)) + b_ref[...]


def _adaln(cond, w_ada, b_ada):
    depth = w_ada.shape[0]
    cols = N_MOD * D_MODEL
    col_block = 1536
    return pl.pallas_call(
        _adaln_kernel,
        grid=(depth, cols // col_block),
        in_specs=[
            pl.BlockSpec((SUBLANES, D_MODEL), lambda d, j: (0, 0)),
            pl.BlockSpec((None, D_MODEL, col_block), lambda d, j: (d, 0, j)),
            pl.BlockSpec((None, 1, col_block), lambda d, j: (d, 0, j)),
        ],
        out_specs=pl.BlockSpec((None, SUBLANES, col_block), lambda d, j: (d, 0, j)),
        out_shape=jax.ShapeDtypeStruct((depth, SUBLANES, cols), _f32),
        compiler_params=_params("arbitrary", "arbitrary"),
        name="adaln",
    )(cond, w_ada, b_ada.reshape(depth, 1, cols))


def _proj_kernel(x_ref, sh_ref, sc_ref, w_ref, avg_ref, qg_ref, kg_ref, ln_ref,
                 q_ref, k_ref, v_ref, gu_ref, vn_ref):
    x = x_ref[...]
    h = x * lax.rsqrt(jnp.mean(x * x, -1, keepdims=True) + EPS) * (1.0 + sc_ref[...]) + sh_ref[...]
    p = _dot(h.astype(_bf16), w_ref[...])
    q = p[:, :NA_W]
    k = p[:, NA_W:2 * NA_W]
    v = p[:, 2 * NA_W:3 * NA_W]
    u = p[:, 3 * NA_W:3 * NA_W + SGU_W]
    vs = p[:, 3 * NA_W + SGU_W:]
    q_ms = _dot((q * q).astype(_bf16), avg_ref[...])
    k_ms = _dot((k * k).astype(_bf16), avg_ref[...])
    qn = (q * lax.rsqrt(q_ms + EPS) * qg_ref[...]).astype(_bf16)
    kn = (k * lax.rsqrt(k_ms + EPS) * kg_ref[...]).astype(_bf16)
    vb = v.astype(_bf16)
    for hp in range(HEAD_PAIRS):
        cols = slice(hp * LANES, (hp + 1) * LANES)
        q_ref[hp] = qn[:, cols]
        k_ref[hp] = kn[:, cols]
        v_ref[hp] = vb[:, cols]
    gu_ref[...] = jax.nn.gelu(u).astype(_bf16)
    gv = jax.nn.gelu(vs)
    mu = jnp.mean(gv, -1, keepdims=True)
    cen = gv - mu
    var = jnp.mean(cen * cen, -1, keepdims=True)
    vn_ref[...] = (cen * lax.rsqrt(var + EPS) * ln_ref[...]).astype(_bf16)


def _proj(x, sh, sc, w_in, avg, qg, kg, ln_g, tm):
    bsz, length, _ = x.shape
    vec = lambda n: pl.BlockSpec((1, n), lambda b, i: (0, 0))
    mod = pl.BlockSpec((None, 1, D_MODEL), lambda b, i: (b, 0, 0))
    tok = lambda n: pl.BlockSpec((None, tm, n), lambda b, i: (b, i, 0))
    hpm = pl.BlockSpec((None, HEAD_PAIRS, tm, LANES), lambda b, i: (b, 0, i, 0))
    hpm_shape = jax.ShapeDtypeStruct((bsz, HEAD_PAIRS, length, LANES), _bf16)
    sgu_shape = jax.ShapeDtypeStruct((bsz, length, SGU_W), _bf16)
    return pl.pallas_call(
        _proj_kernel,
        grid=(bsz, length // tm),
        in_specs=[tok(D_MODEL), mod, mod,
                  pl.BlockSpec((D_MODEL, IN_COLS), lambda b, i: (0, 0)),
                  pl.BlockSpec((NA_W, NA_W), lambda b, i: (0, 0)),
                  vec(NA_W), vec(NA_W), vec(SGU_W)],
        out_specs=[hpm, hpm, hpm, tok(SGU_W), tok(SGU_W)],
        out_shape=[hpm_shape, hpm_shape, hpm_shape, sgu_shape, sgu_shape],
        compiler_params=_params("arbitrary", "arbitrary"),
        name="proj",
    )(x, sh, sc, w_in, avg, qg, kg, ln_g)


def _attend(blocks):
    add = lambda a, b: a + b
    m_rows = blocks[0][0].shape[0]
    lane = lax.broadcasted_iota(jnp.int32, blocks[0][0].shape, 1)
    low = lane < NA_HEAD_DIM
    stacked = [jnp.concatenate([jnp.where(low, q, jnp.zeros_like(q)), jnp.where(low, jnp.zeros_like(q), q)], axis=0)
               for q, _, _, _ in blocks]
    scores = [[_dot_nt(qs, key) if bias is None else _dot_nt(qs, key) + bias[...]
               for key, bias in zip(keys, biases)] for qs, (_, keys, _, biases) in zip(stacked, blocks)]
    maxes = [functools.reduce(jnp.maximum, [jnp.max(s, -1, keepdims=True) for s in ss]) for ss in scores]
    probs = [[jnp.exp(s - m) for s in ss] for ss, m in zip(scores, maxes)]
    dens = [functools.reduce(add, [jnp.sum(p, -1, keepdims=True) for p in ps]) for ps in probs]
    accs = [functools.reduce(add, [_dot(p.astype(_bf16), val) for p, val in zip(ps, blk[2])])
            for ps, blk in zip(probs, blocks)]
    outs = [acc / den for acc, den in zip(accs, dens)]
    return [jnp.where(low, o2[:m_rows], o2[m_rows:]) for o2 in outs]


def _attn_kernel(off_ref, var_ref, q_ref, k_ref, v_ref, kc_ref, vc_ref, bycol_ref, o_ref, bias_ref, *, pieces):
    i = pl.program_id(2)

    @pl.when(i == 0)
    def _():
        for var, piece in enumerate(pieces):
            for hh in range(2):
                for j in range(PAIR_ROWS):
                    for kr in range(PAIR_WIN_ROWS):
                        lo = (kr % 2) * GRID_W
                        row = hh * PAIR_BLOCK + j * GRID_W
                        bias_ref[var, row:row + GRID_W, kr * GRID_W:(kr + 1) * GRID_W] = (
                            bycol_ref[hh, piece[j][kr], :, lo:lo + GRID_W])

    kc = kc_ref[...]
    vc = vc_ref[...]
    npair = Q_ROWS // PAIR_ROWS
    for g0 in range(0, npair, PAIRS_IN_FLIGHT):
        blocks = []
        for g in range(g0, g0 + PAIRS_IN_FLIGHT):
            start = pl.multiple_of(off_ref[i * npair + g] * GRID_W, GRID_W)
            var = var_ref[i * npair + g]
            blocks.append((q_ref[g * PAIR_BLOCK:(g + 1) * PAIR_BLOCK, :],
                           [k_ref[pl.ds(start, PAIR_WIN), :], kc],
                           [v_ref[pl.ds(start, PAIR_WIN), :], vc],
                           [bias_ref.at[var], None]))
        for g, out in zip(range(g0, g0 + PAIRS_IN_FLIGHT), _attend(blocks)):
            o_ref[g * PAIR_BLOCK:(g + 1) * PAIR_BLOCK, :] = out.astype(o_ref.dtype)


def _attention(q, k, v, kc, vc, bycol, plan):
    off_tab, var_tab, pieces = plan
    bsz, _, seq, _ = q.shape
    clen = kc.shape[2]
    nblk = seq // Q_BLOCK

    def win_start(i):
        return jnp.clip(i * Q_ROWS - NA_KH // 2, 0, seq // GRID_W - WIN_ROWS) * GRID_W

    qspec = pl.BlockSpec((None, None, Q_BLOCK, LANES), lambda hp, b, i, *_: (b, hp, i, 0))
    wspec = pl.BlockSpec((None, None, pl.Element(WIN_BLOCK), pl.Element(LANES)),
                         lambda hp, b, i, *_: (b, hp, win_start(i), 0))
    cspec = pl.BlockSpec((None, None, clen, LANES), lambda hp, b, i, *_: (b, hp, 0, 0))
    tspec = pl.BlockSpec((None, 2, N_DR, GRID_W, LANES), lambda hp, b, i, *_: (hp, 0, 0, 0, 0))
    grid_spec = pltpu.PrefetchScalarGridSpec(
        num_scalar_prefetch=2,
        grid=(HEAD_PAIRS, bsz, nblk),
        in_specs=[qspec, wspec, wspec, cspec, cspec, tspec],
        out_specs=qspec,
        scratch_shapes=[pltpu.VMEM((len(pieces), 2 * PAIR_BLOCK, PAIR_WIN), _f32)],
    )
    return pl.pallas_call(
        functools.partial(_attn_kernel, pieces=pieces),
        grid_spec=grid_spec,
        out_shape=jax.ShapeDtypeStruct(q.shape, _bf16),
        compiler_params=_params("arbitrary", "arbitrary", "arbitrary"),
        name="attention",
    )(off_tab, var_tab, q, k, v, kc, vc, bycol)


def _ctx_attn_kernel(q_ref, k_ref, v_ref, o_ref):
    o_ref[...] = _attend([(q_ref[...], [k_ref[...]], [v_ref[...]], [None])])[0].astype(o_ref.dtype)


def _ctx_attention(q, k, v):
    bsz, _, clen, _ = q.shape
    spec = pl.BlockSpec((None, None, clen, LANES), lambda b, hp: (b, hp, 0, 0))
    return pl.pallas_call(
        _ctx_attn_kernel,
        grid=(bsz, HEAD_PAIRS),
        in_specs=[spec, spec, spec],
        out_specs=spec,
        out_shape=jax.ShapeDtypeStruct(q.shape, _bf16),
        compiler_params=_params("arbitrary", "arbitrary"),
        name="ctx_attention",
    )(q, k, v)


def _window_plan(rows):
    patterns, off_tab, var_tab = {}, [], []
    for r in range(0, rows, PAIR_ROWS):
        ws = int(np.clip(r - NA_KH // 2, 0, rows - PAIR_WIN_ROWS))
        r0 = r // Q_ROWS * Q_ROWS
        block_ws = int(np.clip(r0 - NA_KH // 2, 0, rows - WIN_ROWS))
        assert 0 <= ws - block_ws <= WIN_ROWS - PAIR_WIN_ROWS
        piece = []
        for j in range(PAIR_ROWS):
            r_start = int(np.clip(r + j - NA_KH // 2, 0, rows - NA_KH))
            assert ws <= r_start and r_start + NA_KH <= ws + PAIR_WIN_ROWS
            piece.append(tuple(ws + kr - (r + j) + NA_KH - 1 if r_start <= ws + kr < r_start + NA_KH else N_DR - 1
                               for kr in range(PAIR_WIN_ROWS)))
        var_tab.append(patterns.setdefault(tuple(piece), len(patterns)))
        off_tab.append(ws - block_ws)
    return jnp.asarray(off_tab, jnp.int32), jnp.asarray(var_tab, jnp.int32), tuple(patterns)


def _window_col_table(rpb):
    col = np.arange(GRID_W)
    col_start = np.clip(col - NA_KW // 2, 0, GRID_W - NA_KW)
    col_ok = (col[None, :] >= col_start[:, None]) & (col[None, :] < col_start[:, None] + NA_KW)
    dc = np.where(col_ok, col[None, :] - col[:, None] + NA_KW - 1, 0)
    onehot = np.zeros((2 * NA_KW - 1, GRID_W * GRID_W), np.float32)
    onehot[dc.reshape(-1), np.arange(GRID_W * GRID_W)] = col_ok.reshape(-1)
    by_col = jnp.dot(rpb.reshape(-1, 2 * NA_KW - 1), jnp.asarray(onehot), precision=lax.Precision.HIGHEST)
    by_col = by_col.reshape(NA_HEADS, 2 * NA_KH - 1, GRID_W, GRID_W)
    by_col = jnp.where(jnp.asarray(col_ok)[None, None], by_col, MASKED)
    by_col = jnp.concatenate([by_col, jnp.full((NA_HEADS, 1, GRID_W, GRID_W), MASKED, _f32)], axis=1)
    by_col = jnp.concatenate([by_col, by_col], axis=-1)
    return by_col.reshape(HEAD_PAIRS, 2, N_DR, GRID_W, LANES)


def _merge_kernel(x_ref, o_ref, gu_ref, vn_ref, ga_ref, shf_ref, scf_ref, ws_ref, bs_ref, og_ref,
                  wout_ref, wr_ref, br_ref, tri_ref, xn_ref, row_ref, meta_ref, cnt_ref, *, tm):
    first = (pl.program_id(0) == 0) & (pl.program_id(1) == 0)

    @pl.when(first)
    def _():
        cnt_ref[...] = jnp.zeros_like(cnt_ref)

    nchunk = tm // SGU_CHUNK
    vn = vn_ref[...]
    lane = lax.broadcasted_iota(jnp.int32, (SGU_CHUNK, LANES), 1)
    low = lane < SGU_GROUP_DIM
    mixed_cols = []
    for gp in range(SGU_W // LANES):
        rhs_cols = []
        for n in range(nchunk):
            blk = vn[n * SGU_CHUNK:(n + 1) * SGU_CHUNK, gp * LANES:(gp + 1) * LANES]
            zero = jnp.zeros_like(blk)
            rhs_cols.append(jnp.concatenate([jnp.where(low, blk, zero), jnp.where(low, zero, blk)], axis=0))
        rhs = jnp.concatenate(rhs_cols, axis=1)
        mix = _dot(ws_ref[gp], rhs)
        mixed_cols.append(jnp.concatenate([mix[:, n * LANES:(n + 1) * LANES] for n in range(nchunk)], axis=0))
    mixed = jnp.concatenate(mixed_cols, axis=1)
    bias = jnp.concatenate([bs_ref[...]] * nchunk, axis=0)
    o_sg = gu_ref[...].astype(_f32) * (mixed + bias)
    o_na = jnp.concatenate([o_ref[hp].astype(_f32) for hp in range(HEAD_PAIRS)], axis=1)

    def rms(t):
        return t * lax.rsqrt(jnp.mean(t * t, -1, keepdims=True) + EPS)

    y = jnp.concatenate([rms(o_na), rms(o_sg)], axis=1) * og_ref[...]
    xn = x_ref[...] + ga_ref[...] * _dot(y.astype(_bf16), wout_ref[...])
    xn_ref[...] = xn
    hf = rms(xn) * (1.0 + scf_ref[...]) + shf_ref[...]
    row_ref[:, :D_MODEL] = hf

    lg = _dot(hf.astype(_bf16), wr_ref[...]) + br_ref[...]
    lane_i = lax.broadcasted_iota(jnp.int32, lg.shape, 1)
    lane_f = lane_i.astype(_f32)
    far = float(LANES)
    is_g = lane_i < N_GROUPS
    gl = jnp.where(is_g, lg, MASKED)
    gm = jnp.max(gl, -1, keepdims=True)
    p_top = 1.0 / jnp.sum(jnp.where(is_g, jnp.exp(gl - gm), 0.0), -1, keepdims=True)
    g_idx = jnp.min(jnp.where(is_g & (gl == gm), lane_f, far), -1, keepdims=True)
    e_lo = N_GROUPS + EXPERTS_PER_GROUP * g_idx
    is_e = (lane_f >= e_lo) & (lane_f < e_lo + EXPERTS_PER_GROUP)
    el = jnp.where(is_e, lg, MASKED)
    em = jnp.max(el, -1, keepdims=True)
    ee = jnp.where(is_e, jnp.exp(el - em), 0.0)
    pe = jnp.where(is_e, ee / jnp.sum(ee, -1, keepdims=True), -1.0)
    v1 = jnp.max(pe, -1, keepdims=True)
    i1 = jnp.min(jnp.where(pe == v1, lane_f, far), -1, keepdims=True)
    pe2 = jnp.where(lane_f == i1, -1.0, pe)
    v2 = jnp.max(pe2, -1, keepdims=True)
    i2 = jnp.min(jnp.where(pe2 == v2, lane_f, far), -1, keepdims=True)
    den = v1 + v2
    w1 = v1 / den * p_top
    w2 = v2 / den * p_top
    first_low = i1 < i2
    a = jnp.minimum(i1, i2) - e_lo
    b = jnp.maximum(i1, i2) - e_lo
    lex = a * (7.0 - a) * 0.5 + (b - a - 1.0)
    pair = lex
    for src, dst in enumerate(PAIR_OF_LEX):
        if src != dst:
            pair = jnp.where(lex == float(src), float(dst), pair)
    cls = g_idx * float(N_PAIRS) + pair
    gate_a = jnp.where(first_low, w1, w2)
    gate_b = jnp.where(first_low, w2, w1)
    row_ref[:, D_MODEL:] = jnp.where(lane_i == 0, gate_a, jnp.where(lane_i == 1, gate_b, 0.0))

    onehot = lane_f == cls
    before = _dot(tri_ref[...], jnp.where(onehot, 1.0, 0.0).astype(_bf16))
    rank = jnp.sum(jnp.where(onehot, before + cnt_ref[...], 0.0), -1, keepdims=True)
    cnt_ref[...] += jnp.sum(jnp.where(onehot, 1.0, 0.0), 0, keepdims=True)
    meta = jnp.where(lane_i == 0, cls, jnp.where(lane_i == 1, rank, 0.0))
    meta_ref[...] = jnp.transpose(meta)[:SUBLANES, :]


def _merge(x, o_na, gu, vn, g_a, sh_f, sc_f, ws_cat, bs_tab, out_gain, w_out, w_r, b_r, tri, tm):
    bsz, length, _ = x.shape
    mod = pl.BlockSpec((None, 1, D_MODEL), lambda b, i: (b, 0, 0))
    tok = lambda n: pl.BlockSpec((None, tm, n), lambda b, i: (b, i, 0))
    const = lambda *shape: pl.BlockSpec(shape, lambda b, i: (0,) * len(shape))
    return pl.pallas_call(
        functools.partial(_merge_kernel, tm=tm),
        grid=(bsz, length // tm),
        in_specs=[tok(D_MODEL),
                  pl.BlockSpec((None, HEAD_PAIRS, tm, LANES), lambda b, i: (b, 0, i, 0)),
                  tok(SGU_W), tok(SGU_W), mod, mod, mod,
                  const(SGU_W // LANES, SGU_CHUNK, 2 * SGU_CHUNK),
                  const(SGU_CHUNK, SGU_W),
                  const(1, D_MODEL),
                  const(D_MODEL, D_MODEL),
                  const(D_MODEL, LANES),
                  const(1, LANES),
                  const(tm, tm)],
        out_specs=[tok(D_MODEL), tok(ROW_W),
                   pl.BlockSpec((None, SUBLANES, tm), lambda b, i: (b, 0, i)),
                   const(1, LANES)],
        out_shape=[jax.ShapeDtypeStruct((bsz, length, D_MODEL), _f32),
                   jax.ShapeDtypeStruct((bsz, length, ROW_W), _f32),
                   jax.ShapeDtypeStruct((bsz, SUBLANES, length), _f32),
                   jax.ShapeDtypeStruct((1, LANES), _f32)],
        compiler_params=_params("arbitrary", "arbitrary"),
        name="merge",
    )(x, o_na, gu, vn, g_a, sh_f, sc_f, ws_cat, bs_tab, out_gain, w_out, w_r, b_r, tri)


def _combine_kernel(pos_ref, x_ref, g_ref, ys_ref, o_ref, buf_ref, sem, *, tc):
    def issue(j, carry):
        pltpu.make_async_copy(ys_ref.at[pl.ds(pos_ref[0, j], 1)], buf_ref.at[pl.ds(j, 1)], sem).start()
        return carry

    lax.fori_loop(0, tc, issue, 0, unroll=8)
    pltpu.make_async_copy(ys_ref.at[pl.ds(0, tc)], buf_ref, sem).wait()
    o_ref[...] = x_ref[...] + g_ref[...] * buf_ref[...]


def _combine(x, ys, pos, g, tc):
    bsz, length, _ = x.shape
    per_batch = length // tc
    tok = pl.BlockSpec((None, tc, D_MODEL), lambda b, i: (b, i, 0))
    return pl.pallas_call(
        functools.partial(_combine_kernel, tc=tc),
        grid=(bsz, per_batch),
        in_specs=[pl.BlockSpec((None, 1, tc), lambda b, i: (b * per_batch + i, 0, 0), memory_space=pltpu.SMEM),
                  tok,
                  pl.BlockSpec((None, 1, D_MODEL), lambda b, i: (b, 0, 0)),
                  pl.BlockSpec(memory_space=pl.ANY)],
        out_specs=tok,
        out_shape=jax.ShapeDtypeStruct(x.shape, _f32),
        scratch_shapes=[pltpu.VMEM((tc, D_MODEL), _f32), pltpu.SemaphoreType.DMA(())],
        compiler_params=_params("arbitrary", "arbitrary"),
        name="combine",
    )(pos.reshape(bsz * per_batch, 1, tc), x, g, ys)


def _experts_kernel(ea_ref, eb_ref, chg_ref, nlive_ref, src_ref, nsrc_ref, rows_ref, w1a_ref, w3a_ref, w2a_ref,
                    w1b_ref, w3b_ref, w2b_ref, y_ref, c1a, c3a, c2a, c1b, c3b, c2b, xbuf0, xbuf1, sems,
                    *, tile, ntiles):
    t = pl.program_id(0)
    live = t < nlive_ref[0]
    slot = t % 2
    bufs = (xbuf0, xbuf1)

    def row_copy(idx_ref, j, buf_slot):
        return pltpu.make_async_copy(rows_ref.at[pl.ds(idx_ref[0, j], 1)],
                                     bufs[buf_slot].at[pl.ds(j, 1)], sems.at[buf_slot])

    def wait_rows(buf_slot):
        pltpu.make_async_copy(rows_ref.at[pl.ds(0, tile)], bufs[buf_slot], sems.at[buf_slot]).wait()

    @pl.when(t == 0)
    def _():
        def issue(j, carry):
            row_copy(src_ref, j, 0).start()
            return carry
        lax.fori_loop(0, tile, issue, 0, unroll=8)

    for s in range(2):
        @pl.when((t <= nlive_ref[0]) & (slot == s))
        def _():
            wait_rows(s)

    @pl.when(live & ((chg_ref[t] & 1) != 0))
    def _():
        for src, dst in ((w1a_ref, c1a), (w3a_ref, c3a), (w2a_ref, c2a)):
            dst[...] = src[...].astype(_bf16)

    @pl.when(live & ((chg_ref[t] & 2) != 0))
    def _():
        for src, dst in ((w1b_ref, c1b), (w3b_ref, c3b), (w2b_ref, c2b)):
            dst[...] = src[...].astype(_bf16)

    for s in range(2):
        @pl.when(live & (slot == s))
        def _():
            xs = bufs[s][:, :D_MODEL].astype(_bf16)
            gates = bufs[s][:, D_MODEL:]
            for j in range(tile):
                row_copy(nsrc_ref, j, 1 - s).start()

            def expert(w1_ref, w3_ref, w2_ref):
                hid = jax.nn.silu(_dot(xs, w1_ref[...])) * _dot(xs, w3_ref[...])
                return _dot(hid.astype(_bf16), w2_ref[...])

            y_ref[...] = gates[:, 0:1] * expert(c1a, c3a, c2a) + gates[:, 1:2] * expert(c1b, c3b, c2b)

        @pl.when(live & (slot == s) & (t == ntiles - 1))
        def _():
            wait_rows(1 - s)

    @pl.when(jnp.logical_not(live))
    def _():
        y_ref[...] = jnp.zeros_like(y_ref)


def _experts(rows, src, tile_ea, tile_eb, chg, nlive, w1, w3, w2, layer, tile):
    ntiles = src.shape[0] // tile
    up_a = pl.BlockSpec((None, None, D_MODEL, D_EXPERT), lambda t, ea, eb, cg, nl: (layer, ea[t], 0, 0))
    dn_a = pl.BlockSpec((None, None, D_EXPERT, D_MODEL), lambda t, ea, eb, cg, nl: (layer, ea[t], 0, 0))
    up_b = pl.BlockSpec((None, None, D_MODEL, D_EXPERT), lambda t, ea, eb, cg, nl: (layer, eb[t], 0, 0))
    dn_b = pl.BlockSpec((None, None, D_EXPERT, D_MODEL), lambda t, ea, eb, cg, nl: (layer, eb[t], 0, 0))
    grid_spec = pltpu.PrefetchScalarGridSpec(
        num_scalar_prefetch=4,
        grid=(ntiles,),
        in_specs=[pl.BlockSpec((None, 1, tile), lambda t, ea, eb, cg, nl: (t, 0, 0), memory_space=pltpu.SMEM),
                  pl.BlockSpec((None, 1, tile), lambda t, ea, eb, cg, nl: (jnp.minimum(t + 1, ntiles - 1), 0, 0),
                               memory_space=pltpu.SMEM),
                  pl.BlockSpec(memory_space=pl.ANY),
                  up_a, up_a, dn_a, up_b, up_b, dn_b],
        out_specs=pl.BlockSpec((tile, D_MODEL), lambda t, ea, eb, cg, nl: (t, 0)),
        scratch_shapes=[pltpu.VMEM((D_MODEL, D_EXPERT), _bf16), pltpu.VMEM((D_MODEL, D_EXPERT), _bf16),
                        pltpu.VMEM((D_EXPERT, D_MODEL), _bf16)] * 2
        + [pltpu.VMEM((tile, ROW_W), _f32), pltpu.VMEM((tile, ROW_W), _f32), pltpu.SemaphoreType.DMA((2,))],
    )
    src3 = src.reshape(ntiles, 1, tile)
    return pl.pallas_call(
        functools.partial(_experts_kernel, tile=tile, ntiles=ntiles),
        grid_spec=grid_spec,
        out_shape=jax.ShapeDtypeStruct((src.shape[0], D_MODEL), _f32),
        compiler_params=_params("arbitrary"),
        name="experts",
    )(tile_ea, tile_eb, chg, nlive, src3, src3, rows, w1, w3, w2, w1, w3, w2)


def _sorted_rows_kernel(pos_ref, src_ref, *, tb):
    t = pl.program_id(0)

    @pl.when(t == 0)
    def _():
        def clear(j, carry):
            src_ref[j] = 0
            return carry
        lax.fori_loop(0, src_ref.shape[0], clear, 0, unroll=8)

    def put(j, carry):
        src_ref[pos_ref[0, j]] = t * tb + j
        return carry

    lax.fori_loop(0, tb, put, 0, unroll=8)


def _sorted_rows(pos, padded, tb):
    n = pos.shape[0]
    return pl.pallas_call(
        functools.partial(_sorted_rows_kernel, tb=tb),
        grid=(n // tb,),
        in_specs=[pl.BlockSpec((None, 1, tb), lambda t: (t, 0, 0), memory_space=pltpu.SMEM)],
        out_specs=pl.BlockSpec((padded,), lambda t: (0,), memory_space=pltpu.SMEM),
        out_shape=jax.ShapeDtypeStruct((padded,), jnp.int32),
        compiler_params=_params("arbitrary"),
        name="sorted_rows",
    )(pos.reshape(n // tb, 1, tb))


def _dispatch_plan(cls, rank, counts, tile):
    n = cls.shape[0]
    padded = -(-(n + N_CLASSES * (tile - 1)) // tile) * tile
    ntiles = padded // tile
    pcounts = (counts + tile - 1) // tile * tile
    pad_end = jnp.cumsum(pcounts)
    pad_off = pad_end - pcounts
    pos = pad_off[cls] + rank
    nlive = pad_end[-1] // tile
    tile_start = jnp.arange(ntiles, dtype=jnp.int32) * tile
    tile_cls = jnp.sum((pad_end[None, :] <= jnp.minimum(tile_start, pad_end[-1] - 1)[:, None]).astype(jnp.int32), axis=1)
    tile_cls = jnp.minimum(tile_cls, N_CLASSES - 1)
    grp = tile_cls // N_PAIRS
    pair = tile_cls % N_PAIRS
    tile_ea = grp * EXPERTS_PER_GROUP + jnp.asarray(PAIR_A, jnp.int32)[pair]
    tile_eb = grp * EXPERTS_PER_GROUP + jnp.asarray(PAIR_B, jnp.int32)[pair]
    differs = lambda e: jnp.concatenate([jnp.ones((1,), jnp.int32), (e[1:] != e[:-1]).astype(jnp.int32)])
    chg = differs(tile_ea) + 2 * differs(tile_eb)
    return pos.astype(jnp.int32), padded, tile_ea, tile_eb, chg, nlive.astype(jnp.int32).reshape(1)


def _ffn(x_new, rows, meta, counts, g_f, w1, w3, w2, layer, tile, tdma):
    bsz, length, _ = x_new.shape
    n = bsz * length
    cls = meta[:, 0, :].reshape(n).astype(jnp.int32)
    rank = meta[:, 1, :].reshape(n).astype(jnp.int32)
    cnt = counts[0, :N_CLASSES].astype(jnp.int32)
    pos, padded, tile_ea, tile_eb, chg, nlive = _dispatch_plan(cls, rank, cnt, tile)
    src = _sorted_rows(pos, padded, min(n, 4096))
    ys = _experts(rows.reshape(n, ROW_W), src, tile_ea, tile_eb, chg, nlive, w1, w3, w2, layer, tile)
    return _combine(x_new, ys, pos, g_f, tdma)


def kernel(x, c, ctx, c_ctx, w_ada, b_ada, w_in, q_gain, k_gain, rpb, sgu_ln, sgu_w, sgu_b, out_gain,
           w_out, rg_w, rg_b, re_w, re_b, w1, w3, w2):
    bsz, seq, _ = x.shape
    depth = w_ada.shape[0]
    rows = seq // GRID_W
    assert seq % Q_BLOCK == 0 and rows // Q_ROWS >= 3 and ctx.shape[1] % SGU_CHUNK == 0
    tm_x, tm_c = 512, ctx.shape[1]

    cond = jnp.zeros((SUBLANES, D_MODEL), _f32).at[:bsz].set(c).at[bsz].set(c_ctx)
    mods = _adaln(cond, w_ada, b_ada).reshape(depth, SUBLANES, N_MOD, D_MODEL)

    head_id = np.arange(NA_W) // NA_HEAD_DIM
    avg = jnp.asarray((head_id[:, None] == head_id[None, :]).astype(np.float32) / NA_HEAD_DIM, _bf16)
    win_plan = _window_plan(rows)
    tri = {tm: jnp.asarray(np.tril(np.ones((tm, tm), np.float32), -1), _bf16) for tm in {tm_x, tm_c}}

    for i in range(depth):
        last = i == depth - 1
        mx = [mods[i, :bsz, j][:, None, :] for j in range(N_MOD)]
        mc = [jnp.broadcast_to(mods[i, bsz, j][None, None, :], (bsz, 1, D_MODEL)) for j in range(N_MOD)]
        w_in_b = w_in[i].astype(_bf16)
        qg = (jnp.tile(q_gain[i], NA_HEADS) * NA_HEAD_DIM ** -0.5)[None, :]
        kg = jnp.tile(k_gain[i], NA_HEADS)[None, :]
        ln_g = sgu_ln[i][None, :]
        ws_cat = jnp.concatenate([sgu_w[i, 0::2], sgu_w[i, 1::2]], axis=2).astype(_bf16)
        bs_tab = jnp.repeat(sgu_b[i].T, SGU_GROUP_DIM, axis=1)
        og = out_gain[i][None, :]
        w_out_b = w_out[i].astype(_bf16)
        w_r = jnp.concatenate([rg_w[i], jnp.transpose(re_w[i], (1, 0, 2)).reshape(D_MODEL, N_EXPERTS),
                               jnp.zeros((D_MODEL, LANES - N_GROUPS - N_EXPERTS), _f32)], axis=1).astype(_bf16)
        b_r = jnp.concatenate([rg_b[i], re_b[i].reshape(-1),
                               jnp.zeros((LANES - N_GROUPS - N_EXPERTS,), _f32)])[None, :]
        bycol = _window_col_table(rpb[i])

        q, k, v, gu, vn = _proj(x, mx[0], mx[1], w_in_b, avg, qg, kg, ln_g, tm_x)
        qc, kc, vc, guc, vnc = _proj(ctx, mc[0], mc[1], w_in_b, avg, qg, kg, ln_g, tm_c)
        o_na = _attention(q, k, v, kc, vc, bycol, win_plan)
        x_new, xrows, meta, counts = _merge(x, o_na, gu, vn, mx[2], mx[3], mx[4], ws_cat, bs_tab, og,
                                            w_out_b, w_r, b_r, tri[tm_x], tm_x)
        x = _ffn(x_new, xrows, meta, counts, mx[5], w1, w3, w2, i, 256, 1024)
        if not last:
            oc_na = _ctx_attention(qc, kc, vc)
            c_new, crows, cmeta, ccounts = _merge(ctx, oc_na, guc, vnc, mc[2], mc[3], mc[4], ws_cat, bs_tab,
                                                  og, w_out_b, w_r, b_r, tri[tm_c], tm_c)
            ctx = _ffn(c_new, crows, cmeta, ccounts, mc[5], w1, w3, w2, i, 64, tm_c)
    return x
```

```python
import functools

import jax
import jax.numpy as jnp
import numpy as np
from jax import lax
from jax.experimental import pallas as pl
from jax.experimental.pallas import tpu as pltpu

D_MODEL = 1024
GRID_W = 64
NA_HEADS = 8
NA_HEAD_DIM = 64
NA_W = NA_HEADS * NA_HEAD_DIM
NA_KH = 8
NA_KW = 16
SGU_GROUPS = 8
SGU_W = D_MODEL - NA_W
SGU_GROUP_DIM = SGU_W // SGU_GROUPS
SGU_CHUNK = 128
IN_COLS = 3 * NA_W + 2 * SGU_W
N_GROUPS = 4
EXPERTS_PER_GROUP = 4
N_EXPERTS = N_GROUPS * EXPERTS_PER_GROUP
D_EXPERT = D_MODEL // 2
N_MOD = 6
EPS = 1e-6

LANES = 128
SUBLANES = 8
HEAD_PAIRS = NA_W // LANES
Q_ROWS = 8
Q_BLOCK = Q_ROWS * GRID_W
WIN_ROWS = Q_ROWS + NA_KH - 1
WIN_BLOCK = WIN_ROWS * GRID_W
PAIR_ROWS = 2
PAIR_BLOCK = PAIR_ROWS * GRID_W
PAIR_WIN_ROWS = PAIR_ROWS + NA_KH - 1
PAIR_WIN = PAIR_WIN_ROWS * GRID_W
PAIRS_IN_FLIGHT = 2
N_DR = 2 * NA_KH
N_PAIRS = 6
N_CLASSES = N_GROUPS * N_PAIRS
PAIR_A = (0, 0, 1, 1, 0, 2)
PAIR_B = (1, 2, 2, 3, 3, 3)
PAIR_OF_LEX = (0, 1, 4, 2, 3, 5)
ROW_W = D_MODEL + LANES
MXU_TILE = 256
MASKED = -1e30
VMEM_LIMIT_BYTES = 56 * 1024 * 1024

_f32 = jnp.float32
_bf16 = jnp.bfloat16


def _params(*semantics):
    return pltpu.CompilerParams(dimension_semantics=semantics, vmem_limit_bytes=VMEM_LIMIT_BYTES)


def _dot(a, b):
    return jnp.dot(a, b, preferred_element_type=_f32)


def _dot_nt(a, b):
    return lax.dot_general(a, b, (((1,), (1,)), ((), ())), preferred_element_type=_f32)


def _adaln_kernel(cond_ref, w_ref, b_ref, out_ref):
    cond = cond_ref[...]
    act = cond * jax.nn.sigmoid(cond)
    out_ref[...] = _dot(act.astype(_bf16), w_ref[...].astype(_bf16)) + b_ref[...]


def _adaln(cond, w_ada, b_ada):
    depth = w_ada.shape[0]
    cols = N_MOD * D_MODEL
    col_block = 1536
    return pl.pallas_call(
        _adaln_kernel,
        grid=(depth, cols // col_block),
        in_specs=[
            pl.BlockSpec((SUBLANES, D_MODEL), lambda d, j: (0, 0)),
            pl.BlockSpec((None, D_MODEL, col_block), lambda d, j: (d, 0, j)),
            pl.BlockSpec((None, 1, col_block), lambda d, j: (d, 0, j)),
        ],
        out_specs=pl.BlockSpec((None, SUBLANES, col_block), lambda d, j: (d, 0, j)),
        out_shape=jax.ShapeDtypeStruct((depth, SUBLANES, cols), _f32),
        compiler_params=_params("arbitrary", "arbitrary"),
        name="adaln",
    )(cond, w_ada, b_ada.reshape(depth, 1, cols))


def _proj_kernel(x_ref, sh_ref, sc_ref, w_ref, avg_ref, qg_ref, kg_ref, ln_ref,
                 q_ref, k_ref, v_ref, gu_ref, vn_ref):
    x = x_ref[...]
    h = x * lax.rsqrt(jnp.mean(x * x, -1, keepdims=True) + EPS) * (1.0 + sc_ref[...]) + sh_ref[...]
    p = _dot(h.astype(_bf16), w_ref[...])
    q = p[:, :NA_W]
    k = p[:, NA_W:2 * NA_W]
    v = p[:, 2 * NA_W:3 * NA_W]
    u = p[:, 3 * NA_W:3 * NA_W + SGU_W]
    vs = p[:, 3 * NA_W + SGU_W:]
    def head_mean(t):
        sq = (t * t).astype(_bf16)
        return jnp.concatenate([_dot(sq[:, c:c + MXU_TILE], avg_ref[...]) for c in range(0, NA_W, MXU_TILE)], axis=1)

    q_ms = head_mean(q)
    k_ms = head_mean(k)
    qn = (q * lax.rsqrt(q_ms + EPS) * qg_ref[...]).astype(_bf16)
    kn = (k * lax.rsqrt(k_ms + EPS) * kg_ref[...]).astype(_bf16)
    vb = v.astype(_bf16)
    for hp in range(HEAD_PAIRS):
        cols = slice(hp * LANES, (hp + 1) * LANES)
        q_ref[hp] = qn[:, cols]
        k_ref[hp] = kn[:, cols]
        v_ref[hp] = vb[:, cols]
    gu_ref[...] = jax.nn.gelu(u).astype(_bf16)
    gv = jax.nn.gelu(vs)
    mu = jnp.mean(gv, -1, keepdims=True)
    cen = gv - mu
    var = jnp.mean(cen * cen, -1, keepdims=True)
    vn_ref[...] = (cen * lax.rsqrt(var + EPS) * ln_ref[...]).astype(_bf16)


def _proj(x, sh, sc, w_in, avg, qg, kg, ln_g, tm):
    bsz, length, _ = x.shape
    vec = lambda n: pl.BlockSpec((1, n), lambda b, i: (0, 0))
    mod = pl.BlockSpec((None, 1, D_MODEL), lambda b, i: (b, 0, 0))
    tok = lambda n: pl.BlockSpec((None, tm, n), lambda b, i: (b, i, 0))
    hpm = pl.BlockSpec((None, HEAD_PAIRS, tm, LANES), lambda b, i: (b, 0, i, 0))
    hpm_shape = jax.ShapeDtypeStruct((bsz, HEAD_PAIRS, length, LANES), _bf16)
    sgu_shape = jax.ShapeDtypeStruct((bsz, length, SGU_W), _bf16)
    return pl.pallas_call(
        _proj_kernel,
        grid=(bsz, length // tm),
        in_specs=[tok(D_MODEL), mod, mod,
                  pl.BlockSpec((D_MODEL, IN_COLS), lambda b, i: (0, 0)),
                  pl.BlockSpec((MXU_TILE, MXU_TILE), lambda b, i: (0, 0)),
                  vec(NA_W), vec(NA_W), vec(SGU_W)],
        out_specs=[hpm, hpm, hpm, tok(SGU_W), tok(SGU_W)],
        out_shape=[hpm_shape, hpm_shape, hpm_shape, sgu_shape, sgu_shape],
        compiler_params=_params("arbitrary", "arbitrary"),
        name="proj",
    )(x, sh, sc, w_in, avg, qg, kg, ln_g)


def _attend(blocks):
    add = lambda a, b: a + b
    m_rows = blocks[0][0].shape[0]
    lane = lax.broadcasted_iota(jnp.int32, blocks[0][0].shape, 1)
    low = lane < NA_HEAD_DIM
    stacked = [jnp.concatenate([jnp.where(low, q, jnp.zeros_like(q)), jnp.where(low, jnp.zeros_like(q), q)], axis=0)
               for q, _, _, _ in blocks]
    scores = [[_dot_nt(qs, key) if bias is None else _dot_nt(qs, key) + bias[...]
               for key, bias in zip(keys, biases)] for qs, (_, keys, _, biases) in zip(stacked, blocks)]
    maxes = [functools.reduce(jnp.maximum, [jnp.max(s, -1, keepdims=True) for s in ss]) for ss in scores]
    probs = [[jnp.exp(s - m) for s in ss] for ss, m in zip(scores, maxes)]
    dens = [functools.reduce(add, [jnp.sum(p, -1, keepdims=True) for p in ps]) for ps in probs]
    accs = [functools.reduce(add, [_dot(p.astype(_bf16), val) for p, val in zip(ps, blk[2])])
            for ps, blk in zip(probs, blocks)]
    outs = [acc / den for acc, den in zip(accs, dens)]
    return [jnp.where(low, o2[:m_rows], o2[m_rows:]) for o2 in outs]


def _attn_kernel(off_ref, var_ref, q_ref, k_ref, v_ref, kc_ref, vc_ref, bycol_ref, o_ref, bias_ref, *, pieces):
    i = pl.program_id(2)

    @pl.when(i == 0)
    def _():
        for var, piece in enumerate(pieces):
            for hh in range(2):
                for j in range(PAIR_ROWS):
                    for kr in range(PAIR_WIN_ROWS):
                        lo = (kr % 2) * GRID_W
                        row = hh * PAIR_BLOCK + j * GRID_W
                        bias_ref[var, row:row + GRID_W, kr * GRID_W:(kr + 1) * GRID_W] = (
                            bycol_ref[hh, piece[j][kr], :, lo:lo + GRID_W])

    kc = kc_ref[...]
    vc = vc_ref[...]
    npair = Q_ROWS // PAIR_ROWS
    for g0 in range(0, npair, PAIRS_IN_FLIGHT):
        blocks = []
        for g in range(g0, g0 + PAIRS_IN_FLIGHT):
            start = pl.multiple_of(off_ref[i * npair + g] * GRID_W, GRID_W)
            var = var_ref[i * npair + g]
            blocks.append((q_ref[g * PAIR_BLOCK:(g + 1) * PAIR_BLOCK, :],
                           [k_ref[pl.ds(start, PAIR_WIN), :], kc],
                           [v_ref[pl.ds(start, PAIR_WIN), :], vc],
                           [bias_ref.at[var], None]))
        for g, out in zip(range(g0, g0 + PAIRS_IN_FLIGHT), _attend(blocks)):
            o_ref[g * PAIR_BLOCK:(g + 1) * PAIR_BLOCK, :] = out.astype(o_ref.dtype)


def _attention(q, k, v, kc, vc, bycol, plan):
    off_tab, var_tab, pieces = plan
    bsz, _, seq, _ = q.shape
    clen = kc.shape[2]
    nblk = seq // Q_BLOCK

    def win_start(i):
        return jnp.clip(i * Q_ROWS - NA_KH // 2, 0, seq // GRID_W - WIN_ROWS) * GRID_W

    qspec = pl.BlockSpec((None, None, Q_BLOCK, LANES), lambda hp, b, i, *_: (b, hp, i, 0))
    wspec = pl.BlockSpec((None, None, pl.Element(WIN_BLOCK), pl.Element(LANES)),
                         lambda hp, b, i, *_: (b, hp, win_start(i), 0))
    cspec = pl.BlockSpec((None, None, clen, LANES), lambda hp, b, i, *_: (b, hp, 0, 0))
    tspec = pl.BlockSpec((None, 2, N_DR, GRID_W, LANES), lambda hp, b, i, *_: (hp, 0, 0, 0, 0))
    grid_spec = pltpu.PrefetchScalarGridSpec(
        num_scalar_prefetch=2,
        grid=(HEAD_PAIRS, bsz, nblk),
        in_specs=[qspec, wspec, wspec, cspec, cspec, tspec],
        out_specs=qspec,
        scratch_shapes=[pltpu.VMEM((len(pieces), 2 * PAIR_BLOCK, PAIR_WIN), _f32)],
    )
    return pl.pallas_call(
        functools.partial(_attn_kernel, pieces=pieces),
        grid_spec=grid_spec,
        out_shape=jax.ShapeDtypeStruct(q.shape, _bf16),
        compiler_params=_params("arbitrary", "arbitrary", "arbitrary"),
        name="attention",
    )(off_tab, var_tab, q, k, v, kc, vc, bycol)


def _ctx_attn_kernel(q_ref, k_ref, v_ref, o_ref):
    o_ref[...] = _attend([(q_ref[...], [k_ref[...]], [v_ref[...]], [None])])[0].astype(o_ref.dtype)


def _ctx_attention(q, k, v):
    bsz, _, clen, _ = q.shape
    spec = pl.BlockSpec((None, None, clen, LANES), lambda b, hp: (b, hp, 0, 0))
    return pl.pallas_call(
        _ctx_attn_kernel,
        grid=(bsz, HEAD_PAIRS),
        in_specs=[spec, spec, spec],
        out_specs=spec,
        out_shape=jax.ShapeDtypeStruct(q.shape, _bf16),
        compiler_params=_params("arbitrary", "arbitrary"),
        name="ctx_attention",
    )(q, k, v)


def _window_plan(rows):
    patterns, off_tab, var_tab = {}, [], []
    for r in range(0, rows, PAIR_ROWS):
        ws = int(np.clip(r - NA_KH // 2, 0, rows - PAIR_WIN_ROWS))
        r0 = r // Q_ROWS * Q_ROWS
        block_ws = int(np.clip(r0 - NA_KH // 2, 0, rows - WIN_ROWS))
        assert 0 <= ws - block_ws <= WIN_ROWS - PAIR_WIN_ROWS
        piece = []
        for j in range(PAIR_ROWS):
            r_start = int(np.clip(r + j - NA_KH // 2, 0, rows - NA_KH))
            assert ws <= r_start and r_start + NA_KH <= ws + PAIR_WIN_ROWS
            piece.append(tuple(ws + kr - (r + j) + NA_KH - 1 if r_start <= ws + kr < r_start + NA_KH else N_DR - 1
                               for kr in range(PAIR_WIN_ROWS)))
        var_tab.append(patterns.setdefault(tuple(piece), len(patterns)))
        off_tab.append(ws - block_ws)
    return jnp.asarray(off_tab, jnp.int32), jnp.asarray(var_tab, jnp.int32), tuple(patterns)


def _window_col_table(rpb):
    col = np.arange(GRID_W)
    col_start = np.clip(col - NA_KW // 2, 0, GRID_W - NA_KW)
    col_ok = (col[None, :] >= col_start[:, None]) & (col[None, :] < col_start[:, None] + NA_KW)
    dc = np.where(col_ok, col[None, :] - col[:, None] + NA_KW - 1, 0)
    onehot = np.zeros((2 * NA_KW - 1, GRID_W * GRID_W), np.float32)
    onehot[dc.reshape(-1), np.arange(GRID_W * GRID_W)] = col_ok.reshape(-1)
    by_col = jnp.dot(rpb.reshape(-1, 2 * NA_KW - 1), jnp.asarray(onehot), precision=lax.Precision.HIGHEST)
    by_col = by_col.reshape(NA_HEADS, 2 * NA_KH - 1, GRID_W, GRID_W)
    by_col = jnp.where(jnp.asarray(col_ok)[None, None], by_col, MASKED)
    by_col = jnp.concatenate([by_col, jnp.full((NA_HEADS, 1, GRID_W, GRID_W), MASKED, _f32)], axis=1)
    by_col = jnp.concatenate([by_col, by_col], axis=-1)
    return by_col.reshape(HEAD_PAIRS, 2, N_DR, GRID_W, LANES)


def _merge_kernel(x_ref, o_ref, gu_ref, vn_ref, ga_ref, shf_ref, scf_ref, ws_ref, bs_ref, og_ref,
                  wout_ref, wr_ref, br_ref, tri_ref, xn_ref, row_ref, meta_ref, cnt_ref, *, tm):
    first = (pl.program_id(0) == 0) & (pl.program_id(1) == 0)

    @pl.when(first)
    def _():
        cnt_ref[...] = jnp.zeros_like(cnt_ref)

    nchunk = tm // SGU_CHUNK
    vn = vn_ref[...]
    lane = lax.broadcasted_iota(jnp.int32, (SGU_CHUNK, LANES), 1)
    low = lane < SGU_GROUP_DIM
    mixed_cols = []
    for gp in range(SGU_W // LANES):
        rhs_cols = []
        for n in range(nchunk):
            blk = vn[n * SGU_CHUNK:(n + 1) * SGU_CHUNK, gp * LANES:(gp + 1) * LANES]
            zero = jnp.zeros_like(blk)
            rhs_cols.append(jnp.concatenate([jnp.where(low, blk, zero), jnp.where(low, zero, blk)], axis=0))
        rhs = jnp.concatenate(rhs_cols, axis=1)
        mix = _dot(ws_ref[gp], rhs)
        mixed_cols.append(jnp.concatenate([mix[:, n * LANES:(n + 1) * LANES] for n in range(nchunk)], axis=0))
    mixed = jnp.concatenate(mixed_cols, axis=1)
    bias = jnp.concatenate([bs_ref[...]] * nchunk, axis=0)
    o_sg = gu_ref[...].astype(_f32) * (mixed + bias)
    o_na = jnp.concatenate([o_ref[hp].astype(_f32) for hp in range(HEAD_PAIRS)], axis=1)

    def rms(t):
        return t * lax.rsqrt(jnp.mean(t * t, -1, keepdims=True) + EPS)

    y = jnp.concatenate([rms(o_na), rms(o_sg)], axis=1) * og_ref[...]
    xn = x_ref[...] + ga_ref[...] * _dot(y.astype(_bf16), wout_ref[...])
    xn_ref[...] = xn
    hf = rms(xn) * (1.0 + scf_ref[...]) + shf_ref[...]
    row_ref[:, :D_MODEL] = hf

    lg = _dot(hf.astype(_bf16), wr_ref[...]) + br_ref[...]
    lane_i = lax.broadcasted_iota(jnp.int32, lg.shape, 1)
    lane_f = lane_i.astype(_f32)
    far = float(LANES)
    is_g = lane_i < N_GROUPS
    gl = jnp.where(is_g, lg, MASKED)
    gm = jnp.max(gl, -1, keepdims=True)
    p_top = 1.0 / jnp.sum(jnp.where(is_g, jnp.exp(gl - gm), 0.0), -1, keepdims=True)
    g_idx = jnp.min(jnp.where(is_g & (gl == gm), lane_f, far), -1, keepdims=True)
    e_lo = N_GROUPS + EXPERTS_PER_GROUP * g_idx
    is_e = (lane_f >= e_lo) & (lane_f < e_lo + EXPERTS_PER_GROUP)
    el = jnp.where(is_e, lg, MASKED)
    em = jnp.max(el, -1, keepdims=True)
    ee = jnp.where(is_e, jnp.exp(el - em), 0.0)
    pe = jnp.where(is_e, ee / jnp.sum(ee, -1, keepdims=True), -1.0)
    v1 = jnp.max(pe, -1, keepdims=True)
    i1 = jnp.min(jnp.where(pe == v1, lane_f, far), -1, keepdims=True)
    pe2 = jnp.where(lane_f == i1, -1.0, pe)
    v2 = jnp.max(pe2, -1, keepdims=True)
    i2 = jnp.min(jnp.where(pe2 == v2, lane_f, far), -1, keepdims=True)
    den = v1 + v2
    w1 = v1 / den * p_top
    w2 = v2 / den * p_top
    first_low = i1 < i2
    a = jnp.minimum(i1, i2) - e_lo
    b = jnp.maximum(i1, i2) - e_lo
    lex = a * (7.0 - a) * 0.5 + (b - a - 1.0)
    pair = lex
    for src, dst in enumerate(PAIR_OF_LEX):
        if src != dst:
            pair = jnp.where(lex == float(src), float(dst), pair)
    cls = g_idx * float(N_PAIRS) + pair
    gate_a = jnp.where(first_low, w1, w2)
    gate_b = jnp.where(first_low, w2, w1)
    row_ref[:, D_MODEL:] = jnp.where(lane_i == 0, gate_a, jnp.where(lane_i == 1, gate_b, 0.0))

    onehot = lane_f == cls
    before = _dot(tri_ref[...], jnp.where(onehot, 1.0, 0.0).astype(_bf16))
    rank = jnp.sum(jnp.where(onehot, before + cnt_ref[...], 0.0), -1, keepdims=True)
    cnt_ref[...] += jnp.sum(jnp.where(onehot, 1.0, 0.0), 0, keepdims=True)
    meta = jnp.where(lane_i == 0, cls, jnp.where(lane_i == 1, rank, 0.0))
    meta_ref[...] = jnp.transpose(meta)[:SUBLANES, :]


def _merge(x, o_na, gu, vn, g_a, sh_f, sc_f, ws_cat, bs_tab, out_gain, w_out, w_r, b_r, tri, tm):
    bsz, length, _ = x.shape
    mod = pl.BlockSpec((None, 1, D_MODEL), lambda b, i: (b, 0, 0))
    tok = lambda n: pl.BlockSpec((None, tm, n), lambda b, i: (b, i, 0))
    const = lambda *shape: pl.BlockSpec(shape, lambda b, i: (0,) * len(shape))
    return pl.pallas_call(
        functools.partial(_merge_kernel, tm=tm),
        grid=(bsz, length // tm),
        in_specs=[tok(D_MODEL),
                  pl.BlockSpec((None, HEAD_PAIRS, tm, LANES), lambda b, i: (b, 0, i, 0)),
                  tok(SGU_W), tok(SGU_W), mod, mod, mod,
                  const(SGU_W // LANES, SGU_CHUNK, 2 * SGU_CHUNK),
                  const(SGU_CHUNK, SGU_W),
                  const(1, D_MODEL),
                  const(D_MODEL, D_MODEL),
                  const(D_MODEL, LANES),
                  const(1, LANES),
                  const(tm, tm)],
        out_specs=[tok(D_MODEL), tok(ROW_W),
                   pl.BlockSpec((None, SUBLANES, tm), lambda b, i: (b, 0, i)),
                   const(1, LANES)],
        out_shape=[jax.ShapeDtypeStruct((bsz, length, D_MODEL), _f32),
                   jax.ShapeDtypeStruct((bsz, length, ROW_W), _f32),
                   jax.ShapeDtypeStruct((bsz, SUBLANES, length), _f32),
                   jax.ShapeDtypeStruct((1, LANES), _f32)],
        compiler_params=_params("arbitrary", "arbitrary"),
        name="merge",
    )(x, o_na, gu, vn, g_a, sh_f, sc_f, ws_cat, bs_tab, out_gain, w_out, w_r, b_r, tri)


def _issue_row_copies(count, make_copy):
    for j in range(count):
        make_copy(j).start(priority=j % 2)


def _dispatch_kernel(pos_ref, rows_ref, init_ref, xs_ref, sem, *, td):
    del init_ref
    _issue_row_copies(td, lambda j: pltpu.make_async_copy(
        rows_ref.at[pl.ds(j, 1)], xs_ref.at[pl.ds(pos_ref[0, j], 1)], sem))
    pltpu.make_async_copy(rows_ref, xs_ref.at[pl.ds(0, td)], sem).wait()


def _dispatch(rows, pos, padded, td):
    n = rows.shape[0]
    return pl.pallas_call(
        functools.partial(_dispatch_kernel, td=td),
        grid=(n // td,),
        in_specs=[pl.BlockSpec((None, 1, td), lambda i: (i, 0, 0), memory_space=pltpu.SMEM),
                  pl.BlockSpec((td, ROW_W), lambda i: (i, 0)),
                  pl.BlockSpec(memory_space=pl.ANY)],
        out_specs=pl.BlockSpec(memory_space=pl.ANY),
        out_shape=jax.ShapeDtypeStruct((padded, ROW_W), _f32),
        scratch_shapes=[pltpu.SemaphoreType.DMA(())],
        input_output_aliases={2: 0},
        compiler_params=_params("arbitrary"),
        name="dispatch",
    )(pos.reshape(n // td, 1, td), rows, jnp.zeros((padded, ROW_W), _f32))


def _combine_kernel(pos_ref, x_ref, g_ref, ys_ref, o_ref, buf_ref, sem, *, tc):
    _issue_row_copies(tc, lambda j: pltpu.make_async_copy(
        ys_ref.at[pl.ds(pos_ref[0, j], 1)], buf_ref.at[pl.ds(j, 1)], sem))
    pltpu.make_async_copy(ys_ref.at[pl.ds(0, tc)], buf_ref, sem).wait()
    o_ref[...] = x_ref[...] + g_ref[...] * buf_ref[...]


def _combine(x, ys, pos, g, tc):
    bsz, length, _ = x.shape
    per_batch = length // tc
    tok = pl.BlockSpec((None, tc, D_MODEL), lambda b, i: (b, i, 0))
    return pl.pallas_call(
        functools.partial(_combine_kernel, tc=tc),
        grid=(bsz, per_batch),
        in_specs=[pl.BlockSpec((None, 1, tc), lambda b, i: (b * per_batch + i, 0, 0), memory_space=pltpu.SMEM),
                  tok,
                  pl.BlockSpec((None, 1, D_MODEL), lambda b, i: (b, 0, 0)),
                  pl.BlockSpec(memory_space=pl.ANY)],
        out_specs=tok,
        out_shape=jax.ShapeDtypeStruct(x.shape, _f32),
        scratch_shapes=[pltpu.VMEM((tc, D_MODEL), _f32), pltpu.SemaphoreType.DMA(())],
        compiler_params=_params("arbitrary", "arbitrary"),
        name="combine",
    )(pos.reshape(bsz * per_batch, 1, tc), x, g, ys)


def _experts_kernel(ea_ref, eb_ref, chg_ref, nlive_ref, xs_ref, w1a_ref, w3a_ref, w2a_ref,
                    w1b_ref, w3b_ref, w2b_ref, y_ref, c1a, c3a, c2a, c1b, c3b, c2b):
    t = pl.program_id(0)
    live = t < nlive_ref[0]

    @pl.when(live & ((chg_ref[t] & 1) != 0))
    def _():
        for src, dst in ((w1a_ref, c1a), (w3a_ref, c3a), (w2a_ref, c2a)):
            dst[...] = src[...].astype(_bf16)

    @pl.when(live & ((chg_ref[t] & 2) != 0))
    def _():
        for src, dst in ((w1b_ref, c1b), (w3b_ref, c3b), (w2b_ref, c2b)):
            dst[...] = src[...].astype(_bf16)

    @pl.when(live)
    def _():
        xs = xs_ref[:, :D_MODEL].astype(_bf16)
        gates = xs_ref[:, D_MODEL:]

        def expert(w1_ref, w3_ref, w2_ref):
            hid = jax.nn.silu(_dot(xs, w1_ref[...])) * _dot(xs, w3_ref[...])
            return _dot(hid.astype(_bf16), w2_ref[...])

        y_ref[...] = gates[:, 0:1] * expert(c1a, c3a, c2a) + gates[:, 1:2] * expert(c1b, c3b, c2b)

    @pl.when(jnp.logical_not(live))
    def _():
        y_ref[...] = jnp.zeros_like(y_ref)


def _experts(xs, tile_ea, tile_eb, chg, nlive, w1, w3, w2, layer, tile):
    ntiles = xs.shape[0] // tile
    up_a = pl.BlockSpec((None, None, D_MODEL, D_EXPERT), lambda t, ea, eb, cg, nl: (layer, ea[t], 0, 0))
    dn_a = pl.BlockSpec((None, None, D_EXPERT, D_MODEL), lambda t, ea, eb, cg, nl: (layer, ea[t], 0, 0))
    up_b = pl.BlockSpec((None, None, D_MODEL, D_EXPERT), lambda t, ea, eb, cg, nl: (layer, eb[t], 0, 0))
    dn_b = pl.BlockSpec((None, None, D_EXPERT, D_MODEL), lambda t, ea, eb, cg, nl: (layer, eb[t], 0, 0))
    grid_spec = pltpu.PrefetchScalarGridSpec(
        num_scalar_prefetch=4,
        grid=(ntiles,),
        in_specs=[pl.BlockSpec((tile, ROW_W), lambda t, ea, eb, cg, nl: (t, 0)),
                  up_a, up_a, dn_a, up_b, up_b, dn_b],
        out_specs=pl.BlockSpec((tile, D_MODEL), lambda t, ea, eb, cg, nl: (t, 0)),
        scratch_shapes=[pltpu.VMEM((D_MODEL, D_EXPERT), _bf16), pltpu.VMEM((D_MODEL, D_EXPERT), _bf16),
                        pltpu.VMEM((D_EXPERT, D_MODEL), _bf16)] * 2,
    )
    return pl.pallas_call(
        _experts_kernel,
        grid_spec=grid_spec,
        out_shape=jax.ShapeDtypeStruct((xs.shape[0], D_MODEL), _f32),
        compiler_params=_params("arbitrary"),
        name="experts",
    )(tile_ea, tile_eb, chg, nlive, xs, w1, w3, w2, w1, w3, w2)


def _dispatch_plan(cls, rank, counts, tile):
    n = cls.shape[0]
    padded = -(-(n + N_CLASSES * (tile - 1)) // tile) * tile
    ntiles = padded // tile
    pcounts = (counts + tile - 1) // tile * tile
    pad_end = jnp.cumsum(pcounts)
    pad_off = pad_end - pcounts
    pos = pad_off[cls] + rank
    nlive = pad_end[-1] // tile
    tile_start = jnp.arange(ntiles, dtype=jnp.int32) * tile
    tile_cls = jnp.sum((pad_end[None, :] <= jnp.minimum(tile_start, pad_end[-1] - 1)[:, None]).astype(jnp.int32), axis=1)
    tile_cls = jnp.minimum(tile_cls, N_CLASSES - 1)
    grp = tile_cls // N_PAIRS
    pair = tile_cls % N_PAIRS
    tile_ea = grp * EXPERTS_PER_GROUP + jnp.asarray(PAIR_A, jnp.int32)[pair]
    tile_eb = grp * EXPERTS_PER_GROUP + jnp.asarray(PAIR_B, jnp.int32)[pair]
    differs = lambda e: jnp.concatenate([jnp.ones((1,), jnp.int32), (e[1:] != e[:-1]).astype(jnp.int32)])
    chg = differs(tile_ea) + 2 * differs(tile_eb)
    return pos.astype(jnp.int32), padded, tile_ea, tile_eb, chg, nlive.astype(jnp.int32).reshape(1)


def _ffn(x_new, rows, meta, counts, g_f, w1, w3, w2, layer, tile, tdma):
    bsz, length, _ = x_new.shape
    n = bsz * length
    cls = meta[:, 0, :].reshape(n).astype(jnp.int32)
    rank = meta[:, 1, :].reshape(n).astype(jnp.int32)
    cnt = counts[0, :N_CLASSES].astype(jnp.int32)
    pos, padded, tile_ea, tile_eb, chg, nlive = _dispatch_plan(cls, rank, cnt, tile)
    xs = _dispatch(rows.reshape(n, ROW_W), pos, padded, tdma)
    ys = _experts(xs, tile_ea, tile_eb, chg, nlive, w1, w3, w2, layer, tile)
    return _combine(x_new, ys, pos, g_f, tdma)


def kernel(x, c, ctx, c_ctx, w_ada, b_ada, w_in, q_gain, k_gain, rpb, sgu_ln, sgu_w, sgu_b, out_gain,
           w_out, rg_w, rg_b, re_w, re_b, w1, w3, w2):
    bsz, seq, _ = x.shape
    depth = w_ada.shape[0]
    rows = seq // GRID_W
    assert seq % Q_BLOCK == 0 and rows // Q_ROWS >= 3 and ctx.shape[1] % SGU_CHUNK == 0
    tm_x, tm_c = 512, ctx.shape[1]

    cond = jnp.zeros((SUBLANES, D_MODEL), _f32).at[:bsz].set(c).at[bsz].set(c_ctx)
    mods = _adaln(cond, w_ada, b_ada).reshape(depth, SUBLANES, N_MOD, D_MODEL)

    assert MXU_TILE % NA_HEAD_DIM == 0 and NA_W % MXU_TILE == 0
    head_id = np.arange(MXU_TILE) // NA_HEAD_DIM
    avg = jnp.asarray((head_id[:, None] == head_id[None, :]).astype(np.float32) / NA_HEAD_DIM, _bf16)
    win_plan = _window_plan(rows)
    tri = {tm: jnp.asarray(np.tril(np.ones((tm, tm), np.float32), -1), _bf16) for tm in {tm_x, tm_c}}

    for i in range(depth):
        last = i == depth - 1
        mx = [mods[i, :bsz, j][:, None, :] for j in range(N_MOD)]
        mc = [jnp.broadcast_to(mods[i, bsz, j][None, None, :], (bsz, 1, D_MODEL)) for j in range(N_MOD)]
        w_in_b = w_in[i].astype(_bf16)
        qg = (jnp.tile(q_gain[i], NA_HEADS) * NA_HEAD_DIM ** -0.5)[None, :]
        kg = jnp.tile(k_gain[i], NA_HEADS)[None, :]
        ln_g = sgu_ln[i][None, :]
        ws_cat = jnp.concatenate([sgu_w[i, 0::2], sgu_w[i, 1::2]], axis=2).astype(_bf16)
        bs_tab = jnp.repeat(sgu_b[i].T, SGU_GROUP_DIM, axis=1)
        og = out_gain[i][None, :]
        w_out_b = w_out[i].astype(_bf16)
        w_r = jnp.concatenate([rg_w[i], jnp.transpose(re_w[i], (1, 0, 2)).reshape(D_MODEL, N_EXPERTS),
                               jnp.zeros((D_MODEL, LANES - N_GROUPS - N_EXPERTS), _f32)], axis=1).astype(_bf16)
        b_r = jnp.concatenate([rg_b[i], re_b[i].reshape(-1),
                               jnp.zeros((LANES - N_GROUPS - N_EXPERTS,), _f32)])[None, :]
        bycol = _window_col_table(rpb[i])

        q, k, v, gu, vn = _proj(x, mx[0], mx[1], w_in_b, avg, qg, kg, ln_g, tm_x)
        qc, kc, vc, guc, vnc = _proj(ctx, mc[0], mc[1], w_in_b, avg, qg, kg, ln_g, tm_c)
        o_na = _attention(q, k, v, kc, vc, bycol, win_plan)
        x_new, xrows, meta, counts = _merge(x, o_na, gu, vn, mx[2], mx[3], mx[4], ws_cat, bs_tab, og,
                                            w_out_b, w_r, b_r, tri[tm_x], tm_x)
        x = _ffn(x_new, xrows, meta, counts, mx[5], w1, w3, w2, i, 256, 1024)
        if not last:
            oc_na = _ctx_attention(qc, kc, vc)
            c_new, crows, cmeta, ccounts = _merge(ctx, oc_na, guc, vnc, mc[2], mc[3], mc[4], ws_cat, bs_tab,
                                                  og, w_out_b, w_r, b_r, tri[tm_c], tm_c)
            ctx = _ffn(c_new, crows, cmeta, ccounts, mc[5], w1, w3, w2, i, 64, tm_c)
    return x
```

```python
import functools

import jax
import jax.numpy as jnp
import numpy as np
from jax import lax
from jax.experimental import pallas as pl
from jax.experimental.pallas import tpu as pltpu

D_MODEL = 1024
GRID_W = 64
NA_HEADS = 8
NA_HEAD_DIM = 64
NA_W = NA_HEADS * NA_HEAD_DIM
NA_KH = 8
NA_KW = 16
SGU_GROUPS = 8
SGU_W = D_MODEL - NA_W
SGU_GROUP_DIM = SGU_W // SGU_GROUPS
SGU_CHUNK = 128
IN_COLS = 3 * NA_W + 2 * SGU_W
N_GROUPS = 4
EXPERTS_PER_GROUP = 4
N_EXPERTS = N_GROUPS * EXPERTS_PER_GROUP
D_EXPERT = D_MODEL // 2
N_MOD = 6
EPS = 1e-6

LANES = 128
SUBLANES = 8
HEAD_PAIRS = NA_W // LANES
Q_ROWS = 8
Q_BLOCK = Q_ROWS * GRID_W
WIN_ROWS = Q_ROWS + NA_KH - 1
WIN_BLOCK = WIN_ROWS * GRID_W
PAIR_ROWS = 2
PAIR_BLOCK = PAIR_ROWS * GRID_W
PAIR_WIN_ROWS = PAIR_ROWS + NA_KH - 1
PAIR_WIN = PAIR_WIN_ROWS * GRID_W
PAIRS_IN_FLIGHT = 2
N_DR = 2 * NA_KH
N_PAIRS = 6
N_CLASSES = N_GROUPS * N_PAIRS
PAIR_A = (0, 0, 1, 1, 0, 2)
PAIR_B = (1, 2, 2, 3, 3, 3)
PAIR_OF_LEX = (0, 1, 4, 2, 3, 5)
ROW_W = D_MODEL + LANES
MXU_TILE = 256
MASKED = -1e30
VMEM_LIMIT_BYTES = 56 * 1024 * 1024

_f32 = jnp.float32
_bf16 = jnp.bfloat16


def _params(*semantics):
    return pltpu.CompilerParams(dimension_semantics=semantics, vmem_limit_bytes=VMEM_LIMIT_BYTES)


def _dot(a, b):
    return jnp.dot(a, b, preferred_element_type=_f32)


def _dot_nt(a, b):
    return lax.dot_general(a, b, (((1,), (1,)), ((), ())), preferred_element_type=_f32)


def _adaln_kernel(cond_ref, w_ref, b_ref, out_ref):
    cond = cond_ref[...]
    act = cond * jax.nn.sigmoid(cond)
    out_ref[...] = _dot(act.astype(_bf16), w_ref[...].astype(_bf16)) + b_ref[...]


def _adaln(cond, w_ada, b_ada):
    depth = w_ada.shape[0]
    cols = N_MOD * D_MODEL
    col_block = 1536
    return pl.pallas_call(
        _adaln_kernel,
        grid=(depth, cols // col_block),
        in_specs=[
            pl.BlockSpec((SUBLANES, D_MODEL), lambda d, j: (0, 0)),
            pl.BlockSpec((None, D_MODEL, col_block), lambda d, j: (d, 0, j)),
            pl.BlockSpec((None, 1, col_block), lambda d, j: (d, 0, j)),
        ],
        out_specs=pl.BlockSpec((None, SUBLANES, col_block), lambda d, j: (d, 0, j)),
        out_shape=jax.ShapeDtypeStruct((depth, SUBLANES, cols), _f32),
        compiler_params=_params("arbitrary", "arbitrary"),
        name="adaln",
    )(cond, w_ada, b_ada.reshape(depth, 1, cols))


def _proj_kernel(x_ref, sh_ref, sc_ref, w_ref, avg_ref, qg_ref, kg_ref, ln_ref,
                 q_ref, k_ref, v_ref, gu_ref, vn_ref):
    x = x_ref[...]
    h = x * lax.rsqrt(jnp.mean(x * x, -1, keepdims=True) + EPS) * (1.0 + sc_ref[...]) + sh_ref[...]
    p = _dot(h.astype(_bf16), w_ref[...])
    q = p[:, :NA_W]
    k = p[:, NA_W:2 * NA_W]
    v = p[:, 2 * NA_W:3 * NA_W]
    u = p[:, 3 * NA_W:3 * NA_W + SGU_W]
    vs = p[:, 3 * NA_W + SGU_W:]
    def head_mean(t):
        sq = (t * t).astype(_bf16)
        return jnp.concatenate([_dot(sq[:, c:c + MXU_TILE], avg_ref[...]) for c in range(0, NA_W, MXU_TILE)], axis=1)

    q_ms = head_mean(q)
    k_ms = head_mean(k)
    qn = (q * lax.rsqrt(q_ms + EPS) * qg_ref[...]).astype(_bf16)
    kn = (k * lax.rsqrt(k_ms + EPS) * kg_ref[...]).astype(_bf16)
    vb = v.astype(_bf16)
    for hp in range(HEAD_PAIRS):
        cols = slice(hp * LANES, (hp + 1) * LANES)
        q_ref[hp] = qn[:, cols]
        k_ref[hp] = kn[:, cols]
        v_ref[hp] = vb[:, cols]
    gu_ref[...] = jax.nn.gelu(u).astype(_bf16)
    gv = jax.nn.gelu(vs)
    mu = jnp.mean(gv, -1, keepdims=True)
    cen = gv - mu
    var = jnp.mean(cen * cen, -1, keepdims=True)
    vn_ref[...] = (cen * lax.rsqrt(var + EPS) * ln_ref[...]).astype(_bf16)


def _proj(x, sh, sc, w_in, avg, qg, kg, ln_g, tm):
    bsz, length, _ = x.shape
    vec = lambda n: pl.BlockSpec((1, n), lambda b, i: (0, 0))
    mod = pl.BlockSpec((None, 1, D_MODEL), lambda b, i: (b, 0, 0))
    tok = lambda n: pl.BlockSpec((None, tm, n), lambda b, i: (b, i, 0))
    hpm = pl.BlockSpec((None, HEAD_PAIRS, tm, LANES), lambda b, i: (b, 0, i, 0))
    hpm_shape = jax.ShapeDtypeStruct((bsz, HEAD_PAIRS, length, LANES), _bf16)
    sgu_shape = jax.ShapeDtypeStruct((bsz, length, SGU_W), _bf16)
    return pl.pallas_call(
        _proj_kernel,
        grid=(bsz, length // tm),
        in_specs=[tok(D_MODEL), mod, mod,
                  pl.BlockSpec((D_MODEL, IN_COLS), lambda b, i: (0, 0)),
                  pl.BlockSpec((MXU_TILE, MXU_TILE), lambda b, i: (0, 0)),
                  vec(NA_W), vec(NA_W), vec(SGU_W)],
        out_specs=[hpm, hpm, hpm, tok(SGU_W), tok(SGU_W)],
        out_shape=[hpm_shape, hpm_shape, hpm_shape, sgu_shape, sgu_shape],
        compiler_params=_params("arbitrary", "arbitrary"),
        name="proj",
    )(x, sh, sc, w_in, avg, qg, kg, ln_g)


def _attend(blocks):
    add = lambda a, b: a + b
    m_rows = blocks[0][0].shape[0]
    lane = lax.broadcasted_iota(jnp.int32, blocks[0][0].shape, 1)
    low = lane < NA_HEAD_DIM
    stacked = [jnp.concatenate([jnp.where(low, q, jnp.zeros_like(q)), jnp.where(low, jnp.zeros_like(q), q)], axis=0)
               for q, _, _, _ in blocks]
    scores = [[_dot_nt(qs, key) if bias is None else _dot_nt(qs, key) + bias[...]
               for key, bias in zip(keys, biases)] for qs, (_, keys, _, biases) in zip(stacked, blocks)]
    maxes = [functools.reduce(jnp.maximum, [jnp.max(s, -1, keepdims=True) for s in ss]) for ss in scores]
    probs = [[jnp.exp(s - m) for s in ss] for ss, m in zip(scores, maxes)]
    dens = [functools.reduce(add, [jnp.sum(p, -1, keepdims=True) for p in ps]) for ps in probs]
    accs = [functools.reduce(add, [_dot(p.astype(_bf16), val) for p, val in zip(ps, blk[2])])
            for ps, blk in zip(probs, blocks)]
    outs = [acc / den for acc, den in zip(accs, dens)]
    return [jnp.where(low, o2[:m_rows], o2[m_rows:]) for o2 in outs]


def _attn_kernel(off_ref, var_ref, q_ref, k_ref, v_ref, kc_ref, vc_ref, bycol_ref, o_ref, bias_ref, *, pieces):
    i = pl.program_id(2)

    @pl.when(i == 0)
    def _():
        for var, piece in enumerate(pieces):
            for hh in range(2):
                for j in range(PAIR_ROWS):
                    for kr in range(PAIR_WIN_ROWS):
                        lo = (kr % 2) * GRID_W
                        row = hh * PAIR_BLOCK + j * GRID_W
                        bias_ref[var, row:row + GRID_W, kr * GRID_W:(kr + 1) * GRID_W] = (
                            bycol_ref[hh, piece[j][kr], :, lo:lo + GRID_W])

    kc = kc_ref[...]
    vc = vc_ref[...]
    npair = Q_ROWS // PAIR_ROWS
    for g0 in range(0, npair, PAIRS_IN_FLIGHT):
        blocks = []
        for g in range(g0, g0 + PAIRS_IN_FLIGHT):
            start = pl.multiple_of(off_ref[i * npair + g] * GRID_W, GRID_W)
            var = var_ref[i * npair + g]
            blocks.append((q_ref[g * PAIR_BLOCK:(g + 1) * PAIR_BLOCK, :],
                           [k_ref[pl.ds(start, PAIR_WIN), :], kc],
                           [v_ref[pl.ds(start, PAIR_WIN), :], vc],
                           [bias_ref.at[var], None]))
        for g, out in zip(range(g0, g0 + PAIRS_IN_FLIGHT), _attend(blocks)):
            o_ref[g * PAIR_BLOCK:(g + 1) * PAIR_BLOCK, :] = out.astype(o_ref.dtype)


def _attention(q, k, v, kc, vc, bycol, plan):
    off_tab, var_tab, pieces = plan
    bsz, _, seq, _ = q.shape
    clen = kc.shape[2]
    nblk = seq // Q_BLOCK

    def win_start(i):
        return jnp.clip(i * Q_ROWS - NA_KH // 2, 0, seq // GRID_W - WIN_ROWS) * GRID_W

    qspec = pl.BlockSpec((None, None, Q_BLOCK, LANES), lambda hp, b, i, *_: (b, hp, i, 0))
    wspec = pl.BlockSpec((None, None, pl.Element(WIN_BLOCK), pl.Element(LANES)),
                         lambda hp, b, i, *_: (b, hp, win_start(i), 0))
    cspec = pl.BlockSpec((None, None, clen, LANES), lambda hp, b, i, *_: (b, hp, 0, 0))
    tspec = pl.BlockSpec((None, 2, N_DR, GRID_W, LANES), lambda hp, b, i, *_: (hp, 0, 0, 0, 0))
    grid_spec = pltpu.PrefetchScalarGridSpec(
        num_scalar_prefetch=2,
        grid=(HEAD_PAIRS, bsz, nblk),
        in_specs=[qspec, wspec, wspec, cspec, cspec, tspec],
        out_specs=qspec,
        scratch_shapes=[pltpu.VMEM((len(pieces), 2 * PAIR_BLOCK, PAIR_WIN), _f32)],
    )
    return pl.pallas_call(
        functools.partial(_attn_kernel, pieces=pieces),
        grid_spec=grid_spec,
        out_shape=jax.ShapeDtypeStruct(q.shape, _bf16),
        compiler_params=_params("arbitrary", "arbitrary", "arbitrary"),
        name="attention",
    )(off_tab, var_tab, q, k, v, kc, vc, bycol)


def _ctx_attn_kernel(q_ref, k_ref, v_ref, o_ref):
    o_ref[...] = _attend([(q_ref[...], [k_ref[...]], [v_ref[...]], [None])])[0].astype(o_ref.dtype)


def _ctx_attention(q, k, v):
    bsz, _, clen, _ = q.shape
    spec = pl.BlockSpec((None, None, clen, LANES), lambda b, hp: (b, hp, 0, 0))
    return pl.pallas_call(
        _ctx_attn_kernel,
        grid=(bsz, HEAD_PAIRS),
        in_specs=[spec, spec, spec],
        out_specs=spec,
        out_shape=jax.ShapeDtypeStruct(q.shape, _bf16),
        compiler_params=_params("arbitrary", "arbitrary"),
        name="ctx_attention",
    )(q, k, v)


def _window_plan(rows):
    patterns, off_tab, var_tab = {}, [], []
    for r in range(0, rows, PAIR_ROWS):
        ws = int(np.clip(r - NA_KH // 2, 0, rows - PAIR_WIN_ROWS))
        r0 = r // Q_ROWS * Q_ROWS
        block_ws = int(np.clip(r0 - NA_KH // 2, 0, rows - WIN_ROWS))
        assert 0 <= ws - block_ws <= WIN_ROWS - PAIR_WIN_ROWS
        piece = []
        for j in range(PAIR_ROWS):
            r_start = int(np.clip(r + j - NA_KH // 2, 0, rows - NA_KH))
            assert ws <= r_start and r_start + NA_KH <= ws + PAIR_WIN_ROWS
            piece.append(tuple(ws + kr - (r + j) + NA_KH - 1 if r_start <= ws + kr < r_start + NA_KH else N_DR - 1
                               for kr in range(PAIR_WIN_ROWS)))
        var_tab.append(patterns.setdefault(tuple(piece), len(patterns)))
        off_tab.append(ws - block_ws)
    return jnp.asarray(off_tab, jnp.int32), jnp.asarray(var_tab, jnp.int32), tuple(patterns)


def _window_col_table(rpb):
    col = np.arange(GRID_W)
    col_start = np.clip(col - NA_KW // 2, 0, GRID_W - NA_KW)
    col_ok = (col[None, :] >= col_start[:, None]) & (col[None, :] < col_start[:, None] + NA_KW)
    dc = np.where(col_ok, col[None, :] - col[:, None] + NA_KW - 1, 0)
    onehot = np.zeros((2 * NA_KW - 1, GRID_W * GRID_W), np.float32)
    onehot[dc.reshape(-1), np.arange(GRID_W * GRID_W)] = col_ok.reshape(-1)
    by_col = jnp.dot(rpb.reshape(-1, 2 * NA_KW - 1), jnp.asarray(onehot), precision=lax.Precision.HIGHEST)
    by_col = by_col.reshape(NA_HEADS, 2 * NA_KH - 1, GRID_W, GRID_W)
    by_col = jnp.where(jnp.asarray(col_ok)[None, None], by_col, MASKED)
    by_col = jnp.concatenate([by_col, jnp.full((NA_HEADS, 1, GRID_W, GRID_W), MASKED, _f32)], axis=1)
    by_col = jnp.concatenate([by_col, by_col], axis=-1)
    return by_col.reshape(HEAD_PAIRS, 2, N_DR, GRID_W, LANES)


def _merge_kernel(x_ref, o_ref, gu_ref, vn_ref, ga_ref, shf_ref, scf_ref, ws_ref, bs_ref, og_ref,
                  wout_ref, wr_ref, br_ref, tri_ref, xn_ref, row_ref, meta_ref, cnt_ref, *, tm):
    first = (pl.program_id(0) == 0) & (pl.program_id(1) == 0)

    @pl.when(first)
    def _():
        cnt_ref[...] = jnp.zeros_like(cnt_ref)

    nchunk = tm // SGU_CHUNK
    vn = vn_ref[...]
    lane = lax.broadcasted_iota(jnp.int32, (SGU_CHUNK, LANES), 1)
    low = lane < SGU_GROUP_DIM
    mixed_cols = []
    for gp in range(SGU_W // LANES):
        rhs_cols = []
        for n in range(nchunk):
            blk = vn[n * SGU_CHUNK:(n + 1) * SGU_CHUNK, gp * LANES:(gp + 1) * LANES]
            zero = jnp.zeros_like(blk)
            rhs_cols.append(jnp.concatenate([jnp.where(low, blk, zero), jnp.where(low, zero, blk)], axis=0))
        rhs = jnp.concatenate(rhs_cols, axis=1)
        mix = _dot(ws_ref[gp], rhs)
        mixed_cols.append(jnp.concatenate([mix[:, n * LANES:(n + 1) * LANES] for n in range(nchunk)], axis=0))
    mixed = jnp.concatenate(mixed_cols, axis=1)
    bias = jnp.concatenate([bs_ref[...]] * nchunk, axis=0)
    o_sg = gu_ref[...].astype(_f32) * (mixed + bias)
    o_na = jnp.concatenate([o_ref[hp].astype(_f32) for hp in range(HEAD_PAIRS)], axis=1)

    def rms(t):
        return t * lax.rsqrt(jnp.mean(t * t, -1, keepdims=True) + EPS)

    y = jnp.concatenate([rms(o_na), rms(o_sg)], axis=1) * og_ref[...]
    xn = x_ref[...] + ga_ref[...] * _dot(y.astype(_bf16), wout_ref[...])
    xn_ref[...] = xn
    hf = rms(xn) * (1.0 + scf_ref[...]) + shf_ref[...]
    row_ref[:, :D_MODEL] = hf

    lg = _dot(hf.astype(_bf16), wr_ref[...]) + br_ref[...]
    lane_i = lax.broadcasted_iota(jnp.int32, lg.shape, 1)
    lane_f = lane_i.astype(_f32)
    far = float(LANES)
    is_g = lane_i < N_GROUPS
    gl = jnp.where(is_g, lg, MASKED)
    gm = jnp.max(gl, -1, keepdims=True)
    p_top = 1.0 / jnp.sum(jnp.where(is_g, jnp.exp(gl - gm), 0.0), -1, keepdims=True)
    g_idx = jnp.min(jnp.where(is_g & (gl == gm), lane_f, far), -1, keepdims=True)
    e_lo = N_GROUPS + EXPERTS_PER_GROUP * g_idx
    is_e = (lane_f >= e_lo) & (lane_f < e_lo + EXPERTS_PER_GROUP)
    el = jnp.where(is_e, lg, MASKED)
    em = jnp.max(el, -1, keepdims=True)
    ee = jnp.where(is_e, jnp.exp(el - em), 0.0)
    pe = jnp.where(is_e, ee / jnp.sum(ee, -1, keepdims=True), -1.0)
    v1 = jnp.max(pe, -1, keepdims=True)
    i1 = jnp.min(jnp.where(pe == v1, lane_f, far), -1, keepdims=True)
    pe2 = jnp.where(lane_f == i1, -1.0, pe)
    v2 = jnp.max(pe2, -1, keepdims=True)
    i2 = jnp.min(jnp.where(pe2 == v2, lane_f, far), -1, keepdims=True)
    den = v1 + v2
    w1 = v1 / den * p_top
    w2 = v2 / den * p_top
    first_low = i1 < i2
    a = jnp.minimum(i1, i2) - e_lo
    b = jnp.maximum(i1, i2) - e_lo
    lex = a * (7.0 - a) * 0.5 + (b - a - 1.0)
    pair = lex
    for src, dst in enumerate(PAIR_OF_LEX):
        if src != dst:
            pair = jnp.where(lex == float(src), float(dst), pair)
    cls = g_idx * float(N_PAIRS) + pair
    gate_a = jnp.where(first_low, w1, w2)
    gate_b = jnp.where(first_low, w2, w1)
    row_ref[:, D_MODEL:] = jnp.where(lane_i == 0, gate_a, jnp.where(lane_i == 1, gate_b, 0.0))

    onehot = lane_f == cls
    before = _dot(tri_ref[...], jnp.where(onehot, 1.0, 0.0).astype(_bf16))
    rank = jnp.sum(jnp.where(onehot, before + cnt_ref[...], 0.0), -1, keepdims=True)
    cnt_ref[...] += jnp.sum(jnp.where(onehot, 1.0, 0.0), 0, keepdims=True)
    meta = jnp.where(lane_i == 0, cls, jnp.where(lane_i == 1, rank, 0.0))
    meta_ref[...] = jnp.transpose(meta)[:SUBLANES, :]


def _merge(x, o_na, gu, vn, g_a, sh_f, sc_f, ws_cat, bs_tab, out_gain, w_out, w_r, b_r, tri, tm):
    bsz, length, _ = x.shape
    mod = pl.BlockSpec((None, 1, D_MODEL), lambda b, i: (b, 0, 0))
    tok = lambda n: pl.BlockSpec((None, tm, n), lambda b, i: (b, i, 0))
    const = lambda *shape: pl.BlockSpec(shape, lambda b, i: (0,) * len(shape))
    return pl.pallas_call(
        functools.partial(_merge_kernel, tm=tm),
        grid=(bsz, length // tm),
        in_specs=[tok(D_MODEL),
                  pl.BlockSpec((None, HEAD_PAIRS, tm, LANES), lambda b, i: (b, 0, i, 0)),
                  tok(SGU_W), tok(SGU_W), mod, mod, mod,
                  const(SGU_W // LANES, SGU_CHUNK, 2 * SGU_CHUNK),
                  const(SGU_CHUNK, SGU_W),
                  const(1, D_MODEL),
                  const(D_MODEL, D_MODEL),
                  const(D_MODEL, LANES),
                  const(1, LANES),
                  const(tm, tm)],
        out_specs=[tok(D_MODEL), tok(ROW_W),
                   pl.BlockSpec((None, SUBLANES, tm), lambda b, i: (b, 0, i)),
                   const(1, LANES)],
        out_shape=[jax.ShapeDtypeStruct((bsz, length, D_MODEL), _f32),
                   jax.ShapeDtypeStruct((bsz, length, ROW_W), _f32),
                   jax.ShapeDtypeStruct((bsz, SUBLANES, length), _f32),
                   jax.ShapeDtypeStruct((1, LANES), _f32)],
        compiler_params=_params("arbitrary", "arbitrary"),
        name="merge",
    )(x, o_na, gu, vn, g_a, sh_f, sc_f, ws_cat, bs_tab, out_gain, w_out, w_r, b_r, tri)


def _issue_row_copies(count, make_copy):
    for j in range(count):
        make_copy(j).start(priority=j % 2)


def _dispatch_kernel(pos_ref, rows_ref, init_ref, xs_ref, sem, *, td):
    del init_ref
    _issue_row_copies(td, lambda j: pltpu.make_async_copy(
        rows_ref.at[pl.ds(j, 1)], xs_ref.at[pl.ds(pos_ref[0, j], 1)], sem))
    pltpu.make_async_copy(rows_ref, xs_ref.at[pl.ds(0, td)], sem).wait()


def _dispatch(rows, pos, padded, td):
    n = rows.shape[0]
    return pl.pallas_call(
        functools.partial(_dispatch_kernel, td=td),
        grid=(n // td,),
        in_specs=[pl.BlockSpec((None, 1, td), lambda i: (i, 0, 0), memory_space=pltpu.SMEM),
                  pl.BlockSpec((td, ROW_W), lambda i: (i, 0)),
                  pl.BlockSpec(memory_space=pl.ANY)],
        out_specs=pl.BlockSpec(memory_space=pl.ANY),
        out_shape=jax.ShapeDtypeStruct((padded, ROW_W), _f32),
        scratch_shapes=[pltpu.SemaphoreType.DMA(())],
        input_output_aliases={2: 0},
        compiler_params=_params("arbitrary"),
        name="dispatch",
    )(pos.reshape(n // td, 1, td), rows, jnp.zeros((padded, ROW_W), _f32))


def _combine_kernel(pos_ref, npos_ref, x_ref, g_ref, ys_ref, o_ref, buf0, buf1, sems, *, tc, nsteps):
    s = pl.program_id(0) * pl.num_programs(1) + pl.program_id(1)
    bufs = (buf0, buf1)

    def request(idx_ref, slot):
        _issue_row_copies(tc, lambda j: pltpu.make_async_copy(
            ys_ref.at[pl.ds(idx_ref[0, j], 1)], bufs[slot].at[pl.ds(j, 1)], sems.at[slot]))

    @pl.when(s == 0)
    def _():
        request(pos_ref, 0)

    for slot in range(2):
        @pl.when((s % 2 == slot) & (s + 1 < nsteps))
        def _():
            request(npos_ref, 1 - slot)

        @pl.when(s % 2 == slot)
        def _():
            pltpu.make_async_copy(ys_ref.at[pl.ds(0, tc)], bufs[slot], sems.at[slot]).wait()
            o_ref[...] = x_ref[...] + g_ref[...] * bufs[slot][...]


def _combine(x, ys, pos, g, tc):
    bsz, length, _ = x.shape
    per_batch = length // tc
    nsteps = bsz * per_batch
    tok = pl.BlockSpec((None, tc, D_MODEL), lambda b, i: (b, i, 0))
    pos3 = pos.reshape(nsteps, 1, tc)
    return pl.pallas_call(
        functools.partial(_combine_kernel, tc=tc, nsteps=nsteps),
        grid=(bsz, per_batch),
        in_specs=[pl.BlockSpec((None, 1, tc), lambda b, i: (b * per_batch + i, 0, 0), memory_space=pltpu.SMEM),
                  pl.BlockSpec((None, 1, tc), lambda b, i: (jnp.minimum(b * per_batch + i + 1, nsteps - 1), 0, 0),
                               memory_space=pltpu.SMEM),
                  tok,
                  pl.BlockSpec((None, 1, D_MODEL), lambda b, i: (b, 0, 0)),
                  pl.BlockSpec(memory_space=pl.ANY)],
        out_specs=tok,
        out_shape=jax.ShapeDtypeStruct(x.shape, _f32),
        scratch_shapes=[pltpu.VMEM((tc, D_MODEL), _f32), pltpu.VMEM((tc, D_MODEL), _f32),
                        pltpu.SemaphoreType.DMA((2,))],
        compiler_params=_params("arbitrary", "arbitrary"),
        name="combine",
    )(pos3, pos3, x, g, ys)


def _experts_kernel(ea_ref, eb_ref, chg_ref, nlive_ref, xs_ref, w1a_ref, w3a_ref, w2a_ref,
                    w1b_ref, w3b_ref, w2b_ref, y_ref, c1a, c3a, c2a, c1b, c3b, c2b):
    t = pl.program_id(0)
    live = t < nlive_ref[0]

    @pl.when(live & ((chg_ref[t] & 1) != 0))
    def _():
        for src, dst in ((w1a_ref, c1a), (w3a_ref, c3a), (w2a_ref, c2a)):
            dst[...] = src[...].astype(_bf16)

    @pl.when(live & ((chg_ref[t] & 2) != 0))
    def _():
        for src, dst in ((w1b_ref, c1b), (w3b_ref, c3b), (w2b_ref, c2b)):
            dst[...] = src[...].astype(_bf16)

    @pl.when(live)
    def _():
        xs = xs_ref[:, :D_MODEL].astype(_bf16)
        gates = xs_ref[:, D_MODEL:]

        def expert(w1_ref, w3_ref, w2_ref):
            hid = jax.nn.silu(_dot(xs, w1_ref[...])) * _dot(xs, w3_ref[...])
            return _dot(hid.astype(_bf16), w2_ref[...])

        y_ref[...] = gates[:, 0:1] * expert(c1a, c3a, c2a) + gates[:, 1:2] * expert(c1b, c3b, c2b)

    @pl.when(jnp.logical_not(live))
    def _():
        y_ref[...] = jnp.zeros_like(y_ref)


def _experts(xs, tile_ea, tile_eb, chg, nlive, w1, w3, w2, layer, tile):
    ntiles = xs.shape[0] // tile
    up_a = pl.BlockSpec((None, None, D_MODEL, D_EXPERT), lambda t, ea, eb, cg, nl: (layer, ea[t], 0, 0))
    dn_a = pl.BlockSpec((None, None, D_EXPERT, D_MODEL), lambda t, ea, eb, cg, nl: (layer, ea[t], 0, 0))
    up_b = pl.BlockSpec((None, None, D_MODEL, D_EXPERT), lambda t, ea, eb, cg, nl: (layer, eb[t], 0, 0))
    dn_b = pl.BlockSpec((None, None, D_EXPERT, D_MODEL), lambda t, ea, eb, cg, nl: (layer, eb[t], 0, 0))
    grid_spec = pltpu.PrefetchScalarGridSpec(
        num_scalar_prefetch=4,
        grid=(ntiles,),
        in_specs=[pl.BlockSpec((tile, ROW_W), lambda t, ea, eb, cg, nl: (t, 0)),
                  up_a, up_a, dn_a, up_b, up_b, dn_b],
        out_specs=pl.BlockSpec((tile, D_MODEL), lambda t, ea, eb, cg, nl: (t, 0)),
        scratch_shapes=[pltpu.VMEM((D_MODEL, D_EXPERT), _bf16), pltpu.VMEM((D_MODEL, D_EXPERT), _bf16),
                        pltpu.VMEM((D_EXPERT, D_MODEL), _bf16)] * 2,
    )
    return pl.pallas_call(
        _experts_kernel,
        grid_spec=grid_spec,
        out_shape=jax.ShapeDtypeStruct((xs.shape[0], D_MODEL), _f32),
        compiler_params=_params("arbitrary"),
        name="experts",
    )(tile_ea, tile_eb, chg, nlive, xs, w1, w3, w2, w1, w3, w2)


def _dispatch_plan(cls, rank, counts, tile):
    n = cls.shape[0]
    padded = -(-(n + N_CLASSES * (tile - 1)) // tile) * tile
    ntiles = padded // tile
    pcounts = (counts + tile - 1) // tile * tile
    pad_end = jnp.cumsum(pcounts)
    pad_off = pad_end - pcounts
    pos = pad_off[cls] + rank
    nlive = pad_end[-1] // tile
    tile_start = jnp.arange(ntiles, dtype=jnp.int32) * tile
    tile_cls = jnp.sum((pad_end[None, :] <= jnp.minimum(tile_start, pad_end[-1] - 1)[:, None]).astype(jnp.int32), axis=1)
    tile_cls = jnp.minimum(tile_cls, N_CLASSES - 1)
    grp = tile_cls // N_PAIRS
    pair = tile_cls % N_PAIRS
    tile_ea = grp * EXPERTS_PER_GROUP + jnp.asarray(PAIR_A, jnp.int32)[pair]
    tile_eb = grp * EXPERTS_PER_GROUP + jnp.asarray(PAIR_B, jnp.int32)[pair]
    differs = lambda e: jnp.concatenate([jnp.ones((1,), jnp.int32), (e[1:] != e[:-1]).astype(jnp.int32)])
    chg = differs(tile_ea) + 2 * differs(tile_eb)
    return pos.astype(jnp.int32), padded, tile_ea, tile_eb, chg, nlive.astype(jnp.int32).reshape(1)


def _ffn(x_new, rows, meta, counts, g_f, w1, w3, w2, layer, tile, tdma):
    bsz, length, _ = x_new.shape
    n = bsz * length
    cls = meta[:, 0, :].reshape(n).astype(jnp.int32)
    rank = meta[:, 1, :].reshape(n).astype(jnp.int32)
    cnt = counts[0, :N_CLASSES].astype(jnp.int32)
    pos, padded, tile_ea, tile_eb, chg, nlive = _dispatch_plan(cls, rank, cnt, tile)
    xs = _dispatch(rows.reshape(n, ROW_W), pos, padded, tdma)
    ys = _experts(xs, tile_ea, tile_eb, chg, nlive, w1, w3, w2, layer, tile)
    return _combine(x_new, ys, pos, g_f, tdma)


def kernel(x, c, ctx, c_ctx, w_ada, b_ada, w_in, q_gain, k_gain, rpb, sgu_ln, sgu_w, sgu_b, out_gain,
           w_out, rg_w, rg_b, re_w, re_b, w1, w3, w2):
    bsz, seq, _ = x.shape
    depth = w_ada.shape[0]
    rows = seq // GRID_W
    assert seq % Q_BLOCK == 0 and rows // Q_ROWS >= 3 and ctx.shape[1] % SGU_CHUNK == 0
    tm_x, tm_c = 512, ctx.shape[1]

    cond = jnp.zeros((SUBLANES, D_MODEL), _f32).at[:bsz].set(c).at[bsz].set(c_ctx)
    mods = _adaln(cond, w_ada, b_ada).reshape(depth, SUBLANES, N_MOD, D_MODEL)

    assert MXU_TILE % NA_HEAD_DIM == 0 and NA_W % MXU_TILE == 0
    head_id = np.arange(MXU_TILE) // NA_HEAD_DIM
    avg = jnp.asarray((head_id[:, None] == head_id[None, :]).astype(np.float32) / NA_HEAD_DIM, _bf16)
    win_plan = _window_plan(rows)
    tri = {tm: jnp.asarray(np.tril(np.ones((tm, tm), np.float32), -1), _bf16) for tm in {tm_x, tm_c}}

    for i in range(depth):
        last = i == depth - 1
        mx = [mods[i, :bsz, j][:, None, :] for j in range(N_MOD)]
        mc = [jnp.broadcast_to(mods[i, bsz, j][None, None, :], (bsz, 1, D_MODEL)) for j in range(N_MOD)]
        w_in_b = w_in[i].astype(_bf16)
        qg = (jnp.tile(q_gain[i], NA_HEADS) * NA_HEAD_DIM ** -0.5)[None, :]
        kg = jnp.tile(k_gain[i], NA_HEADS)[None, :]
        ln_g = sgu_ln[i][None, :]
        ws_cat = jnp.concatenate([sgu_w[i, 0::2], sgu_w[i, 1::2]], axis=2).astype(_bf16)
        bs_tab = jnp.repeat(sgu_b[i].T, SGU_GROUP_DIM, axis=1)
        og = out_gain[i][None, :]
        w_out_b = w_out[i].astype(_bf16)
        w_r = jnp.concatenate([rg_w[i], jnp.transpose(re_w[i], (1, 0, 2)).reshape(D_MODEL, N_EXPERTS),
                               jnp.zeros((D_MODEL, LANES - N_GROUPS - N_EXPERTS), _f32)], axis=1).astype(_bf16)
        b_r = jnp.concatenate([rg_b[i], re_b[i].reshape(-1),
                               jnp.zeros((LANES - N_GROUPS - N_EXPERTS,), _f32)])[None, :]
        bycol = _window_col_table(rpb[i])

        q, k, v, gu, vn = _proj(x, mx[0], mx[1], w_in_b, avg, qg, kg, ln_g, tm_x)
        qc, kc, vc, guc, vnc = _proj(ctx, mc[0], mc[1], w_in_b, avg, qg, kg, ln_g, tm_c)
        o_na = _attention(q, k, v, kc, vc, bycol, win_plan)
        x_new, xrows, meta, counts = _merge(x, o_na, gu, vn, mx[2], mx[3], mx[4], ws_cat, bs_tab, og,
                                            w_out_b, w_r, b_r, tri[tm_x], tm_x)
        x = _ffn(x_new, xrows, meta, counts, mx[5], w1, w3, w2, i, 512, 1024)
        if not last:
            oc_na = _ctx_attention(qc, kc, vc)
            c_new, crows, cmeta, ccounts = _merge(ctx, oc_na, guc, vnc, mc[2], mc[3], mc[4], ws_cat, bs_tab,
                                                  og, w_out_b, w_r, b_r, tri[tm_c], tm_c)
            ctx = _ffn(c_new, crows, cmeta, ccounts, mc[5], w1, w3, w2, i, 64, tm_c)
    return x
```

```python
import functools

import jax
import jax.numpy as jnp
import numpy as np
from jax import lax
from jax.experimental import pallas as pl
from jax.experimental.pallas import tpu as pltpu

D_MODEL = 1024
GRID_W = 64
NA_HEADS = 8
NA_HEAD_DIM = 64
NA_W = NA_HEADS * NA_HEAD_DIM
NA_KH = 8
NA_KW = 16
SGU_GROUPS = 8
SGU_W = D_MODEL - NA_W
SGU_GROUP_DIM = SGU_W // SGU_GROUPS
SGU_CHUNK = 128
IN_COLS = 3 * NA_W + 2 * SGU_W
N_GROUPS = 4
EXPERTS_PER_GROUP = 4
N_EXPERTS = N_GROUPS * EXPERTS_PER_GROUP
D_EXPERT = D_MODEL // 2
N_MOD = 6
EPS = 1e-6

LANES = 128
SUBLANES = 8
HEAD_PAIRS = NA_W // LANES
Q_ROWS = 8
Q_BLOCK = Q_ROWS * GRID_W
WIN_ROWS = Q_ROWS + NA_KH - 1
WIN_BLOCK = WIN_ROWS * GRID_W
PAIR_ROWS = 2
PAIR_BLOCK = PAIR_ROWS * GRID_W
PAIR_WIN_ROWS = PAIR_ROWS + NA_KH - 1
PAIR_WIN = PAIR_WIN_ROWS * GRID_W
PAIRS_IN_FLIGHT = 2
N_DR = 2 * NA_KH
N_PAIRS = 6
N_CLASSES = N_GROUPS * N_PAIRS
PAIR_A = (0, 0, 1, 1, 0, 2)
PAIR_B = (1, 2, 2, 3, 3, 3)
PAIR_OF_LEX = (0, 1, 4, 2, 3, 5)
CLASS_ROWS = 32
ROUTER_ROWS = 48
GATE_PIECES = 3
ROW_W = D_MODEL + LANES
MXU_TILE = 256
MASKED = -1e30
VMEM_LIMIT_BYTES = 56 * 1024 * 1024

_f32 = jnp.float32
_bf16 = jnp.bfloat16


def _params(*semantics):
    return pltpu.CompilerParams(dimension_semantics=semantics, vmem_limit_bytes=VMEM_LIMIT_BYTES)


def _dot(a, b):
    return jnp.dot(a, b, preferred_element_type=_f32)


def _dot_nt(a, b):
    return lax.dot_general(a, b, (((1,), (1,)), ((), ())), preferred_element_type=_f32)


def _adaln_kernel(cond_ref, w_ref, b_ref, out_ref):
    cond = cond_ref[...]
    act = cond * jax.nn.sigmoid(cond)
    out_ref[...] = _dot(act.astype(_bf16), w_ref[...].astype(_bf16)) + b_ref[...]


def _adaln(cond, w_ada, b_ada):
    depth = w_ada.shape[0]
    cols = N_MOD * D_MODEL
    col_block = 1536
    return pl.pallas_call(
        _adaln_kernel,
        grid=(depth, cols // col_block),
        in_specs=[
            pl.BlockSpec((SUBLANES, D_MODEL), lambda d, j: (0, 0)),
            pl.BlockSpec((None, D_MODEL, col_block), lambda d, j: (d, 0, j)),
            pl.BlockSpec((None, 1, col_block), lambda d, j: (d, 0, j)),
        ],
        out_specs=pl.BlockSpec((None, SUBLANES, col_block), lambda d, j: (d, 0, j)),
        out_shape=jax.ShapeDtypeStruct((depth, SUBLANES, cols), _f32),
        compiler_params=_params("arbitrary", "arbitrary"),
        name="adaln",
    )(cond, w_ada, b_ada.reshape(depth, 1, cols))


def _proj_kernel(x_ref, sh_ref, sc_ref, w_ref, avg_ref, qg_ref, kg_ref, ln_ref,
                 q_ref, k_ref, v_ref, gu_ref, vn_ref):
    x = x_ref[...]
    h = x * lax.rsqrt(jnp.mean(x * x, -1, keepdims=True) + EPS) * (1.0 + sc_ref[...]) + sh_ref[...]
    p = _dot(h.astype(_bf16), w_ref[...])
    q = p[:, :NA_W]
    k = p[:, NA_W:2 * NA_W]
    v = p[:, 2 * NA_W:3 * NA_W]
    u = p[:, 3 * NA_W:3 * NA_W + SGU_W]
    vs = p[:, 3 * NA_W + SGU_W:]
    def head_mean(t):
        sq = (t * t).astype(_bf16)
        return jnp.concatenate([_dot(sq[:, c:c + MXU_TILE], avg_ref[...]) for c in range(0, NA_W, MXU_TILE)], axis=1)

    q_ms = head_mean(q)
    k_ms = head_mean(k)
    qn = (q * lax.rsqrt(q_ms + EPS) * qg_ref[...]).astype(_bf16)
    kn = (k * lax.rsqrt(k_ms + EPS) * kg_ref[...]).astype(_bf16)
    vb = v.astype(_bf16)
    for hp in range(HEAD_PAIRS):
        cols = slice(hp * LANES, (hp + 1) * LANES)
        q_ref[hp] = qn[:, cols]
        k_ref[hp] = kn[:, cols]
        v_ref[hp] = vb[:, cols]
    gu_ref[...] = jax.nn.gelu(u).astype(_bf16)
    gv = jax.nn.gelu(vs)
    mu = jnp.mean(gv, -1, keepdims=True)
    cen = gv - mu
    var = jnp.mean(cen * cen, -1, keepdims=True)
    vn_ref[...] = (cen * lax.rsqrt(var + EPS) * ln_ref[...]).astype(_bf16)


def _proj(x, sh, sc, w_in, avg, qg, kg, ln_g, tm):
    bsz, length, _ = x.shape
    vec = lambda n: pl.BlockSpec((1, n), lambda b, i: (0, 0))
    mod = pl.BlockSpec((None, 1, D_MODEL), lambda b, i: (b, 0, 0))
    tok = lambda n: pl.BlockSpec((None, tm, n), lambda b, i: (b, i, 0))
    hpm = pl.BlockSpec((None, HEAD_PAIRS, tm, LANES), lambda b, i: (b, 0, i, 0))
    hpm_shape = jax.ShapeDtypeStruct((bsz, HEAD_PAIRS, length, LANES), _bf16)
    sgu_shape = jax.ShapeDtypeStruct((bsz, length, SGU_W), _bf16)
    return pl.pallas_call(
        _proj_kernel,
        grid=(bsz, length // tm),
        in_specs=[tok(D_MODEL), mod, mod,
                  pl.BlockSpec((D_MODEL, IN_COLS), lambda b, i: (0, 0)),
                  pl.BlockSpec((MXU_TILE, MXU_TILE), lambda b, i: (0, 0)),
                  vec(NA_W), vec(NA_W), vec(SGU_W)],
        out_specs=[hpm, hpm, hpm, tok(SGU_W), tok(SGU_W)],
        out_shape=[hpm_shape, hpm_shape, hpm_shape, sgu_shape, sgu_shape],
        compiler_params=_params("arbitrary", "arbitrary"),
        name="proj",
    )(x, sh, sc, w_in, avg, qg, kg, ln_g)


def _attend(blocks):
    add = lambda a, b: a + b
    m_rows = blocks[0][0].shape[0]
    lane = lax.broadcasted_iota(jnp.int32, blocks[0][0].shape, 1)
    low = lane < NA_HEAD_DIM
    stacked = [jnp.concatenate([jnp.where(low, q, jnp.zeros_like(q)), jnp.where(low, jnp.zeros_like(q), q)], axis=0)
               for q, _, _, _ in blocks]
    scores = [[_dot_nt(qs, key) if bias is None else _dot_nt(qs, key) + bias[...]
               for key, bias in zip(keys, biases)] for qs, (_, keys, _, biases) in zip(stacked, blocks)]
    maxes = [functools.reduce(jnp.maximum, [jnp.max(s, -1, keepdims=True) for s in ss]) for ss in scores]
    probs = [[jnp.exp(s - m) for s in ss] for ss, m in zip(scores, maxes)]
    dens = [functools.reduce(add, [jnp.sum(p, -1, keepdims=True) for p in ps]) for ps in probs]
    accs = [functools.reduce(add, [_dot(p.astype(_bf16), val) for p, val in zip(ps, blk[2])])
            for ps, blk in zip(probs, blocks)]
    outs = [acc / den for acc, den in zip(accs, dens)]
    return [jnp.where(low, o2[:m_rows], o2[m_rows:]) for o2 in outs]


def _attn_kernel(off_ref, var_ref, q_ref, k_ref, v_ref, kc_ref, vc_ref, bycol_ref, o_ref, bias_ref, *, pieces):
    i = pl.program_id(2)

    @pl.when(i == 0)
    def _():
        for var, piece in enumerate(pieces):
            for hh in range(2):
                for j in range(PAIR_ROWS):
                    for kr in range(PAIR_WIN_ROWS):
                        lo = (kr % 2) * GRID_W
                        row = hh * PAIR_BLOCK + j * GRID_W
                        bias_ref[var, row:row + GRID_W, kr * GRID_W:(kr + 1) * GRID_W] = (
                            bycol_ref[hh, piece[j][kr], :, lo:lo + GRID_W])

    kc = kc_ref[...]
    vc = vc_ref[...]
    npair = Q_ROWS // PAIR_ROWS
    for g0 in range(0, npair, PAIRS_IN_FLIGHT):
        blocks = []
        for g in range(g0, g0 + PAIRS_IN_FLIGHT):
            start = pl.multiple_of(off_ref[i * npair + g] * GRID_W, GRID_W)
            var = var_ref[i * npair + g]
            blocks.append((q_ref[g * PAIR_BLOCK:(g + 1) * PAIR_BLOCK, :],
                           [k_ref[pl.ds(start, PAIR_WIN), :], kc],
                           [v_ref[pl.ds(start, PAIR_WIN), :], vc],
                           [bias_ref.at[var], None]))
        for g, out in zip(range(g0, g0 + PAIRS_IN_FLIGHT), _attend(blocks)):
            o_ref[g * PAIR_BLOCK:(g + 1) * PAIR_BLOCK, :] = out.astype(o_ref.dtype)


def _attention(q, k, v, kc, vc, bycol, plan):
    off_tab, var_tab, pieces = plan
    bsz, _, seq, _ = q.shape
    clen = kc.shape[2]
    nblk = seq // Q_BLOCK

    def win_start(i):
        return jnp.clip(i * Q_ROWS - NA_KH // 2, 0, seq // GRID_W - WIN_ROWS) * GRID_W

    qspec = pl.BlockSpec((None, None, Q_BLOCK, LANES), lambda hp, b, i, *_: (b, hp, i, 0))
    wspec = pl.BlockSpec((None, None, pl.Element(WIN_BLOCK), pl.Element(LANES)),
                         lambda hp, b, i, *_: (b, hp, win_start(i), 0))
    cspec = pl.BlockSpec((None, None, clen, LANES), lambda hp, b, i, *_: (b, hp, 0, 0))
    tspec = pl.BlockSpec((None, 2, N_DR, GRID_W, LANES), lambda hp, b, i, *_: (hp, 0, 0, 0, 0))
    grid_spec = pltpu.PrefetchScalarGridSpec(
        num_scalar_prefetch=2,
        grid=(HEAD_PAIRS, bsz, nblk),
        in_specs=[qspec, wspec, wspec, cspec, cspec, tspec],
        out_specs=qspec,
        scratch_shapes=[pltpu.VMEM((len(pieces), 2 * PAIR_BLOCK, PAIR_WIN), _f32)],
    )
    return pl.pallas_call(
        functools.partial(_attn_kernel, pieces=pieces),
        grid_spec=grid_spec,
        out_shape=jax.ShapeDtypeStruct(q.shape, _bf16),
        compiler_params=_params("arbitrary", "arbitrary", "arbitrary"),
        name="attention",
    )(off_tab, var_tab, q, k, v, kc, vc, bycol)


def _ctx_attn_kernel(q_ref, k_ref, v_ref, o_ref):
    o_ref[...] = _attend([(q_ref[...], [k_ref[...]], [v_ref[...]], [None])])[0].astype(o_ref.dtype)


def _ctx_attention(q, k, v):
    bsz, _, clen, _ = q.shape
    spec = pl.BlockSpec((None, None, clen, LANES), lambda b, hp: (b, hp, 0, 0))
    return pl.pallas_call(
        _ctx_attn_kernel,
        grid=(bsz, HEAD_PAIRS),
        in_specs=[spec, spec, spec],
        out_specs=spec,
        out_shape=jax.ShapeDtypeStruct(q.shape, _bf16),
        compiler_params=_params("arbitrary", "arbitrary"),
        name="ctx_attention",
    )(q, k, v)


def _window_plan(rows):
    patterns, off_tab, var_tab = {}, [], []
    for r in range(0, rows, PAIR_ROWS):
        ws = int(np.clip(r - NA_KH // 2, 0, rows - PAIR_WIN_ROWS))
        r0 = r // Q_ROWS * Q_ROWS
        block_ws = int(np.clip(r0 - NA_KH // 2, 0, rows - WIN_ROWS))
        assert 0 <= ws - block_ws <= WIN_ROWS - PAIR_WIN_ROWS
        piece = []
        for j in range(PAIR_ROWS):
            r_start = int(np.clip(r + j - NA_KH // 2, 0, rows - NA_KH))
            assert ws <= r_start and r_start + NA_KH <= ws + PAIR_WIN_ROWS
            piece.append(tuple(ws + kr - (r + j) + NA_KH - 1 if r_start <= ws + kr < r_start + NA_KH else N_DR - 1
                               for kr in range(PAIR_WIN_ROWS)))
        var_tab.append(patterns.setdefault(tuple(piece), len(patterns)))
        off_tab.append(ws - block_ws)
    return jnp.asarray(off_tab, jnp.int32), jnp.asarray(var_tab, jnp.int32), tuple(patterns)


def _window_col_table(rpb):
    col = np.arange(GRID_W)
    col_start = np.clip(col - NA_KW // 2, 0, GRID_W - NA_KW)
    col_ok = (col[None, :] >= col_start[:, None]) & (col[None, :] < col_start[:, None] + NA_KW)
    dc = np.where(col_ok, col[None, :] - col[:, None] + NA_KW - 1, 0)
    onehot = np.zeros((2 * NA_KW - 1, GRID_W * GRID_W), np.float32)
    onehot[dc.reshape(-1), np.arange(GRID_W * GRID_W)] = col_ok.reshape(-1)
    by_col = jnp.dot(rpb.reshape(-1, 2 * NA_KW - 1), jnp.asarray(onehot), precision=lax.Precision.HIGHEST)
    by_col = by_col.reshape(NA_HEADS, 2 * NA_KH - 1, GRID_W, GRID_W)
    by_col = jnp.where(jnp.asarray(col_ok)[None, None], by_col, MASKED)
    by_col = jnp.concatenate([by_col, jnp.full((NA_HEADS, 1, GRID_W, GRID_W), MASKED, _f32)], axis=1)
    by_col = jnp.concatenate([by_col, by_col], axis=-1)
    return by_col.reshape(HEAD_PAIRS, 2, N_DR, GRID_W, LANES)


def _merge_kernel(x_ref, o_ref, gu_ref, vn_ref, ga_ref, shf_ref, scf_ref, ws_ref, bs_ref, og_ref,
                  wout_ref, wr_ref, br_ref, tri_ref, sel_ref, xn_ref, row_ref, meta_ref, cnt_ref, *, tm):
    first = (pl.program_id(0) == 0) & (pl.program_id(1) == 0)

    @pl.when(first)
    def _():
        cnt_ref[...] = jnp.zeros_like(cnt_ref)

    nchunk = tm // SGU_CHUNK
    vn = vn_ref[...]
    lane = lax.broadcasted_iota(jnp.int32, (SGU_CHUNK, LANES), 1)
    low = lane < SGU_GROUP_DIM
    mixed_cols = []
    for gp in range(SGU_W // LANES):
        rhs_cols = []
        for n in range(nchunk):
            blk = vn[n * SGU_CHUNK:(n + 1) * SGU_CHUNK, gp * LANES:(gp + 1) * LANES]
            zero = jnp.zeros_like(blk)
            rhs_cols.append(jnp.concatenate([jnp.where(low, blk, zero), jnp.where(low, zero, blk)], axis=0))
        rhs = jnp.concatenate(rhs_cols, axis=1)
        mix = _dot(ws_ref[gp], rhs)
        mixed_cols.append(jnp.concatenate([mix[:, n * LANES:(n + 1) * LANES] for n in range(nchunk)], axis=0))
    mixed = jnp.concatenate(mixed_cols, axis=1)
    bias = jnp.concatenate([bs_ref[...]] * nchunk, axis=0)
    o_sg = gu_ref[...].astype(_f32) * (mixed + bias)
    o_na = jnp.concatenate([o_ref[hp].astype(_f32) for hp in range(HEAD_PAIRS)], axis=1)

    def rms(t):
        return t * lax.rsqrt(jnp.mean(t * t, -1, keepdims=True) + EPS)

    y = jnp.concatenate([rms(o_na), rms(o_sg)], axis=1) * og_ref[...]
    xn = x_ref[...] + ga_ref[...] * _dot(y.astype(_bf16), wout_ref[...])
    xn_ref[...] = xn
    hf = rms(xn) * (1.0 + scf_ref[...]) + shf_ref[...]
    row_ref[:, :D_MODEL] = hf

    lg = _dot_nt(wr_ref[...], hf.astype(_bf16)) + br_ref[...]
    quad = lambda r: lg[r:r + EXPERTS_PER_GROUP, :]
    row4 = lax.broadcasted_iota(jnp.int32, (EXPERTS_PER_GROUP, tm), 0).astype(_f32)
    over = lambda fn, t: fn(t, 0, keepdims=True)
    gl = quad(0)
    gm = over(jnp.max, gl)
    p_top = 1.0 / over(jnp.sum, jnp.exp(gl - gm))
    g_idx = over(jnp.min, jnp.where(gl == gm, row4, float(N_GROUPS)))
    el = quad(SUBLANES * N_GROUPS)
    for g in range(N_GROUPS - 2, -1, -1):
        el = jnp.where(g_idx == float(g), quad(SUBLANES * (g + 1)), el)
    ee = jnp.exp(el - over(jnp.max, el))
    pe = ee / over(jnp.sum, ee)
    v1 = over(jnp.max, pe)
    i1 = over(jnp.min, jnp.where(pe == v1, row4, float(EXPERTS_PER_GROUP)))
    pe2 = jnp.where(row4 == i1, -1.0, pe)
    v2 = over(jnp.max, pe2)
    i2 = over(jnp.min, jnp.where(pe2 == v2, row4, float(EXPERTS_PER_GROUP)))
    den = v1 + v2
    w1 = v1 / den * p_top
    w2 = v2 / den * p_top
    first_low = i1 < i2
    a = jnp.minimum(i1, i2)
    b = jnp.maximum(i1, i2)
    lex = a * (7.0 - a) * 0.5 + (b - a - 1.0)
    pair = lex
    for src, dst in enumerate(PAIR_OF_LEX):
        if src != dst:
            pair = jnp.where(lex == float(src), float(dst), pair)
    cls = g_idx * float(N_PAIRS) + pair
    gate_a = jnp.where(first_low, w1, w2)
    gate_b = jnp.where(first_low, w2, w1)

    row_p = lax.broadcasted_iota(jnp.int32, (2 * SUBLANES, tm), 0)
    stacked = jnp.zeros((2 * SUBLANES, tm), _f32)
    for n, gate in enumerate((gate_a, gate_b)):
        rest = gate
        for k in range(GATE_PIECES):
            piece = rest.astype(_bf16).astype(_f32)
            stacked = jnp.where(row_p == n * GATE_PIECES + k, piece, stacked)
            rest = rest - piece
    row_ref[:, D_MODEL:] = lax.dot_general(stacked.astype(_bf16), sel_ref[...],
                                           (((0,), (0,)), ((), ())), preferred_element_type=_f32)

    row_c = lax.broadcasted_iota(jnp.int32, (CLASS_ROWS, tm), 0).astype(_f32)
    onehot = row_c == cls
    ones = jnp.where(onehot, 1.0, 0.0)
    before = _dot(ones.astype(_bf16), tri_ref[...])
    rank = over(jnp.sum, jnp.where(onehot, before + cnt_ref[:, 0:1], 0.0))
    cnt_ref[...] += jnp.sum(ones, 1, keepdims=True)
    meta_ref[...] = jnp.concatenate([cls, rank, jnp.zeros((SUBLANES - 2, tm), _f32)], axis=0)


def _merge(x, o_na, gu, vn, g_a, sh_f, sc_f, ws_cat, bs_tab, out_gain, w_out, w_r, b_r, tri, sel, tm):
    bsz, length, _ = x.shape
    mod = pl.BlockSpec((None, 1, D_MODEL), lambda b, i: (b, 0, 0))
    tok = lambda n: pl.BlockSpec((None, tm, n), lambda b, i: (b, i, 0))
    const = lambda *shape: pl.BlockSpec(shape, lambda b, i: (0,) * len(shape))
    return pl.pallas_call(
        functools.partial(_merge_kernel, tm=tm),
        grid=(bsz, length // tm),
        in_specs=[tok(D_MODEL),
                  pl.BlockSpec((None, HEAD_PAIRS, tm, LANES), lambda b, i: (b, 0, i, 0)),
                  tok(SGU_W), tok(SGU_W), mod, mod, mod,
                  const(SGU_W // LANES, SGU_CHUNK, 2 * SGU_CHUNK),
                  const(SGU_CHUNK, SGU_W),
                  const(1, D_MODEL),
                  const(D_MODEL, D_MODEL),
                  const(ROUTER_ROWS, D_MODEL),
                  const(ROUTER_ROWS, 1),
                  const(tm, tm),
                  const(2 * SUBLANES, LANES)],
        out_specs=[tok(D_MODEL), tok(ROW_W),
                   pl.BlockSpec((None, SUBLANES, tm), lambda b, i: (b, 0, i)),
                   const(CLASS_ROWS, LANES)],
        out_shape=[jax.ShapeDtypeStruct((bsz, length, D_MODEL), _f32),
                   jax.ShapeDtypeStruct((bsz, length, ROW_W), _f32),
                   jax.ShapeDtypeStruct((bsz, SUBLANES, length), _f32),
                   jax.ShapeDtypeStruct((CLASS_ROWS, LANES), _f32)],
        compiler_params=_params("arbitrary", "arbitrary"),
        name="merge",
    )(x, o_na, gu, vn, g_a, sh_f, sc_f, ws_cat, bs_tab, out_gain, w_out, w_r, b_r, tri, sel)


def _issue_row_copies(count, make_copy):
    for j in range(count):
        make_copy(j).start(priority=j % 2)


def _dispatch_kernel(pos_ref, rows_ref, init_ref, xs_ref, sem, *, td):
    del init_ref
    _issue_row_copies(td, lambda j: pltpu.make_async_copy(
        rows_ref.at[pl.ds(j, 1)], xs_ref.at[pl.ds(pos_ref[0, j], 1)], sem))
    pltpu.make_async_copy(rows_ref, xs_ref.at[pl.ds(0, td)], sem).wait()


def _dispatch(rows, pos, padded, td):
    n = rows.shape[0]
    return pl.pallas_call(
        functools.partial(_dispatch_kernel, td=td),
        grid=(n // td,),
        in_specs=[pl.BlockSpec((None, 1, td), lambda i: (i, 0, 0), memory_space=pltpu.SMEM),
                  pl.BlockSpec((td, ROW_W), lambda i: (i, 0)),
                  pl.BlockSpec(memory_space=pl.ANY)],
        out_specs=pl.BlockSpec(memory_space=pl.ANY),
        out_shape=jax.ShapeDtypeStruct((padded, ROW_W), _f32),
        scratch_shapes=[pltpu.SemaphoreType.DMA(())],
        input_output_aliases={2: 0},
        compiler_params=_params("arbitrary"),
        name="dispatch",
    )(pos.reshape(n // td, 1, td), rows, jnp.zeros((padded, ROW_W), _f32))


def _combine_kernel(pos_ref, npos_ref, x_ref, g_ref, ys_ref, o_ref, buf0, buf1, sems, *, tc, nsteps):
    s = pl.program_id(0) * pl.num_programs(1) + pl.program_id(1)
    bufs = (buf0, buf1)

    def request(idx_ref, slot):
        _issue_row_copies(tc, lambda j: pltpu.make_async_copy(
            ys_ref.at[pl.ds(idx_ref[0, j], 1)], bufs[slot].at[pl.ds(j, 1)], sems.at[slot]))

    @pl.when(s == 0)
    def _():
        request(pos_ref, 0)

    for slot in range(2):
        @pl.when((s % 2 == slot) & (s + 1 < nsteps))
        def _():
            request(npos_ref, 1 - slot)

        @pl.when(s % 2 == slot)
        def _():
            pltpu.make_async_copy(ys_ref.at[pl.ds(0, tc)], bufs[slot], sems.at[slot]).wait()
            o_ref[...] = x_ref[...] + g_ref[...] * bufs[slot][...]


def _combine(x, ys, pos, g, tc):
    bsz, length, _ = x.shape
    per_batch = length // tc
    nsteps = bsz * per_batch
    tok = pl.BlockSpec((None, tc, D_MODEL), lambda b, i: (b, i, 0))
    pos3 = pos.reshape(nsteps, 1, tc)
    return pl.pallas_call(
        functools.partial(_combine_kernel, tc=tc, nsteps=nsteps),
        grid=(bsz, per_batch),
        in_specs=[pl.BlockSpec((None, 1, tc), lambda b, i: (b * per_batch + i, 0, 0), memory_space=pltpu.SMEM),
                  pl.BlockSpec((None, 1, tc), lambda b, i: (jnp.minimum(b * per_batch + i + 1, nsteps - 1), 0, 0),
                               memory_space=pltpu.SMEM),
                  tok,
                  pl.BlockSpec((None, 1, D_MODEL), lambda b, i: (b, 0, 0)),
                  pl.BlockSpec(memory_space=pl.ANY)],
        out_specs=tok,
        out_shape=jax.ShapeDtypeStruct(x.shape, _f32),
        scratch_shapes=[pltpu.VMEM((tc, D_MODEL), _f32), pltpu.VMEM((tc, D_MODEL), _f32),
                        pltpu.SemaphoreType.DMA((2,))],
        compiler_params=_params("arbitrary", "arbitrary"),
        name="combine",
    )(pos3, pos3, x, g, ys)


def _experts_kernel(ea_ref, eb_ref, chg_ref, nlive_ref, xs_ref, w1a_ref, w3a_ref, w2a_ref,
                    w1b_ref, w3b_ref, w2b_ref, y_ref, c1a, c3a, c2a, c1b, c3b, c2b):
    t = pl.program_id(0)
    live = t < nlive_ref[0]

    @pl.when(live & ((chg_ref[t] & 1) != 0))
    def _():
        for src, dst in ((w1a_ref, c1a), (w3a_ref, c3a), (w2a_ref, c2a)):
            dst[...] = src[...].astype(_bf16)

    @pl.when(live & ((chg_ref[t] & 2) != 0))
    def _():
        for src, dst in ((w1b_ref, c1b), (w3b_ref, c3b), (w2b_ref, c2b)):
            dst[...] = src[...].astype(_bf16)

    @pl.when(live)
    def _():
        xs = xs_ref[:, :D_MODEL].astype(_bf16)
        gates = xs_ref[:, D_MODEL:]

        def expert(w1_ref, w3_ref, w2_ref):
            hid = jax.nn.silu(_dot(xs, w1_ref[...])) * _dot(xs, w3_ref[...])
            return _dot(hid.astype(_bf16), w2_ref[...])

        y_ref[...] = gates[:, 0:1] * expert(c1a, c3a, c2a) + gates[:, 1:2] * expert(c1b, c3b, c2b)

    @pl.when(jnp.logical_not(live))
    def _():
        y_ref[...] = jnp.zeros_like(y_ref)


def _experts(xs, tile_ea, tile_eb, chg, nlive, w1, w3, w2, layer, tile):
    ntiles = xs.shape[0] // tile
    up_a = pl.BlockSpec((None, None, D_MODEL, D_EXPERT), lambda t, ea, eb, cg, nl: (layer, ea[t], 0, 0))
    dn_a = pl.BlockSpec((None, None, D_EXPERT, D_MODEL), lambda t, ea, eb, cg, nl: (layer, ea[t], 0, 0))
    up_b = pl.BlockSpec((None, None, D_MODEL, D_EXPERT), lambda t, ea, eb, cg, nl: (layer, eb[t], 0, 0))
    dn_b = pl.BlockSpec((None, None, D_EXPERT, D_MODEL), lambda t, ea, eb, cg, nl: (layer, eb[t], 0, 0))
    grid_spec = pltpu.PrefetchScalarGridSpec(
        num_scalar_prefetch=4,
        grid=(ntiles,),
        in_specs=[pl.BlockSpec((tile, ROW_W), lambda t, ea, eb, cg, nl: (t, 0)),
                  up_a, up_a, dn_a, up_b, up_b, dn_b],
        out_specs=pl.BlockSpec((tile, D_MODEL), lambda t, ea, eb, cg, nl: (t, 0)),
        scratch_shapes=[pltpu.VMEM((D_MODEL, D_EXPERT), _bf16), pltpu.VMEM((D_MODEL, D_EXPERT), _bf16),
                        pltpu.VMEM((D_EXPERT, D_MODEL), _bf16)] * 2,
    )
    return pl.pallas_call(
        _experts_kernel,
        grid_spec=grid_spec,
        out_shape=jax.ShapeDtypeStruct((xs.shape[0], D_MODEL), _f32),
        compiler_params=_params("arbitrary"),
        name="experts",
    )(tile_ea, tile_eb, chg, nlive, xs, w1, w3, w2, w1, w3, w2)


def _dispatch_plan(cls, rank, counts, tile):
    n = cls.shape[0]
    padded = -(-(n + N_CLASSES * (tile - 1)) // tile) * tile
    ntiles = padded // tile
    pcounts = (counts + tile - 1) // tile * tile
    pad_end = jnp.cumsum(pcounts)
    pad_off = pad_end - pcounts
    pos = pad_off[cls] + rank
    nlive = pad_end[-1] // tile
    tile_start = jnp.arange(ntiles, dtype=jnp.int32) * tile
    tile_cls = jnp.sum((pad_end[None, :] <= jnp.minimum(tile_start, pad_end[-1] - 1)[:, None]).astype(jnp.int32), axis=1)
    tile_cls = jnp.minimum(tile_cls, N_CLASSES - 1)
    grp = tile_cls // N_PAIRS
    pair = tile_cls % N_PAIRS
    tile_ea = grp * EXPERTS_PER_GROUP + jnp.asarray(PAIR_A, jnp.int32)[pair]
    tile_eb = grp * EXPERTS_PER_GROUP + jnp.asarray(PAIR_B, jnp.int32)[pair]
    differs = lambda e: jnp.concatenate([jnp.ones((1,), jnp.int32), (e[1:] != e[:-1]).astype(jnp.int32)])
    chg = differs(tile_ea) + 2 * differs(tile_eb)
    return pos.astype(jnp.int32), padded, tile_ea, tile_eb, chg, nlive.astype(jnp.int32).reshape(1)


def _ffn(x_new, rows, meta, counts, g_f, w1, w3, w2, layer, tile, tdma):
    bsz, length, _ = x_new.shape
    n = bsz * length
    cls = meta[:, 0, :].reshape(n).astype(jnp.int32)
    rank = meta[:, 1, :].reshape(n).astype(jnp.int32)
    cnt = counts[:N_CLASSES, 0].astype(jnp.int32)
    pos, padded, tile_ea, tile_eb, chg, nlive = _dispatch_plan(cls, rank, cnt, tile)
    xs = _dispatch(rows.reshape(n, ROW_W), pos, padded, tdma)
    ys = _experts(xs, tile_ea, tile_eb, chg, nlive, w1, w3, w2, layer, tile)
    return _combine(x_new, ys, pos, g_f, tdma)


def kernel(x, c, ctx, c_ctx, w_ada, b_ada, w_in, q_gain, k_gain, rpb, sgu_ln, sgu_w, sgu_b, out_gain,
           w_out, rg_w, rg_b, re_w, re_b, w1, w3, w2):
    bsz, seq, _ = x.shape
    depth = w_ada.shape[0]
    rows = seq // GRID_W
    assert seq % Q_BLOCK == 0 and rows // Q_ROWS >= 3 and ctx.shape[1] % SGU_CHUNK == 0
    tm_x, tm_c = 512, ctx.shape[1]

    cond = jnp.zeros((SUBLANES, D_MODEL), _f32).at[:bsz].set(c).at[bsz].set(c_ctx)
    mods = _adaln(cond, w_ada, b_ada).reshape(depth, SUBLANES, N_MOD, D_MODEL)

    assert MXU_TILE % NA_HEAD_DIM == 0 and NA_W % MXU_TILE == 0
    head_id = np.arange(MXU_TILE) // NA_HEAD_DIM
    avg = jnp.asarray((head_id[:, None] == head_id[None, :]).astype(np.float32) / NA_HEAD_DIM, _bf16)
    win_plan = _window_plan(rows)
    tri = {tm: jnp.asarray(np.triu(np.ones((tm, tm), np.float32), 1), _bf16) for tm in {tm_x, tm_c}}
    sel_np = np.zeros((2 * SUBLANES, LANES), np.float32)
    for n in range(2):
        sel_np[n * GATE_PIECES:(n + 1) * GATE_PIECES, n] = 1.0
    sel = jnp.asarray(sel_np, _bf16)

    for i in range(depth):
        last = i == depth - 1
        mx = [mods[i, :bsz, j][:, None, :] for j in range(N_MOD)]
        mc = [jnp.broadcast_to(mods[i, bsz, j][None, None, :], (bsz, 1, D_MODEL)) for j in range(N_MOD)]
        w_in_b = w_in[i].astype(_bf16)
        qg = (jnp.tile(q_gain[i], NA_HEADS) * NA_HEAD_DIM ** -0.5)[None, :]
        kg = jnp.tile(k_gain[i], NA_HEADS)[None, :]
        ln_g = sgu_ln[i][None, :]
        ws_cat = jnp.concatenate([sgu_w[i, 0::2], sgu_w[i, 1::2]], axis=2).astype(_bf16)
        bs_tab = jnp.repeat(sgu_b[i].T, SGU_GROUP_DIM, axis=1)
        og = out_gain[i][None, :]
        w_out_b = w_out[i].astype(_bf16)
        pad_w = jnp.zeros((SUBLANES - EXPERTS_PER_GROUP, D_MODEL), _f32)
        pad_b = jnp.zeros((SUBLANES - EXPERTS_PER_GROUP,), _f32)
        w_r = jnp.concatenate([rg_w[i].T, pad_w] + [t for g in range(N_GROUPS) for t in (re_w[i, g].T, pad_w)]
                              + [jnp.zeros((ROUTER_ROWS - SUBLANES * (N_GROUPS + 1), D_MODEL), _f32)]).astype(_bf16)
        b_r = jnp.concatenate([rg_b[i], pad_b] + [t for g in range(N_GROUPS) for t in (re_b[i, g], pad_b)]
                              + [jnp.zeros((ROUTER_ROWS - SUBLANES * (N_GROUPS + 1),), _f32)])[:, None]
        bycol = _window_col_table(rpb[i])

        q, k, v, gu, vn = _proj(x, mx[0], mx[1], w_in_b, avg, qg, kg, ln_g, tm_x)
        qc, kc, vc, guc, vnc = _proj(ctx, mc[0], mc[1], w_in_b, avg, qg, kg, ln_g, tm_c)
        o_na = _attention(q, k, v, kc, vc, bycol, win_plan)
        x_new, xrows, meta, counts = _merge(x, o_na, gu, vn, mx[2], mx[3], mx[4], ws_cat, bs_tab, og,
                                            w_out_b, w_r, b_r, tri[tm_x], sel, tm_x)
        x = _ffn(x_new, xrows, meta, counts, mx[5], w1, w3, w2, i, 512, 1024)
        if not last:
            oc_na = _ctx_attention(qc, kc, vc)
            c_new, crows, cmeta, ccounts = _merge(ctx, oc_na, guc, vnc, mc[2], mc[3], mc[4], ws_cat, bs_tab,
                                                  og, w_out_b, w_r, b_r, tri[tm_c], sel, tm_c)
            ctx = _ffn(c_new, crows, cmeta, ccounts, mc[5], w1, w3, w2, i, 64, tm_c)
    return x
```

```python
import functools

import jax
import jax.numpy as jnp
import numpy as np
from jax import lax
from jax.experimental import pallas as pl
from jax.experimental.pallas import tpu as pltpu

D_MODEL = 1024
GRID_W = 64
NA_HEADS = 8
NA_HEAD_DIM = 64
NA_W = NA_HEADS * NA_HEAD_DIM
NA_KH = 8
NA_KW = 16
SGU_GROUPS = 8
SGU_W = D_MODEL - NA_W
SGU_GROUP_DIM = SGU_W // SGU_GROUPS
SGU_CHUNK = 128
IN_COLS = 3 * NA_W + 2 * SGU_W
N_GROUPS = 4
EXPERTS_PER_GROUP = 4
N_EXPERTS = N_GROUPS * EXPERTS_PER_GROUP
D_EXPERT = D_MODEL // 2
N_MOD = 6
EPS = 1e-6

LANES = 128
SUBLANES = 8
HEAD_PAIRS = NA_W // LANES
Q_ROWS = 8
Q_BLOCK = Q_ROWS * GRID_W
WIN_ROWS = Q_ROWS + NA_KH - 1
WIN_BLOCK = WIN_ROWS * GRID_W
PAIR_ROWS = 2
PAIR_BLOCK = PAIR_ROWS * GRID_W
PAIR_WIN_ROWS = PAIR_ROWS + NA_KH - 1
PAIR_WIN = PAIR_WIN_ROWS * GRID_W
PAIRS_IN_FLIGHT = 2
N_DR = 2 * NA_KH
N_PAIRS = 6
N_CLASSES = N_GROUPS * N_PAIRS
PAIR_A = (0, 0, 1, 1, 0, 2)
PAIR_B = (1, 2, 2, 3, 3, 3)
PAIR_OF_LEX = (0, 1, 4, 2, 3, 5)
CLASS_ROWS = 32
ROUTER_ROWS = 48
GATE_PIECES = 3
ROW_W = D_MODEL + LANES
MXU_TILE = 256
MASKED = -1e30
VMEM_LIMIT_BYTES = 56 * 1024 * 1024

_f32 = jnp.float32
_bf16 = jnp.bfloat16


def _params(*semantics):
    return pltpu.CompilerParams(dimension_semantics=semantics, vmem_limit_bytes=VMEM_LIMIT_BYTES)


def _dot(a, b):
    return jnp.dot(a, b, preferred_element_type=_f32)


def _dot_nt(a, b):
    return lax.dot_general(a, b, (((1,), (1,)), ((), ())), preferred_element_type=_f32)


def _adaln_kernel(cond_ref, w_ref, b_ref, out_ref):
    cond = cond_ref[...]
    act = cond * jax.nn.sigmoid(cond)
    out_ref[...] = _dot(act.astype(_bf16), w_ref[...].astype(_bf16)) + b_ref[...]


def _adaln(cond, w_ada, b_ada):
    depth = w_ada.shape[0]
    cols = N_MOD * D_MODEL
    col_block = 1536
    return pl.pallas_call(
        _adaln_kernel,
        grid=(depth, cols // col_block),
        in_specs=[
            pl.BlockSpec((SUBLANES, D_MODEL), lambda d, j: (0, 0)),
            pl.BlockSpec((None, D_MODEL, col_block), lambda d, j: (d, 0, j)),
            pl.BlockSpec((None, 1, col_block), lambda d, j: (d, 0, j)),
        ],
        out_specs=pl.BlockSpec((None, SUBLANES, col_block), lambda d, j: (d, 0, j)),
        out_shape=jax.ShapeDtypeStruct((depth, SUBLANES, cols), _f32),
        compiler_params=_params("arbitrary", "arbitrary"),
        name="adaln",
    )(cond, w_ada, b_ada.reshape(depth, 1, cols))


def _proj_kernel(x_ref, sh_ref, sc_ref, w_ref, avg_ref, qg_ref, kg_ref, ln_ref,
                 q_ref, k_ref, v_ref, gu_ref, vn_ref):
    x = x_ref[...]
    h = x * lax.rsqrt(jnp.mean(x * x, -1, keepdims=True) + EPS) * (1.0 + sc_ref[...]) + sh_ref[...]
    p = _dot(h.astype(_bf16), w_ref[...])
    q = p[:, :NA_W]
    k = p[:, NA_W:2 * NA_W]
    v = p[:, 2 * NA_W:3 * NA_W]
    u = p[:, 3 * NA_W:3 * NA_W + SGU_W]
    vs = p[:, 3 * NA_W + SGU_W:]
    def head_mean(t):
        sq = (t * t).astype(_bf16)
        return jnp.concatenate([_dot(sq[:, c:c + MXU_TILE], avg_ref[...]) for c in range(0, NA_W, MXU_TILE)], axis=1)

    q_ms = head_mean(q)
    k_ms = head_mean(k)
    qn = (q * lax.rsqrt(q_ms + EPS) * qg_ref[...]).astype(_bf16)
    kn = (k * lax.rsqrt(k_ms + EPS) * kg_ref[...]).astype(_bf16)
    vb = v.astype(_bf16)
    for hp in range(HEAD_PAIRS):
        cols = slice(hp * LANES, (hp + 1) * LANES)
        q_ref[hp] = qn[:, cols]
        k_ref[hp] = kn[:, cols]
        v_ref[hp] = vb[:, cols]
    gu_ref[...] = jax.nn.gelu(u).astype(_bf16)
    gv = jax.nn.gelu(vs)
    mu = jnp.mean(gv, -1, keepdims=True)
    cen = gv - mu
    var = jnp.mean(cen * cen, -1, keepdims=True)
    vn_ref[...] = (cen * lax.rsqrt(var + EPS) * ln_ref[...]).astype(_bf16)


def _proj(x, sh, sc, w_in, avg, qg, kg, ln_g, tm):
    bsz, length, _ = x.shape
    vec = lambda n: pl.BlockSpec((1, n), lambda b, i: (0, 0))
    mod = pl.BlockSpec((None, 1, D_MODEL), lambda b, i: (b, 0, 0))
    tok = lambda n: pl.BlockSpec((None, tm, n), lambda b, i: (b, i, 0))
    hpm = pl.BlockSpec((None, HEAD_PAIRS, tm, LANES), lambda b, i: (b, 0, i, 0))
    hpm_shape = jax.ShapeDtypeStruct((bsz, HEAD_PAIRS, length, LANES), _bf16)
    sgu_shape = jax.ShapeDtypeStruct((bsz, length, SGU_W), _bf16)
    return pl.pallas_call(
        _proj_kernel,
        grid=(bsz, length // tm),
        in_specs=[tok(D_MODEL), mod, mod,
                  pl.BlockSpec((D_MODEL, IN_COLS), lambda b, i: (0, 0)),
                  pl.BlockSpec((MXU_TILE, MXU_TILE), lambda b, i: (0, 0)),
                  vec(NA_W), vec(NA_W), vec(SGU_W)],
        out_specs=[hpm, hpm, hpm, tok(SGU_W), tok(SGU_W)],
        out_shape=[hpm_shape, hpm_shape, hpm_shape, sgu_shape, sgu_shape],
        compiler_params=_params("arbitrary", "arbitrary"),
        name="proj",
    )(x, sh, sc, w_in, avg, qg, kg, ln_g)


def _attend(blocks):
    add = lambda a, b: a + b
    m_rows = blocks[0][0].shape[0]
    lane = lax.broadcasted_iota(jnp.int32, blocks[0][0].shape, 1)
    low = lane < NA_HEAD_DIM
    stacked = [jnp.concatenate([jnp.where(low, q, jnp.zeros_like(q)), jnp.where(low, jnp.zeros_like(q), q)], axis=0)
               for q, _, _, _ in blocks]
    scores = [[_dot_nt(qs, key) if bias is None else _dot_nt(qs, key) + bias[...]
               for key, bias in zip(keys, biases)] for qs, (_, keys, _, biases) in zip(stacked, blocks)]
    maxes = [functools.reduce(jnp.maximum, [jnp.max(s, -1, keepdims=True) for s in ss]) for ss in scores]
    probs = [[jnp.exp(s - m) for s in ss] for ss, m in zip(scores, maxes)]
    dens = [functools.reduce(add, [jnp.sum(p, -1, keepdims=True) for p in ps]) for ps in probs]
    accs = [functools.reduce(add, [_dot(p.astype(_bf16), val) for p, val in zip(ps, blk[2])])
            for ps, blk in zip(probs, blocks)]
    outs = [acc / den for acc, den in zip(accs, dens)]
    return [jnp.where(low, o2[:m_rows], o2[m_rows:]) for o2 in outs]


def _attn_kernel(off_ref, var_ref, q_ref, k_ref, v_ref, kc_ref, vc_ref, bycol_ref, o_ref, bias_ref, *, pieces):
    i = pl.program_id(2)

    @pl.when(i == 0)
    def _():
        for var, piece in enumerate(pieces):
            for hh in range(2):
                for j in range(PAIR_ROWS):
                    for kr in range(PAIR_WIN_ROWS):
                        lo = (kr % 2) * GRID_W
                        row = hh * PAIR_BLOCK + j * GRID_W
                        bias_ref[var, row:row + GRID_W, kr * GRID_W:(kr + 1) * GRID_W] = (
                            bycol_ref[hh, piece[j][kr], :, lo:lo + GRID_W])

    kc = kc_ref[...]
    vc = vc_ref[...]
    npair = Q_ROWS // PAIR_ROWS
    for g0 in range(0, npair, PAIRS_IN_FLIGHT):
        blocks = []
        for g in range(g0, g0 + PAIRS_IN_FLIGHT):
            start = pl.multiple_of(off_ref[i * npair + g] * GRID_W, GRID_W)
            var = var_ref[i * npair + g]
            blocks.append((q_ref[g * PAIR_BLOCK:(g + 1) * PAIR_BLOCK, :],
                           [k_ref[pl.ds(start, PAIR_WIN), :], kc],
                           [v_ref[pl.ds(start, PAIR_WIN), :], vc],
                           [bias_ref.at[var], None]))
        for g, out in zip(range(g0, g0 + PAIRS_IN_FLIGHT), _attend(blocks)):
            o_ref[g * PAIR_BLOCK:(g + 1) * PAIR_BLOCK, :] = out.astype(o_ref.dtype)


def _attention(q, k, v, kc, vc, bycol, plan):
    off_tab, var_tab, pieces = plan
    bsz, _, seq, _ = q.shape
    clen = kc.shape[2]
    nblk = seq // Q_BLOCK

    def win_start(i):
        return jnp.clip(i * Q_ROWS - NA_KH // 2, 0, seq // GRID_W - WIN_ROWS) * GRID_W

    qspec = pl.BlockSpec((None, None, Q_BLOCK, LANES), lambda hp, b, i, *_: (b, hp, i, 0))
    wspec = pl.BlockSpec((None, None, pl.Element(WIN_BLOCK), pl.Element(LANES)),
                         lambda hp, b, i, *_: (b, hp, win_start(i), 0))
    cspec = pl.BlockSpec((None, None, clen, LANES), lambda hp, b, i, *_: (b, hp, 0, 0))
    tspec = pl.BlockSpec((None, 2, N_DR, GRID_W, LANES), lambda hp, b, i, *_: (hp, 0, 0, 0, 0))
    grid_spec = pltpu.PrefetchScalarGridSpec(
        num_scalar_prefetch=2,
        grid=(HEAD_PAIRS, bsz, nblk),
        in_specs=[qspec, wspec, wspec, cspec, cspec, tspec],
        out_specs=qspec,
        scratch_shapes=[pltpu.VMEM((len(pieces), 2 * PAIR_BLOCK, PAIR_WIN), _f32)],
    )
    return pl.pallas_call(
        functools.partial(_attn_kernel, pieces=pieces),
        grid_spec=grid_spec,
        out_shape=jax.ShapeDtypeStruct(q.shape, _bf16),
        compiler_params=_params("arbitrary", "arbitrary", "arbitrary"),
        name="attention",
    )(off_tab, var_tab, q, k, v, kc, vc, bycol)


def _ctx_attn_kernel(q_ref, k_ref, v_ref, o_ref):
    o_ref[...] = _attend([(q_ref[...], [k_ref[...]], [v_ref[...]], [None])])[0].astype(o_ref.dtype)


def _ctx_attention(q, k, v):
    bsz, _, clen, _ = q.shape
    spec = pl.BlockSpec((None, None, clen, LANES), lambda b, hp: (b, hp, 0, 0))
    return pl.pallas_call(
        _ctx_attn_kernel,
        grid=(bsz, HEAD_PAIRS),
        in_specs=[spec, spec, spec],
        out_specs=spec,
        out_shape=jax.ShapeDtypeStruct(q.shape, _bf16),
        compiler_params=_params("arbitrary", "arbitrary"),
        name="ctx_attention",
    )(q, k, v)


def _window_plan(rows):
    patterns, off_tab, var_tab = {}, [], []
    for r in range(0, rows, PAIR_ROWS):
        ws = int(np.clip(r - NA_KH // 2, 0, rows - PAIR_WIN_ROWS))
        r0 = r // Q_ROWS * Q_ROWS
        block_ws = int(np.clip(r0 - NA_KH // 2, 0, rows - WIN_ROWS))
        assert 0 <= ws - block_ws <= WIN_ROWS - PAIR_WIN_ROWS
        piece = []
        for j in range(PAIR_ROWS):
            r_start = int(np.clip(r + j - NA_KH // 2, 0, rows - NA_KH))
            assert ws <= r_start and r_start + NA_KH <= ws + PAIR_WIN_ROWS
            piece.append(tuple(ws + kr - (r + j) + NA_KH - 1 if r_start <= ws + kr < r_start + NA_KH else N_DR - 1
                               for kr in range(PAIR_WIN_ROWS)))
        var_tab.append(patterns.setdefault(tuple(piece), len(patterns)))
        off_tab.append(ws - block_ws)
    return jnp.asarray(off_tab, jnp.int32), jnp.asarray(var_tab, jnp.int32), tuple(patterns)


def _window_col_table(rpb):
    col = np.arange(GRID_W)
    col_start = np.clip(col - NA_KW // 2, 0, GRID_W - NA_KW)
    col_ok = (col[None, :] >= col_start[:, None]) & (col[None, :] < col_start[:, None] + NA_KW)
    dc = np.where(col_ok, col[None, :] - col[:, None] + NA_KW - 1, 0)
    onehot = np.zeros((2 * NA_KW - 1, GRID_W * GRID_W), np.float32)
    onehot[dc.reshape(-1), np.arange(GRID_W * GRID_W)] = col_ok.reshape(-1)
    by_col = jnp.dot(rpb.reshape(-1, 2 * NA_KW - 1), jnp.asarray(onehot), precision=lax.Precision.HIGHEST)
    by_col = by_col.reshape(NA_HEADS, 2 * NA_KH - 1, GRID_W, GRID_W)
    by_col = jnp.where(jnp.asarray(col_ok)[None, None], by_col, MASKED)
    by_col = jnp.concatenate([by_col, jnp.full((NA_HEADS, 1, GRID_W, GRID_W), MASKED, _f32)], axis=1)
    by_col = jnp.concatenate([by_col, by_col], axis=-1)
    return by_col.reshape(HEAD_PAIRS, 2, N_DR, GRID_W, LANES)


def _merge_kernel(x_ref, o_ref, gu_ref, vn_ref, ga_ref, shf_ref, scf_ref, ws_ref, bs_ref, og_ref,
                  wout_ref, wr_ref, br_ref, tri_ref, sel_ref, xn_ref, row_ref, meta_ref, cnt_ref, *, tm):
    first = (pl.program_id(0) == 0) & (pl.program_id(1) == 0)

    @pl.when(first)
    def _():
        cnt_ref[...] = jnp.zeros_like(cnt_ref)

    nchunk = tm // SGU_CHUNK
    vn = vn_ref[...]
    lane = lax.broadcasted_iota(jnp.int32, (SGU_CHUNK, LANES), 1)
    low = lane < SGU_GROUP_DIM
    mixed_cols = []
    for gp in range(SGU_W // LANES):
        rhs_cols = []
        for n in range(nchunk):
            blk = vn[n * SGU_CHUNK:(n + 1) * SGU_CHUNK, gp * LANES:(gp + 1) * LANES]
            zero = jnp.zeros_like(blk)
            rhs_cols.append(jnp.concatenate([jnp.where(low, blk, zero), jnp.where(low, zero, blk)], axis=0))
        rhs = jnp.concatenate(rhs_cols, axis=1)
        mix = _dot(ws_ref[gp], rhs)
        mixed_cols.append(jnp.concatenate([mix[:, n * LANES:(n + 1) * LANES] for n in range(nchunk)], axis=0))
    mixed = jnp.concatenate(mixed_cols, axis=1)
    bias = jnp.concatenate([bs_ref[...]] * nchunk, axis=0)
    o_sg = gu_ref[...].astype(_f32) * (mixed + bias)
    o_na = jnp.concatenate([o_ref[hp].astype(_f32) for hp in range(HEAD_PAIRS)], axis=1)

    def rms(t):
        return t * lax.rsqrt(jnp.mean(t * t, -1, keepdims=True) + EPS)

    y = jnp.concatenate([rms(o_na), rms(o_sg)], axis=1) * og_ref[...]
    xn = x_ref[...] + ga_ref[...] * _dot(y.astype(_bf16), wout_ref[...])
    xn_ref[...] = xn
    hf = rms(xn) * (1.0 + scf_ref[...]) + shf_ref[...]
    row_ref[:, :D_MODEL] = hf

    lg = _dot_nt(wr_ref[...], hf.astype(_bf16)) + br_ref[...]
    quad = lambda r: lg[r:r + EXPERTS_PER_GROUP, :]
    row4 = lax.broadcasted_iota(jnp.int32, (EXPERTS_PER_GROUP, tm), 0).astype(_f32)
    over = lambda fn, t: fn(t, 0, keepdims=True)
    gl = quad(0)
    gm = over(jnp.max, gl)
    p_top = 1.0 / over(jnp.sum, jnp.exp(gl - gm))
    g_idx = over(jnp.min, jnp.where(gl == gm, row4, float(N_GROUPS)))
    el = quad(SUBLANES * N_GROUPS)
    for g in range(N_GROUPS - 2, -1, -1):
        el = jnp.where(g_idx == float(g), quad(SUBLANES * (g + 1)), el)
    ee = jnp.exp(el - over(jnp.max, el))
    pe = ee / over(jnp.sum, ee)
    v1 = over(jnp.max, pe)
    i1 = over(jnp.min, jnp.where(pe == v1, row4, float(EXPERTS_PER_GROUP)))
    pe2 = jnp.where(row4 == i1, -1.0, pe)
    v2 = over(jnp.max, pe2)
    i2 = over(jnp.min, jnp.where(pe2 == v2, row4, float(EXPERTS_PER_GROUP)))
    den = v1 + v2
    w1 = v1 / den * p_top
    w2 = v2 / den * p_top
    first_low = i1 < i2
    a = jnp.minimum(i1, i2)
    b = jnp.maximum(i1, i2)
    lex = a * (7.0 - a) * 0.5 + (b - a - 1.0)
    pair = lex
    for src, dst in enumerate(PAIR_OF_LEX):
        if src != dst:
            pair = jnp.where(lex == float(src), float(dst), pair)
    cls = g_idx * float(N_PAIRS) + pair
    gate_a = jnp.where(first_low, w1, w2)
    gate_b = jnp.where(first_low, w2, w1)

    row_p = lax.broadcasted_iota(jnp.int32, (2 * SUBLANES, tm), 0)
    stacked = jnp.zeros((2 * SUBLANES, tm), _f32)
    for n, gate in enumerate((gate_a, gate_b)):
        rest = gate
        for k in range(GATE_PIECES):
            piece = rest.astype(_bf16).astype(_f32)
            stacked = jnp.where(row_p == n * GATE_PIECES + k, piece, stacked)
            rest = rest - piece
    row_ref[:, D_MODEL:] = lax.dot_general(stacked.astype(_bf16), sel_ref[...],
                                           (((0,), (0,)), ((), ())), preferred_element_type=_f32)

    row_c = lax.broadcasted_iota(jnp.int32, (CLASS_ROWS, tm), 0).astype(_f32)
    onehot = row_c == cls
    ones = jnp.where(onehot, 1.0, 0.0)
    before = _dot(ones.astype(_bf16), tri_ref[...])
    rank = over(jnp.sum, jnp.where(onehot, before + cnt_ref[:, 0:1], 0.0))
    cnt_ref[...] += jnp.sum(ones, 1, keepdims=True)
    meta_ref[...] = jnp.concatenate([cls, rank, jnp.zeros((SUBLANES - 2, tm), _f32)], axis=0)


def _merge(x, o_na, gu, vn, g_a, sh_f, sc_f, ws_cat, bs_tab, out_gain, w_out, w_r, b_r, tri, sel, tm):
    bsz, length, _ = x.shape
    mod = pl.BlockSpec((None, 1, D_MODEL), lambda b, i: (b, 0, 0))
    tok = lambda n: pl.BlockSpec((None, tm, n), lambda b, i: (b, i, 0))
    const = lambda *shape: pl.BlockSpec(shape, lambda b, i: (0,) * len(shape))
    return pl.pallas_call(
        functools.partial(_merge_kernel, tm=tm),
        grid=(bsz, length // tm),
        in_specs=[tok(D_MODEL),
                  pl.BlockSpec((None, HEAD_PAIRS, tm, LANES), lambda b, i: (b, 0, i, 0)),
                  tok(SGU_W), tok(SGU_W), mod, mod, mod,
                  const(SGU_W // LANES, SGU_CHUNK, 2 * SGU_CHUNK),
                  const(SGU_CHUNK, SGU_W),
                  const(1, D_MODEL),
                  const(D_MODEL, D_MODEL),
                  const(ROUTER_ROWS, D_MODEL),
                  const(ROUTER_ROWS, 1),
                  const(tm, tm),
                  const(2 * SUBLANES, LANES)],
        out_specs=[tok(D_MODEL), tok(ROW_W),
                   pl.BlockSpec((None, SUBLANES, tm), lambda b, i: (b, 0, i)),
                   const(CLASS_ROWS, LANES)],
        out_shape=[jax.ShapeDtypeStruct((bsz, length, D_MODEL), _f32),
                   jax.ShapeDtypeStruct((bsz, length, ROW_W), _f32),
                   jax.ShapeDtypeStruct((bsz, SUBLANES, length), _f32),
                   jax.ShapeDtypeStruct((CLASS_ROWS, LANES), _f32)],
        compiler_params=_params("arbitrary", "arbitrary"),
        name="merge",
    )(x, o_na, gu, vn, g_a, sh_f, sc_f, ws_cat, bs_tab, out_gain, w_out, w_r, b_r, tri, sel)


def _issue_row_copies(count, make_copy):
    for j in range(count):
        make_copy(j).start(priority=j % 2)


def _dispatch_kernel(fill_ref, pos_ref, rows_ref, xs_ref, zero_ref, sem, fill_sem, *, td, tile):
    @pl.when(pl.program_id(0) == 0)
    def _():
        zero_ref[...] = jnp.zeros_like(zero_ref)

        def fill(k):
            start = pl.multiple_of(jnp.maximum(fill_ref[k], 0), tile)
            return pltpu.make_async_copy(zero_ref, xs_ref.at[pl.ds(start, tile)], fill_sem)

        for k in range(fill_ref.shape[0]):
            @pl.when(fill_ref[k] >= 0)
            def _():
                fill(k).start()
        for k in range(fill_ref.shape[0]):
            @pl.when(fill_ref[k] >= 0)
            def _():
                fill(k).wait()

    _issue_row_copies(td, lambda j: pltpu.make_async_copy(
        rows_ref.at[pl.ds(j, 1)], xs_ref.at[pl.ds(pos_ref[0, j], 1)], sem))
    pltpu.make_async_copy(rows_ref, xs_ref.at[pl.ds(0, td)], sem).wait()


def _dispatch(rows, pos, fill, padded, td, tile):
    n = rows.shape[0]
    grid_spec = pltpu.PrefetchScalarGridSpec(
        num_scalar_prefetch=1,
        grid=(n // td,),
        in_specs=[pl.BlockSpec((None, 1, td), lambda i, fl: (i, 0, 0), memory_space=pltpu.SMEM),
                  pl.BlockSpec((td, ROW_W), lambda i, fl: (i, 0))],
        out_specs=pl.BlockSpec(memory_space=pl.ANY),
        scratch_shapes=[pltpu.VMEM((tile, ROW_W), _f32), pltpu.SemaphoreType.DMA(()), pltpu.SemaphoreType.DMA(())],
    )
    return pl.pallas_call(
        functools.partial(_dispatch_kernel, td=td, tile=tile),
        grid_spec=grid_spec,
        out_shape=jax.ShapeDtypeStruct((padded, ROW_W), _f32),
        compiler_params=_params("arbitrary"),
        name="dispatch",
    )(fill, pos.reshape(n // td, 1, td), rows)


def _combine_kernel(pos_ref, npos_ref, x_ref, g_ref, ys_ref, o_ref, buf0, buf1, sems, *, tc, nsteps):
    s = pl.program_id(0) * pl.num_programs(1) + pl.program_id(1)
    bufs = (buf0, buf1)

    def request(idx_ref, slot):
        _issue_row_copies(tc, lambda j: pltpu.make_async_copy(
            ys_ref.at[pl.ds(idx_ref[0, j], 1)], bufs[slot].at[pl.ds(j, 1)], sems.at[slot]))

    @pl.when(s == 0)
    def _():
        request(pos_ref, 0)

    for slot in range(2):
        @pl.when((s % 2 == slot) & (s + 1 < nsteps))
        def _():
            request(npos_ref, 1 - slot)

        @pl.when(s % 2 == slot)
        def _():
            pltpu.make_async_copy(ys_ref.at[pl.ds(0, tc)], bufs[slot], sems.at[slot]).wait()
            o_ref[...] = x_ref[...] + g_ref[...] * bufs[slot][...]


def _combine(x, ys, pos, g, tc):
    bsz, length, _ = x.shape
    per_batch = length // tc
    nsteps = bsz * per_batch
    tok = pl.BlockSpec((None, tc, D_MODEL), lambda b, i: (b, i, 0))
    pos3 = pos.reshape(nsteps, 1, tc)
    return pl.pallas_call(
        functools.partial(_combine_kernel, tc=tc, nsteps=nsteps),
        grid=(bsz, per_batch),
        in_specs=[pl.BlockSpec((None, 1, tc), lambda b, i: (b * per_batch + i, 0, 0), memory_space=pltpu.SMEM),
                  pl.BlockSpec((None, 1, tc), lambda b, i: (jnp.minimum(b * per_batch + i + 1, nsteps - 1), 0, 0),
                               memory_space=pltpu.SMEM),
                  tok,
                  pl.BlockSpec((None, 1, D_MODEL), lambda b, i: (b, 0, 0)),
                  pl.BlockSpec(memory_space=pl.ANY)],
        out_specs=tok,
        out_shape=jax.ShapeDtypeStruct(x.shape, _f32),
        scratch_shapes=[pltpu.VMEM((tc, D_MODEL), _f32), pltpu.VMEM((tc, D_MODEL), _f32),
                        pltpu.SemaphoreType.DMA((2,))],
        compiler_params=_params("arbitrary", "arbitrary"),
        name="combine",
    )(pos3, pos3, x, g, ys)


def _experts_kernel(ea_ref, eb_ref, chg_ref, nlive_ref, xs_ref, w1a_ref, w3a_ref, w2a_ref,
                    w1b_ref, w3b_ref, w2b_ref, y_ref, c1a, c3a, c2a, c1b, c3b, c2b):
    t = pl.program_id(0)
    live = t < nlive_ref[0]

    @pl.when(live & ((chg_ref[t] & 1) != 0))
    def _():
        for src, dst in ((w1a_ref, c1a), (w3a_ref, c3a), (w2a_ref, c2a)):
            dst[...] = src[...].astype(_bf16)

    @pl.when(live & ((chg_ref[t] & 2) != 0))
    def _():
        for src, dst in ((w1b_ref, c1b), (w3b_ref, c3b), (w2b_ref, c2b)):
            dst[...] = src[...].astype(_bf16)

    def run(nrows):
        xs = xs_ref[:nrows, :D_MODEL].astype(_bf16)
        gates = xs_ref[:nrows, D_MODEL:]

        def expert(w1_ref, w3_ref, w2_ref):
            hid = jax.nn.silu(_dot(xs, w1_ref[...])) * _dot(xs, w3_ref[...])
            return _dot(hid.astype(_bf16), w2_ref[...])

        y_ref[:nrows, :] = gates[:, 0:1] * expert(c1a, c3a, c2a) + gates[:, 1:2] * expert(c1b, c3b, c2b)

    half_only = (chg_ref[t] & 4) != 0
    half = y_ref.shape[0] // 2

    @pl.when(live & jnp.logical_not(half_only))
    def _():
        run(y_ref.shape[0])

    @pl.when(live & half_only)
    def _():
        run(half)
        y_ref[half:, :] = jnp.zeros((half, D_MODEL), _f32)

    @pl.when(jnp.logical_not(live))
    def _():
        y_ref[...] = jnp.zeros_like(y_ref)


def _experts(xs, tile_ea, tile_eb, chg, nlive, w1, w3, w2, layer, tile):
    ntiles = xs.shape[0] // tile
    up_a = pl.BlockSpec((None, None, D_MODEL, D_EXPERT), lambda t, ea, eb, cg, nl: (layer, ea[t], 0, 0))
    dn_a = pl.BlockSpec((None, None, D_EXPERT, D_MODEL), lambda t, ea, eb, cg, nl: (layer, ea[t], 0, 0))
    up_b = pl.BlockSpec((None, None, D_MODEL, D_EXPERT), lambda t, ea, eb, cg, nl: (layer, eb[t], 0, 0))
    dn_b = pl.BlockSpec((None, None, D_EXPERT, D_MODEL), lambda t, ea, eb, cg, nl: (layer, eb[t], 0, 0))
    grid_spec = pltpu.PrefetchScalarGridSpec(
        num_scalar_prefetch=4,
        grid=(ntiles,),
        in_specs=[pl.BlockSpec((tile, ROW_W), lambda t, ea, eb, cg, nl: (t, 0)),
                  up_a, up_a, dn_a, up_b, up_b, dn_b],
        out_specs=pl.BlockSpec((tile, D_MODEL), lambda t, ea, eb, cg, nl: (t, 0)),
        scratch_shapes=[pltpu.VMEM((D_MODEL, D_EXPERT), _bf16), pltpu.VMEM((D_MODEL, D_EXPERT), _bf16),
                        pltpu.VMEM((D_EXPERT, D_MODEL), _bf16)] * 2,
    )
    return pl.pallas_call(
        _experts_kernel,
        grid_spec=grid_spec,
        out_shape=jax.ShapeDtypeStruct((xs.shape[0], D_MODEL), _f32),
        compiler_params=_params("arbitrary"),
        name="experts",
    )(tile_ea, tile_eb, chg, nlive, xs, w1, w3, w2, w1, w3, w2)


def _dispatch_plan(cls, rank, counts, tile):
    n = cls.shape[0]
    padded = -(-(n + N_CLASSES * (tile - 1)) // tile) * tile
    ntiles = padded // tile
    pcounts = (counts + tile - 1) // tile * tile
    pad_end = jnp.cumsum(pcounts)
    pad_off = pad_end - pcounts
    pos = pad_off[cls] + rank
    nlive = pad_end[-1] // tile
    tile_start = jnp.arange(ntiles, dtype=jnp.int32) * tile
    tile_cls = jnp.sum((pad_end[None, :] <= jnp.minimum(tile_start, pad_end[-1] - 1)[:, None]).astype(jnp.int32), axis=1)
    tile_cls = jnp.minimum(tile_cls, N_CLASSES - 1)
    grp = tile_cls // N_PAIRS
    pair = tile_cls % N_PAIRS
    tile_ea = grp * EXPERTS_PER_GROUP + jnp.asarray(PAIR_A, jnp.int32)[pair]
    tile_eb = grp * EXPERTS_PER_GROUP + jnp.asarray(PAIR_B, jnp.int32)[pair]
    differs = lambda e: jnp.concatenate([jnp.ones((1,), jnp.int32), (e[1:] != e[:-1]).astype(jnp.int32)])
    tokens_in_tile = (pad_off + counts)[tile_cls] - tile_start
    chg = differs(tile_ea) + 2 * differs(tile_eb) + 4 * (tokens_in_tile <= tile // 2).astype(jnp.int32)
    class_tail = jnp.where(pcounts > 0, pad_end - tile, -1)
    past = (nlive + jnp.arange(N_CLASSES, dtype=jnp.int32)) * tile
    fill = jnp.concatenate([class_tail, jnp.where(past < padded, past, -1)]).astype(jnp.int32)
    return pos.astype(jnp.int32), padded, tile_ea, tile_eb, chg, nlive.astype(jnp.int32).reshape(1), fill


def _ffn(x_new, rows, meta, counts, g_f, w1, w3, w2, layer, tile, tdma):
    bsz, length, _ = x_new.shape
    n = bsz * length
    cls = meta[:, 0, :].reshape(n).astype(jnp.int32)
    rank = meta[:, 1, :].reshape(n).astype(jnp.int32)
    cnt = counts[:N_CLASSES, 0].astype(jnp.int32)
    pos, padded, tile_ea, tile_eb, chg, nlive, fill = _dispatch_plan(cls, rank, cnt, tile)
    xs = _dispatch(rows.reshape(n, ROW_W), pos, fill, padded, tdma, tile)
    ys = _experts(xs, tile_ea, tile_eb, chg, nlive, w1, w3, w2, layer, tile)
    return _combine(x_new, ys, pos, g_f, tdma)


def kernel(x, c, ctx, c_ctx, w_ada, b_ada, w_in, q_gain, k_gain, rpb, sgu_ln, sgu_w, sgu_b, out_gain,
           w_out, rg_w, rg_b, re_w, re_b, w1, w3, w2):
    bsz, seq, _ = x.shape
    depth = w_ada.shape[0]
    rows = seq // GRID_W
    assert seq % Q_BLOCK == 0 and rows // Q_ROWS >= 3 and ctx.shape[1] % SGU_CHUNK == 0
    tm_x, tm_c = 512, ctx.shape[1]

    cond = jnp.zeros((SUBLANES, D_MODEL), _f32).at[:bsz].set(c).at[bsz].set(c_ctx)
    mods = _adaln(cond, w_ada, b_ada).reshape(depth, SUBLANES, N_MOD, D_MODEL)

    assert MXU_TILE % NA_HEAD_DIM == 0 and NA_W % MXU_TILE == 0
    head_id = np.arange(MXU_TILE) // NA_HEAD_DIM
    avg = jnp.asarray((head_id[:, None] == head_id[None, :]).astype(np.float32) / NA_HEAD_DIM, _bf16)
    win_plan = _window_plan(rows)
    tri = {tm: jnp.asarray(np.triu(np.ones((tm, tm), np.float32), 1), _bf16) for tm in {tm_x, tm_c}}
    sel_np = np.zeros((2 * SUBLANES, LANES), np.float32)
    for n in range(2):
        sel_np[n * GATE_PIECES:(n + 1) * GATE_PIECES, n] = 1.0
    sel = jnp.asarray(sel_np, _bf16)

    for i in range(depth):
        last = i == depth - 1
        mx = [mods[i, :bsz, j][:, None, :] for j in range(N_MOD)]
        mc = [jnp.broadcast_to(mods[i, bsz, j][None, None, :], (bsz, 1, D_MODEL)) for j in range(N_MOD)]
        w_in_b = w_in[i].astype(_bf16)
        qg = (jnp.tile(q_gain[i], NA_HEADS) * NA_HEAD_DIM ** -0.5)[None, :]
        kg = jnp.tile(k_gain[i], NA_HEADS)[None, :]
        ln_g = sgu_ln[i][None, :]
        ws_cat = jnp.concatenate([sgu_w[i, 0::2], sgu_w[i, 1::2]], axis=2).astype(_bf16)
        bs_tab = jnp.repeat(sgu_b[i].T, SGU_GROUP_DIM, axis=1)
        og = out_gain[i][None, :]
        w_out_b = w_out[i].astype(_bf16)
        pad_w = jnp.zeros((SUBLANES - EXPERTS_PER_GROUP, D_MODEL), _f32)
        pad_b = jnp.zeros((SUBLANES - EXPERTS_PER_GROUP,), _f32)
        w_r = jnp.concatenate([rg_w[i].T, pad_w] + [t for g in range(N_GROUPS) for t in (re_w[i, g].T, pad_w)]
                              + [jnp.zeros((ROUTER_ROWS - SUBLANES * (N_GROUPS + 1), D_MODEL), _f32)]).astype(_bf16)
        b_r = jnp.concatenate([rg_b[i], pad_b] + [t for g in range(N_GROUPS) for t in (re_b[i, g], pad_b)]
                              + [jnp.zeros((ROUTER_ROWS - SUBLANES * (N_GROUPS + 1),), _f32)])[:, None]
        bycol = _window_col_table(rpb[i])

        q, k, v, gu, vn = _proj(x, mx[0], mx[1], w_in_b, avg, qg, kg, ln_g, tm_x)
        qc, kc, vc, guc, vnc = _proj(ctx, mc[0], mc[1], w_in_b, avg, qg, kg, ln_g, tm_c)
        o_na = _attention(q, k, v, kc, vc, bycol, win_plan)
        x_new, xrows, meta, counts = _merge(x, o_na, gu, vn, mx[2], mx[3], mx[4], ws_cat, bs_tab, og,
                                            w_out_b, w_r, b_r, tri[tm_x], sel, tm_x)
        x = _ffn(x_new, xrows, meta, counts, mx[5], w1, w3, w2, i, 512, 1024)
        if not last:
            oc_na = _ctx_attention(qc, kc, vc)
            c_new, crows, cmeta, ccounts = _merge(ctx, oc_na, guc, vnc, mc[2], mc[3], mc[4], ws_cat, bs_tab,
                                                  og, w_out_b, w_r, b_r, tri[tm_c], sel, tm_c)
            ctx = _ffn(c_new, crows, cmeta, ccounts, mc[5], w1, w3, w2, i, 64, tm_c)
    return x
```

```python
import functools

import jax
import jax.numpy as jnp
import numpy as np
from jax import lax
from jax.experimental import pallas as pl
from jax.experimental.pallas import tpu as pltpu

D_MODEL = 1024
GRID_W = 64
NA_HEADS = 8
NA_HEAD_DIM = 64
NA_W = NA_HEADS * NA_HEAD_DIM
NA_KH = 8
NA_KW = 16
SGU_GROUPS = 8
SGU_W = D_MODEL - NA_W
SGU_GROUP_DIM = SGU_W // SGU_GROUPS
SGU_CHUNK = 128
IN_COLS = 3 * NA_W + 2 * SGU_W
N_GROUPS = 4
EXPERTS_PER_GROUP = 4
N_EXPERTS = N_GROUPS * EXPERTS_PER_GROUP
D_EXPERT = D_MODEL // 2
N_MOD = 6
EPS = 1e-6

LANES = 128
SUBLANES = 8
HEAD_PAIRS = NA_W // LANES
Q_ROWS = 8
Q_BLOCK = Q_ROWS * GRID_W
WIN_ROWS = Q_ROWS + NA_KH - 1
WIN_BLOCK = WIN_ROWS * GRID_W
PAIR_ROWS = 2
PAIR_BLOCK = PAIR_ROWS * GRID_W
PAIR_WIN_ROWS = PAIR_ROWS + NA_KH - 1
PAIR_WIN = PAIR_WIN_ROWS * GRID_W
PAIRS_IN_FLIGHT = 2
N_DR = 2 * NA_KH
N_PAIRS = 6
N_CLASSES = N_GROUPS * N_PAIRS
PAIR_A = (0, 0, 1, 1, 0, 2)
PAIR_B = (1, 2, 2, 3, 3, 3)
PAIR_OF_LEX = (0, 1, 4, 2, 3, 5)
CLASS_ROWS = 32
ROUTER_ROWS = 48
GATE_PIECES = 3
ROW_W = D_MODEL + LANES
MXU_TILE = 256
MASKED = -1e30
VMEM_LIMIT_BYTES = 56 * 1024 * 1024

_f32 = jnp.float32
_bf16 = jnp.bfloat16


def _params(*semantics):
    return pltpu.CompilerParams(dimension_semantics=semantics, vmem_limit_bytes=VMEM_LIMIT_BYTES)


def _dot(a, b):
    return jnp.dot(a, b, preferred_element_type=_f32)


def _dot_nt(a, b):
    return lax.dot_general(a, b, (((1,), (1,)), ((), ())), preferred_element_type=_f32)


def _adaln_kernel(cond_ref, w_ref, b_ref, out_ref):
    cond = cond_ref[...]
    act = cond * jax.nn.sigmoid(cond)
    out_ref[...] = _dot(act.astype(_bf16), w_ref[...].astype(_bf16)) + b_ref[...]


def _adaln(cond, w_ada, b_ada):
    depth = w_ada.shape[0]
    cols = N_MOD * D_MODEL
    col_block = 1536
    return pl.pallas_call(
        _adaln_kernel,
        grid=(depth, cols // col_block),
        in_specs=[
            pl.BlockSpec((SUBLANES, D_MODEL), lambda d, j: (0, 0)),
            pl.BlockSpec((None, D_MODEL, col_block), lambda d, j: (d, 0, j)),
            pl.BlockSpec((None, 1, col_block), lambda d, j: (d, 0, j)),
        ],
        out_specs=pl.BlockSpec((None, SUBLANES, col_block), lambda d, j: (d, 0, j)),
        out_shape=jax.ShapeDtypeStruct((depth, SUBLANES, cols), _f32),
        compiler_params=_params("arbitrary", "arbitrary"),
        name="adaln",
    )(cond, w_ada, b_ada.reshape(depth, 1, cols))


def _proj_kernel(x_ref, sh_ref, sc_ref, w_ref, avg_ref, qg_ref, kg_ref, ln_ref,
                 q_ref, k_ref, v_ref, gu_ref, vn_ref):
    x = x_ref[...]
    h = x * lax.rsqrt(jnp.mean(x * x, -1, keepdims=True) + EPS) * (1.0 + sc_ref[...]) + sh_ref[...]
    p = _dot(h.astype(_bf16), w_ref[...])
    q = p[:, :NA_W]
    k = p[:, NA_W:2 * NA_W]
    v = p[:, 2 * NA_W:3 * NA_W]
    u = p[:, 3 * NA_W:3 * NA_W + SGU_W]
    vs = p[:, 3 * NA_W + SGU_W:]
    def head_mean(t):
        sq = (t * t).astype(_bf16)
        return jnp.concatenate([_dot(sq[:, c:c + MXU_TILE], avg_ref[...]) for c in range(0, NA_W, MXU_TILE)], axis=1)

    q_ms = head_mean(q)
    k_ms = head_mean(k)
    qn = (q * lax.rsqrt(q_ms + EPS) * qg_ref[...]).astype(_bf16)
    kn = (k * lax.rsqrt(k_ms + EPS) * kg_ref[...]).astype(_bf16)
    vb = v.astype(_bf16)
    for hp in range(HEAD_PAIRS):
        cols = slice(hp * LANES, (hp + 1) * LANES)
        q_ref[hp] = qn[:, cols]
        k_ref[hp] = kn[:, cols]
        v_ref[hp] = vb[:, cols]
    gu_ref[...] = jax.nn.gelu(u).astype(_bf16)
    gv = jax.nn.gelu(vs)
    mu = jnp.mean(gv, -1, keepdims=True)
    cen = gv - mu
    var = jnp.mean(cen * cen, -1, keepdims=True)
    vn_ref[...] = (cen * lax.rsqrt(var + EPS) * ln_ref[...]).astype(_bf16)


def _proj(x, sh, sc, w_in, avg, qg, kg, ln_g, tm):
    bsz, length, _ = x.shape
    vec = lambda n: pl.BlockSpec((1, n), lambda b, i: (0, 0))
    mod = pl.BlockSpec((None, 1, D_MODEL), lambda b, i: (b, 0, 0))
    tok = lambda n: pl.BlockSpec((None, tm, n), lambda b, i: (b, i, 0))
    hpm = pl.BlockSpec((None, HEAD_PAIRS, tm, LANES), lambda b, i: (b, 0, i, 0))
    hpm_shape = jax.ShapeDtypeStruct((bsz, HEAD_PAIRS, length, LANES), _bf16)
    sgu_shape = jax.ShapeDtypeStruct((bsz, length, SGU_W), _bf16)
    return pl.pallas_call(
        _proj_kernel,
        grid=(bsz, length // tm),
        in_specs=[tok(D_MODEL), mod, mod,
                  pl.BlockSpec((D_MODEL, IN_COLS), lambda b, i: (0, 0)),
                  pl.BlockSpec((MXU_TILE, MXU_TILE), lambda b, i: (0, 0)),
                  vec(NA_W), vec(NA_W), vec(SGU_W)],
        out_specs=[hpm, hpm, hpm, tok(SGU_W), tok(SGU_W)],
        out_shape=[hpm_shape, hpm_shape, hpm_shape, sgu_shape, sgu_shape],
        compiler_params=_params("arbitrary", "arbitrary"),
        name="proj",
    )(x, sh, sc, w_in, avg, qg, kg, ln_g)


def _attend(blocks):
    add = lambda a, b: a + b
    m_rows = blocks[0][0].shape[0]
    lane = lax.broadcasted_iota(jnp.int32, blocks[0][0].shape, 1)
    low = lane < NA_HEAD_DIM
    stacked = [jnp.concatenate([jnp.where(low, q, jnp.zeros_like(q)), jnp.where(low, jnp.zeros_like(q), q)], axis=0)
               for q, _, _, _ in blocks]
    scores = [[_dot_nt(qs, key) if bias is None else _dot_nt(qs, key) + bias[...]
               for key, bias in zip(keys, biases)] for qs, (_, keys, _, biases) in zip(stacked, blocks)]
    maxes = [functools.reduce(jnp.maximum, [jnp.max(s, -1, keepdims=True) for s in ss]) for ss in scores]
    probs = [[jnp.exp(s - m) for s in ss] for ss, m in zip(scores, maxes)]
    dens = [functools.reduce(add, [jnp.sum(p, -1, keepdims=True) for p in ps]) for ps in probs]
    accs = [functools.reduce(add, [_dot(p.astype(_bf16), val) for p, val in zip(ps, blk[2])])
            for ps, blk in zip(probs, blocks)]
    outs = [acc / den for acc, den in zip(accs, dens)]
    return [jnp.where(low, o2[:m_rows], o2[m_rows:]) for o2 in outs]


def _attn_kernel(off_ref, var_ref, q_ref, k_ref, v_ref, kc_ref, vc_ref, bycol_ref, o_ref, bias_ref, *, pieces):
    i = pl.program_id(2)

    @pl.when(i == 0)
    def _():
        for var, piece in enumerate(pieces):
            for hh in range(2):
                for j in range(PAIR_ROWS):
                    for kr in range(PAIR_WIN_ROWS):
                        lo = (kr % 2) * GRID_W
                        row = hh * PAIR_BLOCK + j * GRID_W
                        bias_ref[var, row:row + GRID_W, kr * GRID_W:(kr + 1) * GRID_W] = (
                            bycol_ref[hh, piece[j][kr], :, lo:lo + GRID_W])

    kc = kc_ref[...]
    vc = vc_ref[...]
    npair = Q_ROWS // PAIR_ROWS
    for g0 in range(0, npair, PAIRS_IN_FLIGHT):
        blocks = []
        for g in range(g0, g0 + PAIRS_IN_FLIGHT):
            start = pl.multiple_of(off_ref[i * npair + g] * GRID_W, GRID_W)
            var = var_ref[i * npair + g]
            blocks.append((q_ref[g * PAIR_BLOCK:(g + 1) * PAIR_BLOCK, :],
                           [k_ref[pl.ds(start, PAIR_WIN), :], kc],
                           [v_ref[pl.ds(start, PAIR_WIN), :], vc],
                           [bias_ref.at[var], None]))
        for g, out in zip(range(g0, g0 + PAIRS_IN_FLIGHT), _attend(blocks)):
            o_ref[g * PAIR_BLOCK:(g + 1) * PAIR_BLOCK, :] = out.astype(o_ref.dtype)


def _attention(q, k, v, kc, vc, bycol, plan):
    off_tab, var_tab, pieces = plan
    bsz, _, seq, _ = q.shape
    clen = kc.shape[2]
    nblk = seq // Q_BLOCK

    def win_start(i):
        return jnp.clip(i * Q_ROWS - NA_KH // 2, 0, seq // GRID_W - WIN_ROWS) * GRID_W

    qspec = pl.BlockSpec((None, None, Q_BLOCK, LANES), lambda hp, b, i, *_: (b, hp, i, 0))
    wspec = pl.BlockSpec((None, None, pl.Element(WIN_BLOCK), pl.Element(LANES)),
                         lambda hp, b, i, *_: (b, hp, win_start(i), 0))
    cspec = pl.BlockSpec((None, None, clen, LANES), lambda hp, b, i, *_: (b, hp, 0, 0))
    tspec = pl.BlockSpec((None, 2, N_DR, GRID_W, LANES), lambda hp, b, i, *_: (hp, 0, 0, 0, 0))
    grid_spec = pltpu.PrefetchScalarGridSpec(
        num_scalar_prefetch=2,
        grid=(HEAD_PAIRS, bsz, nblk),
        in_specs=[qspec, wspec, wspec, cspec, cspec, tspec],
        out_specs=qspec,
        scratch_shapes=[pltpu.VMEM((len(pieces), 2 * PAIR_BLOCK, PAIR_WIN), _f32)],
    )
    return pl.pallas_call(
        functools.partial(_attn_kernel, pieces=pieces),
        grid_spec=grid_spec,
        out_shape=jax.ShapeDtypeStruct(q.shape, _bf16),
        compiler_params=_params("arbitrary", "arbitrary", "arbitrary"),
        name="attention",
    )(off_tab, var_tab, q, k, v, kc, vc, bycol)


def _ctx_attn_kernel(q_ref, k_ref, v_ref, o_ref):
    o_ref[...] = _attend([(q_ref[...], [k_ref[...]], [v_ref[...]], [None])])[0].astype(o_ref.dtype)


def _ctx_attention(q, k, v):
    bsz, _, clen, _ = q.shape
    spec = pl.BlockSpec((None, None, clen, LANES), lambda b, hp: (b, hp, 0, 0))
    return pl.pallas_call(
        _ctx_attn_kernel,
        grid=(bsz, HEAD_PAIRS),
        in_specs=[spec, spec, spec],
        out_specs=spec,
        out_shape=jax.ShapeDtypeStruct(q.shape, _bf16),
        compiler_params=_params("arbitrary", "arbitrary"),
        name="ctx_attention",
    )(q, k, v)


def _window_plan(rows):
    patterns, off_tab, var_tab = {}, [], []
    for r in range(0, rows, PAIR_ROWS):
        ws = int(np.clip(r - NA_KH // 2, 0, rows - PAIR_WIN_ROWS))
        r0 = r // Q_ROWS * Q_ROWS
        block_ws = int(np.clip(r0 - NA_KH // 2, 0, rows - WIN_ROWS))
        assert 0 <= ws - block_ws <= WIN_ROWS - PAIR_WIN_ROWS
        piece = []
        for j in range(PAIR_ROWS):
            r_start = int(np.clip(r + j - NA_KH // 2, 0, rows - NA_KH))
            assert ws <= r_start and r_start + NA_KH <= ws + PAIR_WIN_ROWS
            piece.append(tuple(ws + kr - (r + j) + NA_KH - 1 if r_start <= ws + kr < r_start + NA_KH else N_DR - 1
                               for kr in range(PAIR_WIN_ROWS)))
        var_tab.append(patterns.setdefault(tuple(piece), len(patterns)))
        off_tab.append(ws - block_ws)
    return jnp.asarray(off_tab, jnp.int32), jnp.asarray(var_tab, jnp.int32), tuple(patterns)


def _window_col_table(rpb):
    col = np.arange(GRID_W)
    col_start = np.clip(col - NA_KW // 2, 0, GRID_W - NA_KW)
    col_ok = (col[None, :] >= col_start[:, None]) & (col[None, :] < col_start[:, None] + NA_KW)
    dc = np.where(col_ok, col[None, :] - col[:, None] + NA_KW - 1, 0)
    onehot = np.zeros((2 * NA_KW - 1, GRID_W * GRID_W), np.float32)
    onehot[dc.reshape(-1), np.arange(GRID_W * GRID_W)] = col_ok.reshape(-1)
    by_col = jnp.dot(rpb.reshape(-1, 2 * NA_KW - 1), jnp.asarray(onehot), precision=lax.Precision.HIGHEST)
    by_col = by_col.reshape(NA_HEADS, 2 * NA_KH - 1, GRID_W, GRID_W)
    by_col = jnp.where(jnp.asarray(col_ok)[None, None], by_col, MASKED)
    by_col = jnp.concatenate([by_col, jnp.full((NA_HEADS, 1, GRID_W, GRID_W), MASKED, _f32)], axis=1)
    by_col = jnp.concatenate([by_col, by_col], axis=-1)
    return by_col.reshape(HEAD_PAIRS, 2, N_DR, GRID_W, LANES)


def _merge_kernel(x_ref, o_ref, gu_ref, vn_ref, ga_ref, shf_ref, scf_ref, ws_ref, bs_ref, og_ref,
                  wout_ref, wr_ref, br_ref, tri_ref, sel_ref, xn_ref, row_ref, meta_ref, cnt_ref, *, tm):
    first = (pl.program_id(0) == 0) & (pl.program_id(1) == 0)

    @pl.when(first)
    def _():
        cnt_ref[...] = jnp.zeros_like(cnt_ref)

    nchunk = tm // SGU_CHUNK
    vn = vn_ref[...]
    lane = lax.broadcasted_iota(jnp.int32, (SGU_CHUNK, LANES), 1)
    low = lane < SGU_GROUP_DIM
    mixed_cols = []
    for gp in range(SGU_W // LANES):
        rhs_cols = []
        for n in range(nchunk):
            blk = vn[n * SGU_CHUNK:(n + 1) * SGU_CHUNK, gp * LANES:(gp + 1) * LANES]
            zero = jnp.zeros_like(blk)
            rhs_cols.append(jnp.concatenate([jnp.where(low, blk, zero), jnp.where(low, zero, blk)], axis=0))
        rhs = jnp.concatenate(rhs_cols, axis=1)
        mix = _dot(ws_ref[gp], rhs)
        mixed_cols.append(jnp.concatenate([mix[:, n * LANES:(n + 1) * LANES] for n in range(nchunk)], axis=0))
    mixed = jnp.concatenate(mixed_cols, axis=1)
    bias = jnp.concatenate([bs_ref[...]] * nchunk, axis=0)
    o_sg = gu_ref[...].astype(_f32) * (mixed + bias)
    o_na = jnp.concatenate([o_ref[hp].astype(_f32) for hp in range(HEAD_PAIRS)], axis=1)

    def rms(t):
        return t * lax.rsqrt(jnp.mean(t * t, -1, keepdims=True) + EPS)

    y = jnp.concatenate([rms(o_na), rms(o_sg)], axis=1) * og_ref[...]
    xn = x_ref[...] + ga_ref[...] * _dot(y.astype(_bf16), wout_ref[...])
    xn_ref[...] = xn
    hf = rms(xn) * (1.0 + scf_ref[...]) + shf_ref[...]
    row_ref[:, :D_MODEL] = hf

    lg = _dot_nt(wr_ref[...], hf.astype(_bf16)) + br_ref[...]
    quad = lambda r: lg[r:r + EXPERTS_PER_GROUP, :]
    row4 = lax.broadcasted_iota(jnp.int32, (EXPERTS_PER_GROUP, tm), 0).astype(_f32)
    over = lambda fn, t: fn(t, 0, keepdims=True)
    gl = quad(0)
    gm = over(jnp.max, gl)
    p_top = 1.0 / over(jnp.sum, jnp.exp(gl - gm))
    g_idx = over(jnp.min, jnp.where(gl == gm, row4, float(N_GROUPS)))
    el = quad(SUBLANES * N_GROUPS)
    for g in range(N_GROUPS - 2, -1, -1):
        el = jnp.where(g_idx == float(g), quad(SUBLANES * (g + 1)), el)
    ee = jnp.exp(el - over(jnp.max, el))
    pe = ee / over(jnp.sum, ee)
    v1 = over(jnp.max, pe)
    i1 = over(jnp.min, jnp.where(pe == v1, row4, float(EXPERTS_PER_GROUP)))
    pe2 = jnp.where(row4 == i1, -1.0, pe)
    v2 = over(jnp.max, pe2)
    i2 = over(jnp.min, jnp.where(pe2 == v2, row4, float(EXPERTS_PER_GROUP)))
    den = v1 + v2
    w1 = v1 / den * p_top
    w2 = v2 / den * p_top
    first_low = i1 < i2
    a = jnp.minimum(i1, i2)
    b = jnp.maximum(i1, i2)
    lex = a * (7.0 - a) * 0.5 + (b - a - 1.0)
    pair = lex
    for src, dst in enumerate(PAIR_OF_LEX):
        if src != dst:
            pair = jnp.where(lex == float(src), float(dst), pair)
    cls = g_idx * float(N_PAIRS) + pair
    gate_a = jnp.where(first_low, w1, w2)
    gate_b = jnp.where(first_low, w2, w1)

    row_p = lax.broadcasted_iota(jnp.int32, (2 * SUBLANES, tm), 0)
    stacked = jnp.zeros((2 * SUBLANES, tm), _f32)
    for n, gate in enumerate((gate_a, gate_b)):
        rest = gate
        for k in range(GATE_PIECES):
            piece = rest.astype(_bf16).astype(_f32)
            stacked = jnp.where(row_p == n * GATE_PIECES + k, piece, stacked)
            rest = rest - piece
    row_ref[:, D_MODEL:] = lax.dot_general(stacked.astype(_bf16), sel_ref[...],
                                           (((0,), (0,)), ((), ())), preferred_element_type=_f32)

    row_c = lax.broadcasted_iota(jnp.int32, (CLASS_ROWS, tm), 0).astype(_f32)
    onehot = row_c == cls
    ones = jnp.where(onehot, 1.0, 0.0)
    before = _dot(ones.astype(_bf16), tri_ref[...])
    rank = over(jnp.sum, jnp.where(onehot, before + cnt_ref[:, 0:1], 0.0))
    cnt_ref[...] += jnp.sum(ones, 1, keepdims=True)
    meta_ref[...] = jnp.concatenate([cls, rank, jnp.zeros((SUBLANES - 2, tm), _f32)], axis=0)


def _merge(x, o_na, gu, vn, g_a, sh_f, sc_f, ws_cat, bs_tab, out_gain, w_out, w_r, b_r, tri, sel, tm):
    bsz, length, _ = x.shape
    mod = pl.BlockSpec((None, 1, D_MODEL), lambda b, i: (b, 0, 0))
    tok = lambda n: pl.BlockSpec((None, tm, n), lambda b, i: (b, i, 0))
    const = lambda *shape: pl.BlockSpec(shape, lambda b, i: (0,) * len(shape))
    return pl.pallas_call(
        functools.partial(_merge_kernel, tm=tm),
        grid=(bsz, length // tm),
        in_specs=[tok(D_MODEL),
                  pl.BlockSpec((None, HEAD_PAIRS, tm, LANES), lambda b, i: (b, 0, i, 0)),
                  tok(SGU_W), tok(SGU_W), mod, mod, mod,
                  const(SGU_W // LANES, SGU_CHUNK, 2 * SGU_CHUNK),
                  const(SGU_CHUNK, SGU_W),
                  const(1, D_MODEL),
                  const(D_MODEL, D_MODEL),
                  const(ROUTER_ROWS, D_MODEL),
                  const(ROUTER_ROWS, 1),
                  const(tm, tm),
                  const(2 * SUBLANES, LANES)],
        out_specs=[tok(D_MODEL), tok(ROW_W),
                   pl.BlockSpec((None, SUBLANES, tm), lambda b, i: (b, 0, i)),
                   const(CLASS_ROWS, LANES)],
        out_shape=[jax.ShapeDtypeStruct((bsz, length, D_MODEL), _f32),
                   jax.ShapeDtypeStruct((bsz, length, ROW_W), _f32),
                   jax.ShapeDtypeStruct((bsz, SUBLANES, length), _f32),
                   jax.ShapeDtypeStruct((CLASS_ROWS, LANES), _f32)],
        compiler_params=_params("arbitrary", "arbitrary"),
        name="merge",
    )(x, o_na, gu, vn, g_a, sh_f, sc_f, ws_cat, bs_tab, out_gain, w_out, w_r, b_r, tri, sel)


def _issue_row_copies(count, make_copy):
    for j in range(count):
        make_copy(j).start(priority=j % 2)


def _dispatch_kernel(fill_ref, pos_ref, rows_ref, xs_ref, zero_ref, sem, fill_sem, *, td, tile):
    @pl.when(pl.program_id(0) == 0)
    def _():
        zero_ref[...] = jnp.zeros_like(zero_ref)

        def fill(k):
            start = pl.multiple_of(jnp.maximum(fill_ref[k], 0), tile)
            return pltpu.make_async_copy(zero_ref, xs_ref.at[pl.ds(start, tile)], fill_sem)

        for k in range(fill_ref.shape[0]):
            @pl.when(fill_ref[k] >= 0)
            def _():
                fill(k).start()
        for k in range(fill_ref.shape[0]):
            @pl.when(fill_ref[k] >= 0)
            def _():
                fill(k).wait()

    _issue_row_copies(td, lambda j: pltpu.make_async_copy(
        rows_ref.at[pl.ds(j, 1)], xs_ref.at[pl.ds(pos_ref[0, j], 1)], sem))
    pltpu.make_async_copy(rows_ref, xs_ref.at[pl.ds(0, td)], sem).wait()


def _dispatch(rows, pos, fill, padded, td, tile):
    n = rows.shape[0]
    grid_spec = pltpu.PrefetchScalarGridSpec(
        num_scalar_prefetch=1,
        grid=(n // td,),
        in_specs=[pl.BlockSpec((None, 1, td), lambda i, fl: (i, 0, 0), memory_space=pltpu.SMEM),
                  pl.BlockSpec((td, ROW_W), lambda i, fl: (i, 0))],
        out_specs=pl.BlockSpec(memory_space=pl.ANY),
        scratch_shapes=[pltpu.VMEM((tile, ROW_W), _f32), pltpu.SemaphoreType.DMA(()), pltpu.SemaphoreType.DMA(())],
    )
    return pl.pallas_call(
        functools.partial(_dispatch_kernel, td=td, tile=tile),
        grid_spec=grid_spec,
        out_shape=jax.ShapeDtypeStruct((padded, ROW_W), _f32),
        compiler_params=_params("arbitrary"),
        name="dispatch",
    )(fill, pos.reshape(n // td, 1, td), rows)


def _combine_kernel(pos_ref, npos_ref, x_ref, g_ref, ys_ref, o_ref, buf0, buf1, sems, *, tc, nsteps):
    s = pl.program_id(0) * pl.num_programs(1) + pl.program_id(1)
    bufs = (buf0, buf1)

    def request(idx_ref, slot):
        _issue_row_copies(tc, lambda j: pltpu.make_async_copy(
            ys_ref.at[pl.ds(idx_ref[0, j], 1)], bufs[slot].at[pl.ds(j, 1)], sems.at[slot]))

    @pl.when(s == 0)
    def _():
        request(pos_ref, 0)

    for slot in range(2):
        @pl.when((s % 2 == slot) & (s + 1 < nsteps))
        def _():
            request(npos_ref, 1 - slot)

        @pl.when(s % 2 == slot)
        def _():
            pltpu.make_async_copy(ys_ref.at[pl.ds(0, tc)], bufs[slot], sems.at[slot]).wait()
            o_ref[...] = x_ref[...] + g_ref[...] * bufs[slot][...]


def _combine(x, ys, pos, g, tc):
    bsz, length, _ = x.shape
    per_batch = length // tc
    nsteps = bsz * per_batch
    tok = pl.BlockSpec((None, tc, D_MODEL), lambda b, i: (b, i, 0))
    pos3 = pos.reshape(nsteps, 1, tc)
    return pl.pallas_call(
        functools.partial(_combine_kernel, tc=tc, nsteps=nsteps),
        grid=(bsz, per_batch),
        in_specs=[pl.BlockSpec((None, 1, tc), lambda b, i: (b * per_batch + i, 0, 0), memory_space=pltpu.SMEM),
                  pl.BlockSpec((None, 1, tc), lambda b, i: (jnp.minimum(b * per_batch + i + 1, nsteps - 1), 0, 0),
                               memory_space=pltpu.SMEM),
                  tok,
                  pl.BlockSpec((None, 1, D_MODEL), lambda b, i: (b, 0, 0)),
                  pl.BlockSpec(memory_space=pl.ANY)],
        out_specs=tok,
        out_shape=jax.ShapeDtypeStruct(x.shape, _f32),
        scratch_shapes=[pltpu.VMEM((tc, D_MODEL), _f32), pltpu.VMEM((tc, D_MODEL), _f32),
                        pltpu.SemaphoreType.DMA((2,))],
        compiler_params=_params("arbitrary", "arbitrary"),
        name="combine",
    )(pos3, pos3, x, g, ys)


def _experts_kernel(ea_ref, eb_ref, chg_ref, nlive_ref, xs_ref, w1_ref, w3_ref, w2_ref, y_ref,
                    b1, b3, b2, s1, s3, s2, sems, *, layer):
    t = pl.program_id(0)
    live = t < nlive_ref[0]
    first = (ea_ref[t] // EXPERTS_PER_GROUP) * EXPERTS_PER_GROUP

    def stage(k, slot):
        return [pltpu.make_async_copy(w.at[layer, first + k], s.at[slot], sems.at[slot])
                for w, s in ((w1_ref, s1), (w3_ref, s3), (w2_ref, s2))]

    @pl.when(live & ((chg_ref[t] & 1) != 0))
    def _():
        for k in range(2):
            for cp in stage(k, k):
                cp.start()
        for k in range(EXPERTS_PER_GROUP):
            slot = k % 2
            for cp in stage(k, slot):
                cp.wait()
            b1[k] = s1[slot].astype(_bf16)
            b3[k] = s3[slot].astype(_bf16)
            b2[k] = s2[slot].astype(_bf16)
            if k + 2 < EXPERTS_PER_GROUP:
                for cp in stage(k + 2, slot):
                    cp.start()

    def run(nrows):
        xs = xs_ref[:nrows, :D_MODEL].astype(_bf16)
        gates = xs_ref[:nrows, D_MODEL:]

        def expert(e):
            hid = jax.nn.silu(_dot(xs, b1[e])) * _dot(xs, b3[e])
            return _dot(hid.astype(_bf16), b2[e])

        y_ref[:nrows, :] = (gates[:, 0:1] * expert(ea_ref[t] - first) + gates[:, 1:2] * expert(eb_ref[t] - first))

    half_only = (chg_ref[t] & 4) != 0
    half = y_ref.shape[0] // 2

    @pl.when(live & jnp.logical_not(half_only))
    def _():
        run(y_ref.shape[0])

    @pl.when(live & half_only)
    def _():
        run(half)
        y_ref[half:, :] = jnp.zeros((half, D_MODEL), _f32)

    @pl.when(jnp.logical_not(live))
    def _():
        y_ref[...] = jnp.zeros_like(y_ref)


def _experts(xs, tile_ea, tile_eb, chg, nlive, w1, w3, w2, layer, tile):
    ntiles = xs.shape[0] // tile
    hbm = pl.BlockSpec(memory_space=pl.ANY)
    up, down = (D_MODEL, D_EXPERT), (D_EXPERT, D_MODEL)
    grid_spec = pltpu.PrefetchScalarGridSpec(
        num_scalar_prefetch=4,
        grid=(ntiles,),
        in_specs=[pl.BlockSpec((tile, ROW_W), lambda t, ea, eb, cg, nl: (t, 0)), hbm, hbm, hbm],
        out_specs=pl.BlockSpec((tile, D_MODEL), lambda t, ea, eb, cg, nl: (t, 0)),
        scratch_shapes=[pltpu.VMEM((EXPERTS_PER_GROUP,) + up, _bf16), pltpu.VMEM((EXPERTS_PER_GROUP,) + up, _bf16),
                        pltpu.VMEM((EXPERTS_PER_GROUP,) + down, _bf16),
                        pltpu.VMEM((2,) + up, _f32), pltpu.VMEM((2,) + up, _f32), pltpu.VMEM((2,) + down, _f32),
                        pltpu.SemaphoreType.DMA((2,))],
    )
    return pl.pallas_call(
        functools.partial(_experts_kernel, layer=layer),
        grid_spec=grid_spec,
        out_shape=jax.ShapeDtypeStruct((xs.shape[0], D_MODEL), _f32),
        compiler_params=_params("arbitrary"),
        name="experts",
    )(tile_ea, tile_eb, chg, nlive, xs, w1, w3, w2)


def _dispatch_plan(cls, rank, counts, tile):
    n = cls.shape[0]
    padded = -(-(n + N_CLASSES * (tile - 1)) // tile) * tile
    ntiles = padded // tile
    pcounts = (counts + tile - 1) // tile * tile
    pad_end = jnp.cumsum(pcounts)
    pad_off = pad_end - pcounts
    pos = pad_off[cls] + rank
    nlive = pad_end[-1] // tile
    tile_start = jnp.arange(ntiles, dtype=jnp.int32) * tile
    tile_cls = jnp.sum((pad_end[None, :] <= jnp.minimum(tile_start, pad_end[-1] - 1)[:, None]).astype(jnp.int32), axis=1)
    tile_cls = jnp.minimum(tile_cls, N_CLASSES - 1)
    grp = tile_cls // N_PAIRS
    pair = tile_cls % N_PAIRS
    tile_ea = grp * EXPERTS_PER_GROUP + jnp.asarray(PAIR_A, jnp.int32)[pair]
    tile_eb = grp * EXPERTS_PER_GROUP + jnp.asarray(PAIR_B, jnp.int32)[pair]
    new_group = jnp.concatenate([jnp.ones((1,), jnp.int32), (grp[1:] != grp[:-1]).astype(jnp.int32)])
    tokens_in_tile = (pad_off + counts)[tile_cls] - tile_start
    chg = new_group + 4 * (tokens_in_tile <= tile // 2).astype(jnp.int32)
    class_tail = jnp.where(pcounts > 0, pad_end - tile, -1)
    past = (nlive + jnp.arange(N_CLASSES, dtype=jnp.int32)) * tile
    fill = jnp.concatenate([class_tail, jnp.where(past < padded, past, -1)]).astype(jnp.int32)
    return pos.astype(jnp.int32), padded, tile_ea, tile_eb, chg, nlive.astype(jnp.int32).reshape(1), fill


def _ffn(x_new, rows, meta, counts, g_f, w1, w3, w2, layer, tile, tdma):
    bsz, length, _ = x_new.shape
    n = bsz * length
    cls = meta[:, 0, :].reshape(n).astype(jnp.int32)
    rank = meta[:, 1, :].reshape(n).astype(jnp.int32)
    cnt = counts[:N_CLASSES, 0].astype(jnp.int32)
    pos, padded, tile_ea, tile_eb, chg, nlive, fill = _dispatch_plan(cls, rank, cnt, tile)
    xs = _dispatch(rows.reshape(n, ROW_W), pos, fill, padded, tdma, tile)
    ys = _experts(xs, tile_ea, tile_eb, chg, nlive, w1, w3, w2, layer, tile)
    return _combine(x_new, ys, pos, g_f, tdma)


def kernel(x, c, ctx, c_ctx, w_ada, b_ada, w_in, q_gain, k_gain, rpb, sgu_ln, sgu_w, sgu_b, out_gain,
           w_out, rg_w, rg_b, re_w, re_b, w1, w3, w2):
    bsz, seq, _ = x.shape
    depth = w_ada.shape[0]
    rows = seq // GRID_W
    assert seq % Q_BLOCK == 0 and rows // Q_ROWS >= 3 and ctx.shape[1] % SGU_CHUNK == 0
    tm_x, tm_c = 512, ctx.shape[1]

    cond = jnp.zeros((SUBLANES, D_MODEL), _f32).at[:bsz].set(c).at[bsz].set(c_ctx)
    mods = _adaln(cond, w_ada, b_ada).reshape(depth, SUBLANES, N_MOD, D_MODEL)

    assert MXU_TILE % NA_HEAD_DIM == 0 and NA_W % MXU_TILE == 0
    head_id = np.arange(MXU_TILE) // NA_HEAD_DIM
    avg = jnp.asarray((head_id[:, None] == head_id[None, :]).astype(np.float32) / NA_HEAD_DIM, _bf16)
    win_plan = _window_plan(rows)
    tri = {tm: jnp.asarray(np.triu(np.ones((tm, tm), np.float32), 1), _bf16) for tm in {tm_x, tm_c}}
    sel_np = np.zeros((2 * SUBLANES, LANES), np.float32)
    for n in range(2):
        sel_np[n * GATE_PIECES:(n + 1) * GATE_PIECES, n] = 1.0
    sel = jnp.asarray(sel_np, _bf16)

    for i in range(depth):
        last = i == depth - 1
        mx = [mods[i, :bsz, j][:, None, :] for j in range(N_MOD)]
        mc = [jnp.broadcast_to(mods[i, bsz, j][None, None, :], (bsz, 1, D_MODEL)) for j in range(N_MOD)]
        w_in_b = w_in[i].astype(_bf16)
        qg = (jnp.tile(q_gain[i], NA_HEADS) * NA_HEAD_DIM ** -0.5)[None, :]
        kg = jnp.tile(k_gain[i], NA_HEADS)[None, :]
        ln_g = sgu_ln[i][None, :]
        ws_cat = jnp.concatenate([sgu_w[i, 0::2], sgu_w[i, 1::2]], axis=2).astype(_bf16)
        bs_tab = jnp.repeat(sgu_b[i].T, SGU_GROUP_DIM, axis=1)
        og = out_gain[i][None, :]
        w_out_b = w_out[i].astype(_bf16)
        pad_w = jnp.zeros((SUBLANES - EXPERTS_PER_GROUP, D_MODEL), _f32)
        pad_b = jnp.zeros((SUBLANES - EXPERTS_PER_GROUP,), _f32)
        w_r = jnp.concatenate([rg_w[i].T, pad_w] + [t for g in range(N_GROUPS) for t in (re_w[i, g].T, pad_w)]
                              + [jnp.zeros((ROUTER_ROWS - SUBLANES * (N_GROUPS + 1), D_MODEL), _f32)]).astype(_bf16)
        b_r = jnp.concatenate([rg_b[i], pad_b] + [t for g in range(N_GROUPS) for t in (re_b[i, g], pad_b)]
                              + [jnp.zeros((ROUTER_ROWS - SUBLANES * (N_GROUPS + 1),), _f32)])[:, None]
        bycol = _window_col_table(rpb[i])

        q, k, v, gu, vn = _proj(x, mx[0], mx[1], w_in_b, avg, qg, kg, ln_g, tm_x)
        qc, kc, vc, guc, vnc = _proj(ctx, mc[0], mc[1], w_in_b, avg, qg, kg, ln_g, tm_c)
        o_na = _attention(q, k, v, kc, vc, bycol, win_plan)
        x_new, xrows, meta, counts = _merge(x, o_na, gu, vn, mx[2], mx[3], mx[4], ws_cat, bs_tab, og,
                                            w_out_b, w_r, b_r, tri[tm_x], sel, tm_x)
        x = _ffn(x_new, xrows, meta, counts, mx[5], w1, w3, w2, i, 512, 1024)
        if not last:
            oc_na = _ctx_attention(qc, kc, vc)
            c_new, crows, cmeta, ccounts = _merge(ctx, oc_na, guc, vnc, mc[2], mc[3], mc[4], ws_cat, bs_tab,
                                                  og, w_out_b, w_r, b_r, tri[tm_c], sel, tm_c)
            ctx = _ffn(c_new, crows, cmeta, ccounts, mc[5], w1, w3, w2, i, 64, tm_c)
    return x
```

```python
import functools

import jax
import jax.numpy as jnp
import numpy as np
from jax import lax
from jax.experimental import pallas as pl
from jax.experimental.pallas import tpu as pltpu

D_MODEL = 1024
GRID_W = 64
NA_HEADS = 8
NA_HEAD_DIM = 64
NA_W = NA_HEADS * NA_HEAD_DIM
NA_KH = 8
NA_KW = 16
SGU_GROUPS = 8
SGU_W = D_MODEL - NA_W
SGU_GROUP_DIM = SGU_W // SGU_GROUPS
SGU_CHUNK = 128
IN_COLS = 3 * NA_W + 2 * SGU_W
N_GROUPS = 4
EXPERTS_PER_GROUP = 4
N_EXPERTS = N_GROUPS * EXPERTS_PER_GROUP
D_EXPERT = D_MODEL // 2
N_MOD = 6
EPS = 1e-6

LANES = 128
SUBLANES = 8
HEAD_PAIRS = NA_W // LANES
Q_ROWS = 8
Q_BLOCK = Q_ROWS * GRID_W
WIN_ROWS = Q_ROWS + NA_KH - 1
WIN_BLOCK = WIN_ROWS * GRID_W
PAIR_ROWS = 2
PAIR_BLOCK = PAIR_ROWS * GRID_W
PAIR_WIN_ROWS = PAIR_ROWS + NA_KH - 1
PAIR_WIN = PAIR_WIN_ROWS * GRID_W
PAIRS_IN_FLIGHT = 2
N_DR = 2 * NA_KH
N_PAIRS = 6
N_CLASSES = N_GROUPS * N_PAIRS
PAIR_A = (0, 0, 1, 1, 0, 2)
PAIR_B = (1, 2, 2, 3, 3, 3)
PAIR_OF_LEX = (0, 1, 4, 2, 3, 5)
CLASS_ROWS = 32
ROUTER_ROWS = 48
GATE_PIECES = 3
ROW_W = D_MODEL + LANES
MXU_TILE = 256
MASKED = -1e30
VMEM_LIMIT_BYTES = 56 * 1024 * 1024

_f32 = jnp.float32
_bf16 = jnp.bfloat16


def _params(*semantics):
    return pltpu.CompilerParams(dimension_semantics=semantics, vmem_limit_bytes=VMEM_LIMIT_BYTES)


def _dot(a, b):
    return jnp.dot(a, b, preferred_element_type=_f32)


def _dot_nt(a, b):
    return lax.dot_general(a, b, (((1,), (1,)), ((), ())), preferred_element_type=_f32)


def _adaln_kernel(cond_ref, w_ref, b_ref, out_ref):
    cond = cond_ref[...]
    act = cond * jax.nn.sigmoid(cond)
    out_ref[...] = _dot(act.astype(_bf16), w_ref[...].astype(_bf16)) + b_ref[...]


def _adaln(cond, w_ada, b_ada):
    depth = w_ada.shape[0]
    cols = N_MOD * D_MODEL
    col_block = 1536
    return pl.pallas_call(
        _adaln_kernel,
        grid=(depth, cols // col_block),
        in_specs=[
            pl.BlockSpec((SUBLANES, D_MODEL), lambda d, j: (0, 0)),
            pl.BlockSpec((None, D_MODEL, col_block), lambda d, j: (d, 0, j)),
            pl.BlockSpec((None, 1, col_block), lambda d, j: (d, 0, j)),
        ],
        out_specs=pl.BlockSpec((None, SUBLANES, col_block), lambda d, j: (d, 0, j)),
        out_shape=jax.ShapeDtypeStruct((depth, SUBLANES, cols), _f32),
        compiler_params=_params("arbitrary", "arbitrary"),
        name="adaln",
    )(cond, w_ada, b_ada.reshape(depth, 1, cols))


def _proj_kernel(x_ref, sh_ref, sc_ref, w_ref, avg_ref, qg_ref, kg_ref, ln_ref,
                 q_ref, k_ref, v_ref, gu_ref, vn_ref):
    x = x_ref[...]
    h = x * lax.rsqrt(jnp.mean(x * x, -1, keepdims=True) + EPS) * (1.0 + sc_ref[...]) + sh_ref[...]
    p = _dot(h.astype(_bf16), w_ref[...])
    q = p[:, :NA_W]
    k = p[:, NA_W:2 * NA_W]
    v = p[:, 2 * NA_W:3 * NA_W]
    u = p[:, 3 * NA_W:3 * NA_W + SGU_W]
    vs = p[:, 3 * NA_W + SGU_W:]
    def head_mean(t):
        sq = (t * t).astype(_bf16)
        return jnp.concatenate([_dot(sq[:, c:c + MXU_TILE], avg_ref[...]) for c in range(0, NA_W, MXU_TILE)], axis=1)

    q_ms = head_mean(q)
    k_ms = head_mean(k)
    qn = (q * lax.rsqrt(q_ms + EPS) * qg_ref[...]).astype(_bf16)
    kn = (k * lax.rsqrt(k_ms + EPS) * kg_ref[...]).astype(_bf16)
    vb = v.astype(_bf16)
    for hp in range(HEAD_PAIRS):
        cols = slice(hp * LANES, (hp + 1) * LANES)
        q_ref[hp] = qn[:, cols]
        k_ref[hp] = kn[:, cols]
        v_ref[hp] = vb[:, cols]
    gu_ref[...] = jax.nn.gelu(u).astype(_bf16)
    gv = jax.nn.gelu(vs)
    mu = jnp.mean(gv, -1, keepdims=True)
    cen = gv - mu
    var = jnp.mean(cen * cen, -1, keepdims=True)
    vn_ref[...] = (cen * lax.rsqrt(var + EPS) * ln_ref[...]).astype(_bf16)


def _proj(x, sh, sc, w_in, avg, qg, kg, ln_g, tm):
    bsz, length, _ = x.shape
    vec = lambda n: pl.BlockSpec((1, n), lambda b, i: (0, 0))
    mod = pl.BlockSpec((None, 1, D_MODEL), lambda b, i: (b, 0, 0))
    tok = lambda n: pl.BlockSpec((None, tm, n), lambda b, i: (b, i, 0))
    hpm = pl.BlockSpec((None, HEAD_PAIRS, tm, LANES), lambda b, i: (b, 0, i, 0))
    hpm_shape = jax.ShapeDtypeStruct((bsz, HEAD_PAIRS, length, LANES), _bf16)
    sgu_shape = jax.ShapeDtypeStruct((bsz, length, SGU_W), _bf16)
    return pl.pallas_call(
        _proj_kernel,
        grid=(bsz, length // tm),
        in_specs=[tok(D_MODEL), mod, mod,
                  pl.BlockSpec((D_MODEL, IN_COLS), lambda b, i: (0, 0)),
                  pl.BlockSpec((MXU_TILE, MXU_TILE), lambda b, i: (0, 0)),
                  vec(NA_W), vec(NA_W), vec(SGU_W)],
        out_specs=[hpm, hpm, hpm, tok(SGU_W), tok(SGU_W)],
        out_shape=[hpm_shape, hpm_shape, hpm_shape, sgu_shape, sgu_shape],
        compiler_params=_params("arbitrary", "arbitrary"),
        name="proj",
    )(x, sh, sc, w_in, avg, qg, kg, ln_g)


def _attend(blocks):
    add = lambda a, b: a + b
    m_rows = blocks[0][0].shape[0]
    lane = lax.broadcasted_iota(jnp.int32, blocks[0][0].shape, 1)
    low = lane < NA_HEAD_DIM
    stacked = [jnp.concatenate([jnp.where(low, q, jnp.zeros_like(q)), jnp.where(low, jnp.zeros_like(q), q)], axis=0)
               for q, _, _, _ in blocks]
    scores = [[_dot_nt(qs, key) if bias is None else _dot_nt(qs, key) + bias[...]
               for key, bias in zip(keys, biases)] for qs, (_, keys, _, biases) in zip(stacked, blocks)]
    maxes = [functools.reduce(jnp.maximum, [jnp.max(s, -1, keepdims=True) for s in ss]) for ss in scores]
    probs = [[jnp.exp(s - m) for s in ss] for ss, m in zip(scores, maxes)]
    dens = [functools.reduce(add, [jnp.sum(p, -1, keepdims=True) for p in ps]) for ps in probs]
    accs = [functools.reduce(add, [_dot(p.astype(_bf16), val) for p, val in zip(ps, blk[2])])
            for ps, blk in zip(probs, blocks)]
    outs = [acc / den for acc, den in zip(accs, dens)]
    return [jnp.where(low, o2[:m_rows], o2[m_rows:]) for o2 in outs]


def _attn_kernel(off_ref, var_ref, q_ref, k_ref, v_ref, kc_ref, vc_ref, bycol_ref, o_ref, bias_ref, *, pieces):
    i = pl.program_id(2)

    @pl.when(i == 0)
    def _():
        for var, piece in enumerate(pieces):
            for hh in range(2):
                for j in range(PAIR_ROWS):
                    for kr in range(PAIR_WIN_ROWS):
                        lo = (kr % 2) * GRID_W
                        row = hh * PAIR_BLOCK + j * GRID_W
                        bias_ref[var, row:row + GRID_W, kr * GRID_W:(kr + 1) * GRID_W] = (
                            bycol_ref[hh, piece[j][kr], :, lo:lo + GRID_W])

    kc = kc_ref[...]
    vc = vc_ref[...]
    npair = Q_ROWS // PAIR_ROWS
    for g0 in range(0, npair, PAIRS_IN_FLIGHT):
        blocks = []
        for g in range(g0, g0 + PAIRS_IN_FLIGHT):
            start = pl.multiple_of(off_ref[i * npair + g] * GRID_W, GRID_W)
            var = var_ref[i * npair + g]
            blocks.append((q_ref[g * PAIR_BLOCK:(g + 1) * PAIR_BLOCK, :],
                           [k_ref[pl.ds(start, PAIR_WIN), :], kc],
                           [v_ref[pl.ds(start, PAIR_WIN), :], vc],
                           [bias_ref.at[var], None]))
        for g, out in zip(range(g0, g0 + PAIRS_IN_FLIGHT), _attend(blocks)):
            o_ref[g * PAIR_BLOCK:(g + 1) * PAIR_BLOCK, :] = out.astype(o_ref.dtype)


def _attention(q, k, v, kc, vc, bycol, plan):
    off_tab, var_tab, pieces = plan
    bsz, _, seq, _ = q.shape
    clen = kc.shape[2]
    nblk = seq // Q_BLOCK

    def win_start(i):
        return jnp.clip(i * Q_ROWS - NA_KH // 2, 0, seq // GRID_W - WIN_ROWS) * GRID_W

    qspec = pl.BlockSpec((None, None, Q_BLOCK, LANES), lambda hp, b, i, *_: (b, hp, i, 0))
    wspec = pl.BlockSpec((None, None, pl.Element(WIN_BLOCK), pl.Element(LANES)),
                         lambda hp, b, i, *_: (b, hp, win_start(i), 0))
    cspec = pl.BlockSpec((None, None, clen, LANES), lambda hp, b, i, *_: (b, hp, 0, 0))
    tspec = pl.BlockSpec((None, 2, N_DR, GRID_W, LANES), lambda hp, b, i, *_: (hp, 0, 0, 0, 0))
    grid_spec = pltpu.PrefetchScalarGridSpec(
        num_scalar_prefetch=2,
        grid=(HEAD_PAIRS, bsz, nblk),
        in_specs=[qspec, wspec, wspec, cspec, cspec, tspec],
        out_specs=qspec,
        scratch_shapes=[pltpu.VMEM((len(pieces), 2 * PAIR_BLOCK, PAIR_WIN), _f32)],
    )
    return pl.pallas_call(
        functools.partial(_attn_kernel, pieces=pieces),
        grid_spec=grid_spec,
        out_shape=jax.ShapeDtypeStruct(q.shape, _bf16),
        compiler_params=_params("arbitrary", "arbitrary", "arbitrary"),
        name="attention",
    )(off_tab, var_tab, q, k, v, kc, vc, bycol)


def _ctx_attn_kernel(q_ref, k_ref, v_ref, o_ref):
    o_ref[...] = _attend([(q_ref[...], [k_ref[...]], [v_ref[...]], [None])])[0].astype(o_ref.dtype)


def _ctx_attention(q, k, v):
    bsz, _, clen, _ = q.shape
    spec = pl.BlockSpec((None, None, clen, LANES), lambda b, hp: (b, hp, 0, 0))
    return pl.pallas_call(
        _ctx_attn_kernel,
        grid=(bsz, HEAD_PAIRS),
        in_specs=[spec, spec, spec],
        out_specs=spec,
        out_shape=jax.ShapeDtypeStruct(q.shape, _bf16),
        compiler_params=_params("arbitrary", "arbitrary"),
        name="ctx_attention",
    )(q, k, v)


def _window_plan(rows):
    patterns, off_tab, var_tab = {}, [], []
    for r in range(0, rows, PAIR_ROWS):
        ws = int(np.clip(r - NA_KH // 2, 0, rows - PAIR_WIN_ROWS))
        r0 = r // Q_ROWS * Q_ROWS
        block_ws = int(np.clip(r0 - NA_KH // 2, 0, rows - WIN_ROWS))
        assert 0 <= ws - block_ws <= WIN_ROWS - PAIR_WIN_ROWS
        piece = []
        for j in range(PAIR_ROWS):
            r_start = int(np.clip(r + j - NA_KH // 2, 0, rows - NA_KH))
            assert ws <= r_start and r_start + NA_KH <= ws + PAIR_WIN_ROWS
            piece.append(tuple(ws + kr - (r + j) + NA_KH - 1 if r_start <= ws + kr < r_start + NA_KH else N_DR - 1
                               for kr in range(PAIR_WIN_ROWS)))
        var_tab.append(patterns.setdefault(tuple(piece), len(patterns)))
        off_tab.append(ws - block_ws)
    return jnp.asarray(off_tab, jnp.int32), jnp.asarray(var_tab, jnp.int32), tuple(patterns)


def _window_col_table(rpb):
    col = np.arange(GRID_W)
    col_start = np.clip(col - NA_KW // 2, 0, GRID_W - NA_KW)
    col_ok = (col[None, :] >= col_start[:, None]) & (col[None, :] < col_start[:, None] + NA_KW)
    dc = np.where(col_ok, col[None, :] - col[:, None] + NA_KW - 1, 0)
    onehot = np.zeros((2 * NA_KW - 1, GRID_W * GRID_W), np.float32)
    onehot[dc.reshape(-1), np.arange(GRID_W * GRID_W)] = col_ok.reshape(-1)
    by_col = jnp.dot(rpb.reshape(-1, 2 * NA_KW - 1), jnp.asarray(onehot), precision=lax.Precision.HIGHEST)
    by_col = by_col.reshape(NA_HEADS, 2 * NA_KH - 1, GRID_W, GRID_W)
    by_col = jnp.where(jnp.asarray(col_ok)[None, None], by_col, MASKED)
    by_col = jnp.concatenate([by_col, jnp.full((NA_HEADS, 1, GRID_W, GRID_W), MASKED, _f32)], axis=1)
    by_col = jnp.concatenate([by_col, by_col], axis=-1)
    return by_col.reshape(HEAD_PAIRS, 2, N_DR, GRID_W, LANES)


def _merge_kernel(x_ref, o_ref, gu_ref, vn_ref, ga_ref, shf_ref, scf_ref, ws_ref, bs_ref, og_ref,
                  wout_ref, wr_ref, br_ref, tri_ref, sel_ref, xn_ref, row_ref, meta_ref, cnt_ref, *, tm):
    first = (pl.program_id(0) == 0) & (pl.program_id(1) == 0)

    @pl.when(first)
    def _():
        cnt_ref[...] = jnp.zeros_like(cnt_ref)

    nchunk = tm // SGU_CHUNK
    vn = vn_ref[...]
    lane = lax.broadcasted_iota(jnp.int32, (SGU_CHUNK, LANES), 1)
    low = lane < SGU_GROUP_DIM
    mixed_cols = []
    for gp in range(SGU_W // LANES):
        rhs_cols = []
        for n in range(nchunk):
            blk = vn[n * SGU_CHUNK:(n + 1) * SGU_CHUNK, gp * LANES:(gp + 1) * LANES]
            zero = jnp.zeros_like(blk)
            rhs_cols.append(jnp.concatenate([jnp.where(low, blk, zero), jnp.where(low, zero, blk)], axis=0))
        rhs = jnp.concatenate(rhs_cols, axis=1)
        mix = _dot(ws_ref[gp], rhs)
        mixed_cols.append(jnp.concatenate([mix[:, n * LANES:(n + 1) * LANES] for n in range(nchunk)], axis=0))
    mixed = jnp.concatenate(mixed_cols, axis=1)
    bias = jnp.concatenate([bs_ref[...]] * nchunk, axis=0)
    o_sg = gu_ref[...].astype(_f32) * (mixed + bias)
    o_na = jnp.concatenate([o_ref[hp].astype(_f32) for hp in range(HEAD_PAIRS)], axis=1)

    def rms(t):
        return t * lax.rsqrt(jnp.mean(t * t, -1, keepdims=True) + EPS)

    y = jnp.concatenate([rms(o_na), rms(o_sg)], axis=1) * og_ref[...]
    xn = x_ref[...] + ga_ref[...] * _dot(y.astype(_bf16), wout_ref[...])
    xn_ref[...] = xn
    hf = rms(xn) * (1.0 + scf_ref[...]) + shf_ref[...]
    row_ref[:, :D_MODEL] = hf

    lg = _dot_nt(wr_ref[...], hf.astype(_bf16)) + br_ref[...]
    quad = lambda r: lg[r:r + EXPERTS_PER_GROUP, :]
    row4 = lax.broadcasted_iota(jnp.int32, (EXPERTS_PER_GROUP, tm), 0).astype(_f32)
    over = lambda fn, t: fn(t, 0, keepdims=True)
    gl = quad(0)
    gm = over(jnp.max, gl)
    p_top = 1.0 / over(jnp.sum, jnp.exp(gl - gm))
    g_idx = over(jnp.min, jnp.where(gl == gm, row4, float(N_GROUPS)))
    el = quad(SUBLANES * N_GROUPS)
    for g in range(N_GROUPS - 2, -1, -1):
        el = jnp.where(g_idx == float(g), quad(SUBLANES * (g + 1)), el)
    ee = jnp.exp(el - over(jnp.max, el))
    pe = ee / over(jnp.sum, ee)
    v1 = over(jnp.max, pe)
    i1 = over(jnp.min, jnp.where(pe == v1, row4, float(EXPERTS_PER_GROUP)))
    pe2 = jnp.where(row4 == i1, -1.0, pe)
    v2 = over(jnp.max, pe2)
    i2 = over(jnp.min, jnp.where(pe2 == v2, row4, float(EXPERTS_PER_GROUP)))
    den = v1 + v2
    w1 = v1 / den * p_top
    w2 = v2 / den * p_top
    first_low = i1 < i2
    a = jnp.minimum(i1, i2)
    b = jnp.maximum(i1, i2)
    lex = a * (7.0 - a) * 0.5 + (b - a - 1.0)
    pair = lex
    for src, dst in enumerate(PAIR_OF_LEX):
        if src != dst:
            pair = jnp.where(lex == float(src), float(dst), pair)
    cls = g_idx * float(N_PAIRS) + pair
    gate_a = jnp.where(first_low, w1, w2)
    gate_b = jnp.where(first_low, w2, w1)

    row_p = lax.broadcasted_iota(jnp.int32, (2 * SUBLANES, tm), 0)
    stacked = jnp.zeros((2 * SUBLANES, tm), _f32)
    for n, gate in enumerate((gate_a, gate_b)):
        rest = gate
        for k in range(GATE_PIECES):
            piece = rest.astype(_bf16).astype(_f32)
            stacked = jnp.where(row_p == n * GATE_PIECES + k, piece, stacked)
            rest = rest - piece
    row_ref[:, D_MODEL:] = lax.dot_general(stacked.astype(_bf16), sel_ref[...],
                                           (((0,), (0,)), ((), ())), preferred_element_type=_f32)

    row_c = lax.broadcasted_iota(jnp.int32, (CLASS_ROWS, tm), 0).astype(_f32)
    onehot = row_c == cls
    ones = jnp.where(onehot, 1.0, 0.0)
    before = _dot(ones.astype(_bf16), tri_ref[...])
    rank = over(jnp.sum, jnp.where(onehot, before + cnt_ref[:, 0:1], 0.0))
    cnt_ref[...] += jnp.sum(ones, 1, keepdims=True)
    meta_ref[...] = jnp.concatenate([cls, rank, jnp.zeros((SUBLANES - 2, tm), _f32)], axis=0)


def _merge(x, o_na, gu, vn, g_a, sh_f, sc_f, ws_cat, bs_tab, out_gain, w_out, w_r, b_r, tri, sel, tm):
    bsz, length, _ = x.shape
    mod = pl.BlockSpec((None, 1, D_MODEL), lambda b, i: (b, 0, 0))
    tok = lambda n: pl.BlockSpec((None, tm, n), lambda b, i: (b, i, 0))
    const = lambda *shape: pl.BlockSpec(shape, lambda b, i: (0,) * len(shape))
    return pl.pallas_call(
        functools.partial(_merge_kernel, tm=tm),
        grid=(bsz, length // tm),
        in_specs=[tok(D_MODEL),
                  pl.BlockSpec((None, HEAD_PAIRS, tm, LANES), lambda b, i: (b, 0, i, 0)),
                  tok(SGU_W), tok(SGU_W), mod, mod, mod,
                  const(SGU_W // LANES, SGU_CHUNK, 2 * SGU_CHUNK),
                  const(SGU_CHUNK, SGU_W),
                  const(1, D_MODEL),
                  const(D_MODEL, D_MODEL),
                  const(ROUTER_ROWS, D_MODEL),
                  const(ROUTER_ROWS, 1),
                  const(tm, tm),
                  const(2 * SUBLANES, LANES)],
        out_specs=[tok(D_MODEL), tok(ROW_W),
                   pl.BlockSpec((None, SUBLANES, tm), lambda b, i: (b, 0, i)),
                   const(CLASS_ROWS, LANES)],
        out_shape=[jax.ShapeDtypeStruct((bsz, length, D_MODEL), _f32),
                   jax.ShapeDtypeStruct((bsz, length, ROW_W), _f32),
                   jax.ShapeDtypeStruct((bsz, SUBLANES, length), _f32),
                   jax.ShapeDtypeStruct((CLASS_ROWS, LANES), _f32)],
        compiler_params=_params("arbitrary", "arbitrary"),
        name="merge",
    )(x, o_na, gu, vn, g_a, sh_f, sc_f, ws_cat, bs_tab, out_gain, w_out, w_r, b_r, tri, sel)


def _issue_row_copies(count, make_copy):
    for j in range(count):
        make_copy(j).start(priority=j % 2)


def _dispatch_kernel(fill_ref, pos_ref, rows_ref, xs_ref, zero_ref, sem, fill_sem, *, td, tile):
    @pl.when(pl.program_id(0) == 0)
    def _():
        zero_ref[...] = jnp.zeros_like(zero_ref)

        def fill(k):
            start = pl.multiple_of(jnp.maximum(fill_ref[k], 0), tile)
            return pltpu.make_async_copy(zero_ref, xs_ref.at[pl.ds(start, tile)], fill_sem)

        for k in range(fill_ref.shape[0]):
            @pl.when(fill_ref[k] >= 0)
            def _():
                fill(k).start()
        for k in range(fill_ref.shape[0]):
            @pl.when(fill_ref[k] >= 0)
            def _():
                fill(k).wait()

    _scatter_rows(pos_ref, rows_ref, xs_ref, sem, td)


def _scatter_rows(pos_ref, rows_ref, xs_ref, sem, td):
    _issue_row_copies(td, lambda j: pltpu.make_async_copy(
        rows_ref.at[pl.ds(j, 1)], xs_ref.at[pl.ds(pos_ref[0, j], 1)], sem))
    pltpu.make_async_copy(rows_ref, xs_ref.at[pl.ds(0, td)], sem).wait()


def _dispatch_more_kernel(pos_ref, rows_ref, base_ref, xs_ref, sem, *, td):
    del base_ref
    _scatter_rows(pos_ref, rows_ref, xs_ref, sem, td)


def _dispatch(rows, pos, fill, padded, td, tile):
    n = rows.shape[0]
    grid_spec = pltpu.PrefetchScalarGridSpec(
        num_scalar_prefetch=1,
        grid=(n // td,),
        in_specs=[pl.BlockSpec((None, 1, td), lambda i, fl: (i, 0, 0), memory_space=pltpu.SMEM),
                  pl.BlockSpec((td, ROW_W), lambda i, fl: (i, 0))],
        out_specs=pl.BlockSpec(memory_space=pl.ANY),
        scratch_shapes=[pltpu.VMEM((tile, ROW_W), _f32), pltpu.SemaphoreType.DMA(()), pltpu.SemaphoreType.DMA(())],
    )
    return pl.pallas_call(
        functools.partial(_dispatch_kernel, td=td, tile=tile),
        grid_spec=grid_spec,
        out_shape=jax.ShapeDtypeStruct((padded, ROW_W), _f32),
        compiler_params=_params("arbitrary"),
        name="dispatch",
    )(fill, pos.reshape(n // td, 1, td), rows)


def _dispatch_more(xs, rows, pos, td):
    n = rows.shape[0]
    return pl.pallas_call(
        functools.partial(_dispatch_more_kernel, td=td),
        grid=(n // td,),
        in_specs=[pl.BlockSpec((None, 1, td), lambda i: (i, 0, 0), memory_space=pltpu.SMEM),
                  pl.BlockSpec((td, ROW_W), lambda i: (i, 0)),
                  pl.BlockSpec(memory_space=pl.ANY)],
        out_specs=pl.BlockSpec(memory_space=pl.ANY),
        out_shape=jax.ShapeDtypeStruct(xs.shape, _f32),
        scratch_shapes=[pltpu.SemaphoreType.DMA(())],
        input_output_aliases={2: 0},
        compiler_params=_params("arbitrary"),
        name="dispatch_more",
    )(pos.reshape(n // td, 1, td), rows, xs)


def _combine_kernel(pos_ref, npos_ref, x_ref, g_ref, ys_ref, o_ref, buf0, buf1, sems, *, tc, nsteps):
    s = pl.program_id(0) * pl.num_programs(1) + pl.program_id(1)
    bufs = (buf0, buf1)

    def request(idx_ref, slot):
        _issue_row_copies(tc, lambda j: pltpu.make_async_copy(
            ys_ref.at[pl.ds(idx_ref[0, j], 1)], bufs[slot].at[pl.ds(j, 1)], sems.at[slot]))

    @pl.when(s == 0)
    def _():
        request(pos_ref, 0)

    for slot in range(2):
        @pl.when((s % 2 == slot) & (s + 1 < nsteps))
        def _():
            request(npos_ref, 1 - slot)

        @pl.when(s % 2 == slot)
        def _():
            pltpu.make_async_copy(ys_ref.at[pl.ds(0, tc)], bufs[slot], sems.at[slot]).wait()
            o_ref[...] = x_ref[...] + g_ref[...] * bufs[slot][...]


def _combine(x, ys, pos, g, tc):
    bsz, length, _ = x.shape
    per_batch = length // tc
    nsteps = bsz * per_batch
    tok = pl.BlockSpec((None, tc, D_MODEL), lambda b, i: (b, i, 0))
    pos3 = pos.reshape(nsteps, 1, tc)
    return pl.pallas_call(
        functools.partial(_combine_kernel, tc=tc, nsteps=nsteps),
        grid=(bsz, per_batch),
        in_specs=[pl.BlockSpec((None, 1, tc), lambda b, i: (b * per_batch + i, 0, 0), memory_space=pltpu.SMEM),
                  pl.BlockSpec((None, 1, tc), lambda b, i: (jnp.minimum(b * per_batch + i + 1, nsteps - 1), 0, 0),
                               memory_space=pltpu.SMEM),
                  tok,
                  pl.BlockSpec((None, 1, D_MODEL), lambda b, i: (b, 0, 0)),
                  pl.BlockSpec(memory_space=pl.ANY)],
        out_specs=tok,
        out_shape=jax.ShapeDtypeStruct(x.shape, _f32),
        scratch_shapes=[pltpu.VMEM((tc, D_MODEL), _f32), pltpu.VMEM((tc, D_MODEL), _f32),
                        pltpu.SemaphoreType.DMA((2,))],
        compiler_params=_params("arbitrary", "arbitrary"),
        name="combine",
    )(pos3, pos3, x, g, ys)


def _experts_kernel(ea_ref, eb_ref, chg_ref, nlive_ref, xs_ref, w1_ref, w3_ref, w2_ref, y_ref,
                    b1, b3, b2, s1, s3, s2, sems, *, layer):
    t = pl.program_id(0)
    live = t < nlive_ref[0]
    first = (ea_ref[t] // EXPERTS_PER_GROUP) * EXPERTS_PER_GROUP

    def stage(k, slot):
        return [pltpu.make_async_copy(w.at[layer, first + k], s.at[slot], sems.at[slot])
                for w, s in ((w1_ref, s1), (w3_ref, s3), (w2_ref, s2))]

    @pl.when(live & ((chg_ref[t] & 1) != 0))
    def _():
        for k in range(2):
            for cp in stage(k, k):
                cp.start()
        for k in range(EXPERTS_PER_GROUP):
            slot = k % 2
            for cp in stage(k, slot):
                cp.wait()
            b1[k] = s1[slot].astype(_bf16)
            b3[k] = s3[slot].astype(_bf16)
            b2[k] = s2[slot].astype(_bf16)
            if k + 2 < EXPERTS_PER_GROUP:
                for cp in stage(k + 2, slot):
                    cp.start()

    def run(nrows):
        xs = xs_ref[:nrows, :D_MODEL].astype(_bf16)
        gates = xs_ref[:nrows, D_MODEL:]

        def expert(e):
            hid = jax.nn.silu(_dot(xs, b1[e])) * _dot(xs, b3[e])
            return _dot(hid.astype(_bf16), b2[e])

        y_ref[:nrows, :] = (gates[:, 0:1] * expert(ea_ref[t] - first) + gates[:, 1:2] * expert(eb_ref[t] - first))

    half_only = (chg_ref[t] & 4) != 0
    half = y_ref.shape[0] // 2

    @pl.when(live & jnp.logical_not(half_only))
    def _():
        run(y_ref.shape[0])

    @pl.when(live & half_only)
    def _():
        run(half)
        y_ref[half:, :] = jnp.zeros((half, D_MODEL), _f32)

    @pl.when(jnp.logical_not(live))
    def _():
        y_ref[...] = jnp.zeros_like(y_ref)


def _experts(xs, tile_ea, tile_eb, chg, nlive, w1, w3, w2, layer, tile):
    ntiles = xs.shape[0] // tile
    hbm = pl.BlockSpec(memory_space=pl.ANY)
    up, down = (D_MODEL, D_EXPERT), (D_EXPERT, D_MODEL)
    grid_spec = pltpu.PrefetchScalarGridSpec(
        num_scalar_prefetch=4,
        grid=(ntiles,),
        in_specs=[pl.BlockSpec((tile, ROW_W), lambda t, ea, eb, cg, nl: (t, 0)), hbm, hbm, hbm],
        out_specs=pl.BlockSpec((tile, D_MODEL), lambda t, ea, eb, cg, nl: (t, 0)),
        scratch_shapes=[pltpu.VMEM((EXPERTS_PER_GROUP,) + up, _bf16), pltpu.VMEM((EXPERTS_PER_GROUP,) + up, _bf16),
                        pltpu.VMEM((EXPERTS_PER_GROUP,) + down, _bf16),
                        pltpu.VMEM((2,) + up, _f32), pltpu.VMEM((2,) + up, _f32), pltpu.VMEM((2,) + down, _f32),
                        pltpu.SemaphoreType.DMA((2,))],
    )
    return pl.pallas_call(
        functools.partial(_experts_kernel, layer=layer),
        grid_spec=grid_spec,
        out_shape=jax.ShapeDtypeStruct((xs.shape[0], D_MODEL), _f32),
        compiler_params=_params("arbitrary"),
        name="experts",
    )(tile_ea, tile_eb, chg, nlive, xs, w1, w3, w2)


def _dispatch_plan(cls, rank, counts, first_counts, later, tile):
    n = cls.shape[0]
    padded = -(-(n + N_CLASSES * (tile - 1)) // tile) * tile
    ntiles = padded // tile
    pcounts = (counts + tile - 1) // tile * tile
    pad_end = jnp.cumsum(pcounts)
    pad_off = pad_end - pcounts
    pos = pad_off[cls] + rank
    nlive = pad_end[-1] // tile
    tile_start = jnp.arange(ntiles, dtype=jnp.int32) * tile
    tile_cls = jnp.sum((pad_end[None, :] <= jnp.minimum(tile_start, pad_end[-1] - 1)[:, None]).astype(jnp.int32), axis=1)
    tile_cls = jnp.minimum(tile_cls, N_CLASSES - 1)
    grp = tile_cls // N_PAIRS
    pair = tile_cls % N_PAIRS
    tile_ea = grp * EXPERTS_PER_GROUP + jnp.asarray(PAIR_A, jnp.int32)[pair]
    tile_eb = grp * EXPERTS_PER_GROUP + jnp.asarray(PAIR_B, jnp.int32)[pair]
    new_group = jnp.concatenate([jnp.ones((1,), jnp.int32), (grp[1:] != grp[:-1]).astype(jnp.int32)])
    tokens_in_tile = (pad_off + counts)[tile_cls] - tile_start
    chg = new_group + 4 * (tokens_in_tile <= tile // 2).astype(jnp.int32)
    per_class = -(-(later + tile - 1) // tile) + 1
    uncovered = pad_off + first_counts
    tails = []
    for j in range(per_class):
        start = pad_end - (j + 1) * tile
        tails.append(jnp.where((start >= pad_off) & (start + tile > uncovered), start, -1))
    past = (nlive + jnp.arange(N_CLASSES, dtype=jnp.int32)) * tile
    fill = jnp.concatenate(tails + [jnp.where(past < padded, past, -1)]).astype(jnp.int32)
    return pos.astype(jnp.int32), padded, tile_ea, tile_eb, chg, nlive.astype(jnp.int32).reshape(1), fill


def _ffn(streams, w1, w3, w2, layer, tile):
    sizes = [s[0].shape[0] * s[0].shape[1] for s in streams]
    cls = [s[2][:, 0, :].reshape(n).astype(jnp.int32) for s, n in zip(streams, sizes)]
    cnts = [s[3][:N_CLASSES, 0].astype(jnp.int32) for s in streams]
    ranks, before = [], jnp.zeros((N_CLASSES,), jnp.int32)
    for s, n, c, cnt in zip(streams, sizes, cls, cnts):
        ranks.append(s[2][:, 1, :].reshape(n).astype(jnp.int32) + before[c])
        before = before + cnt
    pos, padded, tile_ea, tile_eb, chg, nlive, fill = _dispatch_plan(
        jnp.concatenate(cls), jnp.concatenate(ranks), before, cnts[0], sum(sizes[1:]), tile)
    starts = np.cumsum([0] + sizes)
    pos = [pos[a:b] for a, b in zip(starts[:-1], starts[1:])]
    xs = _dispatch(streams[0][1].reshape(sizes[0], ROW_W), pos[0], fill, padded, streams[0][5], tile)
    for s, n, p in zip(streams[1:], sizes[1:], pos[1:]):
        xs = _dispatch_more(xs, s[1].reshape(n, ROW_W), p, s[5])
    ys = _experts(xs, tile_ea, tile_eb, chg, nlive, w1, w3, w2, layer, tile)
    return [_combine(s[0], ys, p, s[4], s[5]) for s, p in zip(streams, pos)]


def kernel(x, c, ctx, c_ctx, w_ada, b_ada, w_in, q_gain, k_gain, rpb, sgu_ln, sgu_w, sgu_b, out_gain,
           w_out, rg_w, rg_b, re_w, re_b, w1, w3, w2):
    bsz, seq, _ = x.shape
    depth = w_ada.shape[0]
    rows = seq // GRID_W
    assert seq % Q_BLOCK == 0 and rows // Q_ROWS >= 3 and ctx.shape[1] % SGU_CHUNK == 0
    tm_x, tm_c = 512, ctx.shape[1]

    cond = jnp.zeros((SUBLANES, D_MODEL), _f32).at[:bsz].set(c).at[bsz].set(c_ctx)
    mods = _adaln(cond, w_ada, b_ada).reshape(depth, SUBLANES, N_MOD, D_MODEL)

    assert MXU_TILE % NA_HEAD_DIM == 0 and NA_W % MXU_TILE == 0
    head_id = np.arange(MXU_TILE) // NA_HEAD_DIM
    avg = jnp.asarray((head_id[:, None] == head_id[None, :]).astype(np.float32) / NA_HEAD_DIM, _bf16)
    win_plan = _window_plan(rows)
    tri = {tm: jnp.asarray(np.triu(np.ones((tm, tm), np.float32), 1), _bf16) for tm in {tm_x, tm_c}}
    sel_np = np.zeros((2 * SUBLANES, LANES), np.float32)
    for n in range(2):
        sel_np[n * GATE_PIECES:(n + 1) * GATE_PIECES, n] = 1.0
    sel = jnp.asarray(sel_np, _bf16)

    for i in range(depth):
        last = i == depth - 1
        mx = [mods[i, :bsz, j][:, None, :] for j in range(N_MOD)]
        mc = [jnp.broadcast_to(mods[i, bsz, j][None, None, :], (bsz, 1, D_MODEL)) for j in range(N_MOD)]
        w_in_b = w_in[i].astype(_bf16)
        qg = (jnp.tile(q_gain[i], NA_HEADS) * NA_HEAD_DIM ** -0.5)[None, :]
        kg = jnp.tile(k_gain[i], NA_HEADS)[None, :]
        ln_g = sgu_ln[i][None, :]
        ws_cat = jnp.concatenate([sgu_w[i, 0::2], sgu_w[i, 1::2]], axis=2).astype(_bf16)
        bs_tab = jnp.repeat(sgu_b[i].T, SGU_GROUP_DIM, axis=1)
        og = out_gain[i][None, :]
        w_out_b = w_out[i].astype(_bf16)
        pad_w = jnp.zeros((SUBLANES - EXPERTS_PER_GROUP, D_MODEL), _f32)
        pad_b = jnp.zeros((SUBLANES - EXPERTS_PER_GROUP,), _f32)
        w_r = jnp.concatenate([rg_w[i].T, pad_w] + [t for g in range(N_GROUPS) for t in (re_w[i, g].T, pad_w)]
                              + [jnp.zeros((ROUTER_ROWS - SUBLANES * (N_GROUPS + 1), D_MODEL), _f32)]).astype(_bf16)
        b_r = jnp.concatenate([rg_b[i], pad_b] + [t for g in range(N_GROUPS) for t in (re_b[i, g], pad_b)]
                              + [jnp.zeros((ROUTER_ROWS - SUBLANES * (N_GROUPS + 1),), _f32)])[:, None]
        bycol = _window_col_table(rpb[i])

        q, k, v, gu, vn = _proj(x, mx[0], mx[1], w_in_b, avg, qg, kg, ln_g, tm_x)
        qc, kc, vc, guc, vnc = _proj(ctx, mc[0], mc[1], w_in_b, avg, qg, kg, ln_g, tm_c)
        o_na = _attention(q, k, v, kc, vc, bycol, win_plan)
        x_new, xrows, meta, counts = _merge(x, o_na, gu, vn, mx[2], mx[3], mx[4], ws_cat, bs_tab, og,
                                            w_out_b, w_r, b_r, tri[tm_x], sel, tm_x)
        streams = [(x_new, xrows, meta, counts, mx[5], 1024)]
        if not last:
            oc_na = _ctx_attention(qc, kc, vc)
            c_new, crows, cmeta, ccounts = _merge(ctx, oc_na, guc, vnc, mc[2], mc[3], mc[4], ws_cat, bs_tab,
                                                  og, w_out_b, w_r, b_r, tri[tm_c], sel, tm_c)
            streams.append((c_new, crows, cmeta, ccounts, mc[5], tm_c))
        outs = _ffn(streams, w1, w3, w2, i, 512)
        x = outs[0]
        if not last:
            ctx = outs[1]
    return x
```

```python
import functools

import jax
import jax.numpy as jnp
import numpy as np
from jax import lax
from jax.experimental import pallas as pl
from jax.experimental.pallas import tpu as pltpu

D_MODEL = 1024
GRID_W = 64
NA_HEADS = 8
NA_HEAD_DIM = 64
NA_W = NA_HEADS * NA_HEAD_DIM
NA_KH = 8
NA_KW = 16
SGU_GROUPS = 8
SGU_W = D_MODEL - NA_W
SGU_GROUP_DIM = SGU_W // SGU_GROUPS
SGU_CHUNK = 128
IN_COLS = 3 * NA_W + 2 * SGU_W
N_GROUPS = 4
EXPERTS_PER_GROUP = 4
N_EXPERTS = N_GROUPS * EXPERTS_PER_GROUP
D_EXPERT = D_MODEL // 2
N_MOD = 6
EPS = 1e-6

LANES = 128
SUBLANES = 8
HEAD_PAIRS = NA_W // LANES
Q_ROWS = 8
Q_BLOCK = Q_ROWS * GRID_W
WIN_ROWS = Q_ROWS + NA_KH - 1
WIN_BLOCK = WIN_ROWS * GRID_W
PAIR_ROWS = 2
PAIR_BLOCK = PAIR_ROWS * GRID_W
PAIR_WIN_ROWS = PAIR_ROWS + NA_KH - 1
PAIR_WIN = PAIR_WIN_ROWS * GRID_W
PAIRS_IN_FLIGHT = 2
N_DR = 2 * NA_KH
N_PAIRS = 6
N_CLASSES = N_GROUPS * N_PAIRS
PAIR_A = (0, 0, 1, 1, 0, 2)
PAIR_B = (1, 2, 2, 3, 3, 3)
PAIR_OF_LEX = (0, 1, 4, 2, 3, 5)
CLASS_ROWS = 32
ROUTER_ROWS = 48
GATE_PIECES = 3
ROW_W = D_MODEL + LANES
MXU_TILE = 256
MASKED = -1e30
MAX_SHIFT_SPREAD = 60.0
VMEM_LIMIT_BYTES = 56 * 1024 * 1024

_f32 = jnp.float32
_bf16 = jnp.bfloat16


def _params(*semantics):
    return pltpu.CompilerParams(dimension_semantics=semantics, vmem_limit_bytes=VMEM_LIMIT_BYTES)


def _dot(a, b):
    return jnp.dot(a, b, preferred_element_type=_f32)


def _dot_nt(a, b):
    return lax.dot_general(a, b, (((1,), (1,)), ((), ())), preferred_element_type=_f32)


def _adaln_kernel(cond_ref, w_ref, b_ref, out_ref):
    cond = cond_ref[...]
    act = cond * jax.nn.sigmoid(cond)
    out_ref[...] = _dot(act.astype(_bf16), w_ref[...].astype(_bf16)) + b_ref[...]


def _adaln(cond, w_ada, b_ada):
    depth = w_ada.shape[0]
    cols = N_MOD * D_MODEL
    col_block = 1536
    return pl.pallas_call(
        _adaln_kernel,
        grid=(depth, cols // col_block),
        in_specs=[
            pl.BlockSpec((SUBLANES, D_MODEL), lambda d, j: (0, 0)),
            pl.BlockSpec((None, D_MODEL, col_block), lambda d, j: (d, 0, j)),
            pl.BlockSpec((None, 1, col_block), lambda d, j: (d, 0, j)),
        ],
        out_specs=pl.BlockSpec((None, SUBLANES, col_block), lambda d, j: (d, 0, j)),
        out_shape=jax.ShapeDtypeStruct((depth, SUBLANES, cols), _f32),
        compiler_params=_params("arbitrary", "arbitrary"),
        name="adaln",
    )(cond, w_ada, b_ada.reshape(depth, 1, cols))


def _proj_kernel(x_ref, sh_ref, sc_ref, w_ref, avg_ref, qg_ref, kg_ref, ln_ref,
                 q_ref, k_ref, v_ref, gu_ref, vn_ref):
    x = x_ref[...]
    h = x * lax.rsqrt(jnp.mean(x * x, -1, keepdims=True) + EPS) * (1.0 + sc_ref[...]) + sh_ref[...]
    p = _dot(h.astype(_bf16), w_ref[...])
    q = p[:, :NA_W]
    k = p[:, NA_W:2 * NA_W]
    v = p[:, 2 * NA_W:3 * NA_W]
    u = p[:, 3 * NA_W:3 * NA_W + SGU_W]
    vs = p[:, 3 * NA_W + SGU_W:]
    def head_mean(t):
        sq = (t * t).astype(_bf16)
        return jnp.concatenate([_dot(sq[:, c:c + MXU_TILE], avg_ref[...]) for c in range(0, NA_W, MXU_TILE)], axis=1)

    q_ms = head_mean(q)
    k_ms = head_mean(k)
    qn = (q * lax.rsqrt(q_ms + EPS) * qg_ref[...]).astype(_bf16)
    kn = (k * lax.rsqrt(k_ms + EPS) * kg_ref[...]).astype(_bf16)
    vb = v.astype(_bf16)
    for hp in range(HEAD_PAIRS):
        cols = slice(hp * LANES, (hp + 1) * LANES)
        q_ref[hp] = qn[:, cols]
        k_ref[hp] = kn[:, cols]
        v_ref[hp] = vb[:, cols]
    gu_ref[...] = jax.nn.gelu(u).astype(_bf16)
    gv = jax.nn.gelu(vs)
    mu = jnp.mean(gv, -1, keepdims=True)
    cen = gv - mu
    var = jnp.mean(cen * cen, -1, keepdims=True)
    vn_ref[...] = (cen * lax.rsqrt(var + EPS) * ln_ref[...]).astype(_bf16)


def _proj(x, sh, sc, w_in, avg, qg, kg, ln_g, tm):
    bsz, length, _ = x.shape
    vec = lambda n: pl.BlockSpec((1, n), lambda b, i: (0, 0))
    mod = pl.BlockSpec((None, 1, D_MODEL), lambda b, i: (b, 0, 0))
    tok = lambda n: pl.BlockSpec((None, tm, n), lambda b, i: (b, i, 0))
    hpm = pl.BlockSpec((None, HEAD_PAIRS, tm, LANES), lambda b, i: (b, 0, i, 0))
    hpm_shape = jax.ShapeDtypeStruct((bsz, HEAD_PAIRS, length, LANES), _bf16)
    sgu_shape = jax.ShapeDtypeStruct((bsz, length, SGU_W), _bf16)
    return pl.pallas_call(
        _proj_kernel,
        grid=(bsz, length // tm),
        in_specs=[tok(D_MODEL), mod, mod,
                  pl.BlockSpec((D_MODEL, IN_COLS), lambda b, i: (0, 0)),
                  pl.BlockSpec((MXU_TILE, MXU_TILE), lambda b, i: (0, 0)),
                  vec(NA_W), vec(NA_W), vec(SGU_W)],
        out_specs=[hpm, hpm, hpm, tok(SGU_W), tok(SGU_W)],
        out_shape=[hpm_shape, hpm_shape, hpm_shape, sgu_shape, sgu_shape],
        compiler_params=_params("arbitrary", "arbitrary"),
        name="proj",
    )(x, sh, sc, w_in, avg, qg, kg, ln_g)


def _attend(blocks, shift=None):
    add = lambda a, b: a + b
    m_rows = blocks[0][0].shape[0]
    lane = lax.broadcasted_iota(jnp.int32, blocks[0][0].shape, 1)
    low = lane < NA_HEAD_DIM
    stacked = [jnp.concatenate([jnp.where(low, q, jnp.zeros_like(q)), jnp.where(low, jnp.zeros_like(q), q)], axis=0)
               for q, _, _, _ in blocks]
    if shift is None:
        scores = [[_dot_nt(qs, key) if bias is None else _dot_nt(qs, key) + bias[...]
                   for key, bias in zip(keys, biases)] for qs, (_, keys, _, biases) in zip(stacked, blocks)]
        maxes = [functools.reduce(jnp.maximum, [jnp.max(s, -1, keepdims=True) for s in ss]) for ss in scores]
        probs = [[jnp.exp(s - m) for s in ss] for ss, m in zip(scores, maxes)]
    else:
        probs = [[jnp.exp(_dot_nt(qs, key) - shift) if bias is None else jnp.exp(_dot_nt(qs, key) + bias[...])
                  for key, bias in zip(keys, biases)] for qs, (_, keys, _, biases) in zip(stacked, blocks)]
    dens = [functools.reduce(add, [jnp.sum(p, -1, keepdims=True) for p in ps]) for ps in probs]
    accs = [functools.reduce(add, [_dot(p.astype(_bf16), val) for p, val in zip(ps, blk[2])])
            for ps, blk in zip(probs, blocks)]
    outs = [acc / den for acc, den in zip(accs, dens)]
    return [jnp.where(low, o2[:m_rows], o2[m_rows:]) for o2 in outs]


def _attn_kernel(off_ref, var_ref, shift_ref, q_ref, k_ref, v_ref, kc_ref, vc_ref, bycol_ref, o_ref, bias_ref,
                 *, pieces, shifted):
    i = pl.program_id(2)
    shift = shift_ref[0] if shifted else None

    @pl.when(i == 0)
    def _():
        for var, piece in enumerate(pieces):
            for hh in range(2):
                for j in range(PAIR_ROWS):
                    for kr in range(PAIR_WIN_ROWS):
                        lo = (kr % 2) * GRID_W
                        row = hh * PAIR_BLOCK + j * GRID_W
                        bias_ref[var, row:row + GRID_W, kr * GRID_W:(kr + 1) * GRID_W] = (
                            bycol_ref[hh, piece[j][kr], :, lo:lo + GRID_W])

    kc = kc_ref[...]
    vc = vc_ref[...]
    npair = Q_ROWS // PAIR_ROWS
    for g0 in range(0, npair, PAIRS_IN_FLIGHT):
        blocks = []
        for g in range(g0, g0 + PAIRS_IN_FLIGHT):
            start = pl.multiple_of(off_ref[i * npair + g] * GRID_W, GRID_W)
            var = var_ref[i * npair + g]
            blocks.append((q_ref[g * PAIR_BLOCK:(g + 1) * PAIR_BLOCK, :],
                           [k_ref[pl.ds(start, PAIR_WIN), :], kc],
                           [v_ref[pl.ds(start, PAIR_WIN), :], vc],
                           [bias_ref.at[var], None]))
        for g, out in zip(range(g0, g0 + PAIRS_IN_FLIGHT), _attend(blocks, shift)):
            o_ref[g * PAIR_BLOCK:(g + 1) * PAIR_BLOCK, :] = out.astype(o_ref.dtype)


def _attention(q, k, v, kc, vc, bycol, plan, shift, shifted):
    off_tab, var_tab, pieces = plan
    bsz, _, seq, _ = q.shape
    clen = kc.shape[2]
    nblk = seq // Q_BLOCK

    def win_start(i):
        return jnp.clip(i * Q_ROWS - NA_KH // 2, 0, seq // GRID_W - WIN_ROWS) * GRID_W

    qspec = pl.BlockSpec((None, None, Q_BLOCK, LANES), lambda hp, b, i, *_: (b, hp, i, 0))
    wspec = pl.BlockSpec((None, None, pl.Element(WIN_BLOCK), pl.Element(LANES)),
                         lambda hp, b, i, *_: (b, hp, win_start(i), 0))
    cspec = pl.BlockSpec((None, None, clen, LANES), lambda hp, b, i, *_: (b, hp, 0, 0))
    tspec = pl.BlockSpec((None, 2, N_DR, GRID_W, LANES), lambda hp, b, i, *_: (hp, 0, 0, 0, 0))
    grid_spec = pltpu.PrefetchScalarGridSpec(
        num_scalar_prefetch=3,
        grid=(HEAD_PAIRS, bsz, nblk),
        in_specs=[qspec, wspec, wspec, cspec, cspec, tspec],
        out_specs=qspec,
        scratch_shapes=[pltpu.VMEM((len(pieces), 2 * PAIR_BLOCK, PAIR_WIN), _f32)],
    )
    return pl.pallas_call(
        functools.partial(_attn_kernel, pieces=pieces, shifted=shifted),
        grid_spec=grid_spec,
        out_shape=jax.ShapeDtypeStruct(q.shape, _bf16),
        compiler_params=_params("arbitrary", "arbitrary", "arbitrary"),
        name="attention_shifted" if shifted else "attention",
    )(off_tab, var_tab, shift, q, k, v, kc, vc, bycol)


def _ctx_attn_kernel(q_ref, k_ref, v_ref, o_ref):
    o_ref[...] = _attend([(q_ref[...], [k_ref[...]], [v_ref[...]], [None])])[0].astype(o_ref.dtype)


def _ctx_attention(q, k, v):
    bsz, _, clen, _ = q.shape
    spec = pl.BlockSpec((None, None, clen, LANES), lambda b, hp: (b, hp, 0, 0))
    return pl.pallas_call(
        _ctx_attn_kernel,
        grid=(bsz, HEAD_PAIRS),
        in_specs=[spec, spec, spec],
        out_specs=spec,
        out_shape=jax.ShapeDtypeStruct(q.shape, _bf16),
        compiler_params=_params("arbitrary", "arbitrary"),
        name="ctx_attention",
    )(q, k, v)


def _window_plan(rows):
    patterns, off_tab, var_tab = {}, [], []
    for r in range(0, rows, PAIR_ROWS):
        ws = int(np.clip(r - NA_KH // 2, 0, rows - PAIR_WIN_ROWS))
        r0 = r // Q_ROWS * Q_ROWS
        block_ws = int(np.clip(r0 - NA_KH // 2, 0, rows - WIN_ROWS))
        assert 0 <= ws - block_ws <= WIN_ROWS - PAIR_WIN_ROWS
        piece = []
        for j in range(PAIR_ROWS):
            r_start = int(np.clip(r + j - NA_KH // 2, 0, rows - NA_KH))
            assert ws <= r_start and r_start + NA_KH <= ws + PAIR_WIN_ROWS
            piece.append(tuple(ws + kr - (r + j) + NA_KH - 1 if r_start <= ws + kr < r_start + NA_KH else N_DR - 1
                               for kr in range(PAIR_WIN_ROWS)))
        var_tab.append(patterns.setdefault(tuple(piece), len(patterns)))
        off_tab.append(ws - block_ws)
    return jnp.asarray(off_tab, jnp.int32), jnp.asarray(var_tab, jnp.int32), tuple(patterns)


def _window_col_table(rpb, shift):
    col = np.arange(GRID_W)
    col_start = np.clip(col - NA_KW // 2, 0, GRID_W - NA_KW)
    col_ok = (col[None, :] >= col_start[:, None]) & (col[None, :] < col_start[:, None] + NA_KW)
    dc = np.where(col_ok, col[None, :] - col[:, None] + NA_KW - 1, 0)
    onehot = np.zeros((2 * NA_KW - 1, GRID_W * GRID_W), np.float32)
    onehot[dc.reshape(-1), np.arange(GRID_W * GRID_W)] = col_ok.reshape(-1)
    by_col = jnp.dot(rpb.reshape(-1, 2 * NA_KW - 1), jnp.asarray(onehot), precision=lax.Precision.HIGHEST)
    by_col = by_col.reshape(NA_HEADS, 2 * NA_KH - 1, GRID_W, GRID_W)
    by_col = jnp.where(jnp.asarray(col_ok)[None, None], by_col - shift, MASKED)
    by_col = jnp.concatenate([by_col, jnp.full((NA_HEADS, 1, GRID_W, GRID_W), MASKED, _f32)], axis=1)
    by_col = jnp.concatenate([by_col, by_col], axis=-1)
    return by_col.reshape(HEAD_PAIRS, 2, N_DR, GRID_W, LANES)


def _merge_kernel(x_ref, o_ref, gu_ref, vn_ref, ga_ref, shf_ref, scf_ref, ws_ref, bs_ref, og_ref,
                  wout_ref, wr_ref, br_ref, tri_ref, sel_ref, xn_ref, row_ref, meta_ref, cnt_ref, *, tm):
    first = (pl.program_id(0) == 0) & (pl.program_id(1) == 0)

    @pl.when(first)
    def _():
        cnt_ref[...] = jnp.zeros_like(cnt_ref)

    nchunk = tm // SGU_CHUNK
    vn = vn_ref[...]
    lane = lax.broadcasted_iota(jnp.int32, (SGU_CHUNK, LANES), 1)
    low = lane < SGU_GROUP_DIM
    mixed_cols = []
    for gp in range(SGU_W // LANES):
        rhs_cols = []
        for n in range(nchunk):
            blk = vn[n * SGU_CHUNK:(n + 1) * SGU_CHUNK, gp * LANES:(gp + 1) * LANES]
            zero = jnp.zeros_like(blk)
            rhs_cols.append(jnp.concatenate([jnp.where(low, blk, zero), jnp.where(low, zero, blk)], axis=0))
        rhs = jnp.concatenate(rhs_cols, axis=1)
        mix = _dot(ws_ref[gp], rhs)
        mixed_cols.append(jnp.concatenate([mix[:, n * LANES:(n + 1) * LANES] for n in range(nchunk)], axis=0))
    mixed = jnp.concatenate(mixed_cols, axis=1)
    bias = jnp.concatenate([bs_ref[...]] * nchunk, axis=0)
    o_sg = gu_ref[...].astype(_f32) * (mixed + bias)
    o_na = jnp.concatenate([o_ref[hp].astype(_f32) for hp in range(HEAD_PAIRS)], axis=1)

    def rms(t):
        return t * lax.rsqrt(jnp.mean(t * t, -1, keepdims=True) + EPS)

    y = jnp.concatenate([rms(o_na), rms(o_sg)], axis=1) * og_ref[...]
    xn = x_ref[...] + ga_ref[...] * _dot(y.astype(_bf16), wout_ref[...])
    xn_ref[...] = xn
    hf = rms(xn) * (1.0 + scf_ref[...]) + shf_ref[...]
    row_ref[:, :D_MODEL] = hf

    lg = _dot_nt(wr_ref[...], hf.astype(_bf16)) + br_ref[...]
    quad = lambda r: lg[r:r + EXPERTS_PER_GROUP, :]
    row4 = lax.broadcasted_iota(jnp.int32, (EXPERTS_PER_GROUP, tm), 0).astype(_f32)
    over = lambda fn, t: fn(t, 0, keepdims=True)
    gl = quad(0)
    gm = over(jnp.max, gl)
    p_top = 1.0 / over(jnp.sum, jnp.exp(gl - gm))
    g_idx = over(jnp.min, jnp.where(gl == gm, row4, float(N_GROUPS)))
    el = quad(SUBLANES * N_GROUPS)
    for g in range(N_GROUPS - 2, -1, -1):
        el = jnp.where(g_idx == float(g), quad(SUBLANES * (g + 1)), el)
    ee = jnp.exp(el - over(jnp.max, el))
    pe = ee / over(jnp.sum, ee)
    v1 = over(jnp.max, pe)
    i1 = over(jnp.min, jnp.where(pe == v1, row4, float(EXPERTS_PER_GROUP)))
    pe2 = jnp.where(row4 == i1, -1.0, pe)
    v2 = over(jnp.max, pe2)
    i2 = over(jnp.min, jnp.where(pe2 == v2, row4, float(EXPERTS_PER_GROUP)))
    den = v1 + v2
    w1 = v1 / den * p_top
    w2 = v2 / den * p_top
    first_low = i1 < i2
    a = jnp.minimum(i1, i2)
    b = jnp.maximum(i1, i2)
    lex = a * (7.0 - a) * 0.5 + (b - a - 1.0)
    pair = lex
    for src, dst in enumerate(PAIR_OF_LEX):
        if src != dst:
            pair = jnp.where(lex == float(src), float(dst), pair)
    cls = g_idx * float(N_PAIRS) + pair
    gate_a = jnp.where(first_low, w1, w2)
    gate_b = jnp.where(first_low, w2, w1)

    row_p = lax.broadcasted_iota(jnp.int32, (2 * SUBLANES, tm), 0)
    stacked = jnp.zeros((2 * SUBLANES, tm), _f32)
    for n, gate in enumerate((gate_a, gate_b)):
        rest = gate
        for k in range(GATE_PIECES):
            piece = rest.astype(_bf16).astype(_f32)
            stacked = jnp.where(row_p == n * GATE_PIECES + k, piece, stacked)
            rest = rest - piece
    row_ref[:, D_MODEL:] = lax.dot_general(stacked.astype(_bf16), sel_ref[...],
                                           (((0,), (0,)), ((), ())), preferred_element_type=_f32)

    row_c = lax.broadcasted_iota(jnp.int32, (CLASS_ROWS, tm), 0).astype(_f32)
    onehot = row_c == cls
    ones = jnp.where(onehot, 1.0, 0.0)
    before = _dot(ones.astype(_bf16), tri_ref[...])
    rank = over(jnp.sum, jnp.where(onehot, before + cnt_ref[:, 0:1], 0.0))
    cnt_ref[...] += jnp.sum(ones, 1, keepdims=True)
    meta_ref[...] = jnp.concatenate([cls, rank, jnp.zeros((SUBLANES - 2, tm), _f32)], axis=0)


def _merge(x, o_na, gu, vn, g_a, sh_f, sc_f, ws_cat, bs_tab, out_gain, w_out, w_r, b_r, tri, sel, tm):
    bsz, length, _ = x.shape
    mod = pl.BlockSpec((None, 1, D_MODEL), lambda b, i: (b, 0, 0))
    tok = lambda n: pl.BlockSpec((None, tm, n), lambda b, i: (b, i, 0))
    const = lambda *shape: pl.BlockSpec(shape, lambda b, i: (0,) * len(shape))
    return pl.pallas_call(
        functools.partial(_merge_kernel, tm=tm),
        grid=(bsz, length // tm),
        in_specs=[tok(D_MODEL),
                  pl.BlockSpec((None, HEAD_PAIRS, tm, LANES), lambda b, i: (b, 0, i, 0)),
                  tok(SGU_W), tok(SGU_W), mod, mod, mod,
                  const(SGU_W // LANES, SGU_CHUNK, 2 * SGU_CHUNK),
                  const(SGU_CHUNK, SGU_W),
                  const(1, D_MODEL),
                  const(D_MODEL, D_MODEL),
                  const(ROUTER_ROWS, D_MODEL),
                  const(ROUTER_ROWS, 1),
                  const(tm, tm),
                  const(2 * SUBLANES, LANES)],
        out_specs=[tok(D_MODEL), tok(ROW_W),
                   pl.BlockSpec((None, SUBLANES, tm), lambda b, i: (b, 0, i)),
                   const(CLASS_ROWS, LANES)],
        out_shape=[jax.ShapeDtypeStruct((bsz, length, D_MODEL), _f32),
                   jax.ShapeDtypeStruct((bsz, length, ROW_W), _f32),
                   jax.ShapeDtypeStruct((bsz, SUBLANES, length), _f32),
                   jax.ShapeDtypeStruct((CLASS_ROWS, LANES), _f32)],
        compiler_params=_params("arbitrary", "arbitrary"),
        name="merge",
    )(x, o_na, gu, vn, g_a, sh_f, sc_f, ws_cat, bs_tab, out_gain, w_out, w_r, b_r, tri, sel)


def _issue_row_copies(count, make_copy):
    for j in range(count):
        make_copy(j).start(priority=j % 2)


def _dispatch_kernel(fill_ref, pos_ref, rows_ref, xs_ref, zero_ref, sem, fill_sem, *, td, tile):
    @pl.when(pl.program_id(0) == 0)
    def _():
        zero_ref[...] = jnp.zeros_like(zero_ref)

        def fill(k):
            start = pl.multiple_of(jnp.maximum(fill_ref[k], 0), tile)
            return pltpu.make_async_copy(zero_ref, xs_ref.at[pl.ds(start, tile)], fill_sem)

        for k in range(fill_ref.shape[0]):
            @pl.when(fill_ref[k] >= 0)
            def _():
                fill(k).start()
        for k in range(fill_ref.shape[0]):
            @pl.when(fill_ref[k] >= 0)
            def _():
                fill(k).wait()

    _scatter_rows(pos_ref, rows_ref, xs_ref, sem, td)


def _scatter_rows(pos_ref, rows_ref, xs_ref, sem, td):
    _issue_row_copies(td, lambda j: pltpu.make_async_copy(
        rows_ref.at[pl.ds(j, 1)], xs_ref.at[pl.ds(pos_ref[0, j], 1)], sem))
    pltpu.make_async_copy(rows_ref, xs_ref.at[pl.ds(0, td)], sem).wait()


def _dispatch_more_kernel(pos_ref, rows_ref, base_ref, xs_ref, sem, *, td):
    del base_ref
    _scatter_rows(pos_ref, rows_ref, xs_ref, sem, td)


def _dispatch(rows, pos, fill, padded, td, tile):
    n = rows.shape[0]
    grid_spec = pltpu.PrefetchScalarGridSpec(
        num_scalar_prefetch=1,
        grid=(n // td,),
        in_specs=[pl.BlockSpec((None, 1, td), lambda i, fl: (i, 0, 0), memory_space=pltpu.SMEM),
                  pl.BlockSpec((td, ROW_W), lambda i, fl: (i, 0))],
        out_specs=pl.BlockSpec(memory_space=pl.ANY),
        scratch_shapes=[pltpu.VMEM((tile, ROW_W), _f32), pltpu.SemaphoreType.DMA(()), pltpu.SemaphoreType.DMA(())],
    )
    return pl.pallas_call(
        functools.partial(_dispatch_kernel, td=td, tile=tile),
        grid_spec=grid_spec,
        out_shape=jax.ShapeDtypeStruct((padded, ROW_W), _f32),
        compiler_params=_params("arbitrary"),
        name="dispatch",
    )(fill, pos.reshape(n // td, 1, td), rows)


def _dispatch_more(xs, rows, pos, td):
    n = rows.shape[0]
    return pl.pallas_call(
        functools.partial(_dispatch_more_kernel, td=td),
        grid=(n // td,),
        in_specs=[pl.BlockSpec((None, 1, td), lambda i: (i, 0, 0), memory_space=pltpu.SMEM),
                  pl.BlockSpec((td, ROW_W), lambda i: (i, 0)),
                  pl.BlockSpec(memory_space=pl.ANY)],
        out_specs=pl.BlockSpec(memory_space=pl.ANY),
        out_shape=jax.ShapeDtypeStruct(xs.shape, _f32),
        scratch_shapes=[pltpu.SemaphoreType.DMA(())],
        input_output_aliases={2: 0},
        compiler_params=_params("arbitrary"),
        name="dispatch_more",
    )(pos.reshape(n // td, 1, td), rows, xs)


def _combine_kernel(pos_ref, npos_ref, x_ref, g_ref, ys_ref, o_ref, buf0, buf1, sems, *, tc, nsteps):
    s = pl.program_id(0) * pl.num_programs(1) + pl.program_id(1)
    bufs = (buf0, buf1)

    def request(idx_ref, slot):
        _issue_row_copies(tc, lambda j: pltpu.make_async_copy(
            ys_ref.at[pl.ds(idx_ref[0, j], 1)], bufs[slot].at[pl.ds(j, 1)], sems.at[slot]))

    @pl.when(s == 0)
    def _():
        request(pos_ref, 0)

    for slot in range(2):
        @pl.when((s % 2 == slot) & (s + 1 < nsteps))
        def _():
            request(npos_ref, 1 - slot)

        @pl.when(s % 2 == slot)
        def _():
            pltpu.make_async_copy(ys_ref.at[pl.ds(0, tc)], bufs[slot], sems.at[slot]).wait()
            o_ref[...] = x_ref[...] + g_ref[...] * bufs[slot][...]


def _combine(x, ys, pos, g, tc):
    bsz, length, _ = x.shape
    per_batch = length // tc
    nsteps = bsz * per_batch
    tok = pl.BlockSpec((None, tc, D_MODEL), lambda b, i: (b, i, 0))
    pos3 = pos.reshape(nsteps, 1, tc)
    return pl.pallas_call(
        functools.partial(_combine_kernel, tc=tc, nsteps=nsteps),
        grid=(bsz, per_batch),
        in_specs=[pl.BlockSpec((None, 1, tc), lambda b, i: (b * per_batch + i, 0, 0), memory_space=pltpu.SMEM),
                  pl.BlockSpec((None, 1, tc), lambda b, i: (jnp.minimum(b * per_batch + i + 1, nsteps - 1), 0, 0),
                               memory_space=pltpu.SMEM),
                  tok,
                  pl.BlockSpec((None, 1, D_MODEL), lambda b, i: (b, 0, 0)),
                  pl.BlockSpec(memory_space=pl.ANY)],
        out_specs=tok,
        out_shape=jax.ShapeDtypeStruct(x.shape, _f32),
        scratch_shapes=[pltpu.VMEM((tc, D_MODEL), _f32), pltpu.VMEM((tc, D_MODEL), _f32),
                        pltpu.SemaphoreType.DMA((2,))],
        compiler_params=_params("arbitrary", "arbitrary"),
        name="combine",
    )(pos3, pos3, x, g, ys)


def _experts_kernel(ea_ref, eb_ref, chg_ref, nlive_ref, xs_ref, w1_ref, w3_ref, w2_ref, y_ref,
                    b1, b3, b2, s1, s3, s2, sems, *, layer):
    t = pl.program_id(0)
    live = t < nlive_ref[0]
    first = (ea_ref[t] // EXPERTS_PER_GROUP) * EXPERTS_PER_GROUP

    def stage(k, slot):
        return [pltpu.make_async_copy(w.at[layer, first + k], s.at[slot], sems.at[slot])
                for w, s in ((w1_ref, s1), (w3_ref, s3), (w2_ref, s2))]

    @pl.when(live & ((chg_ref[t] & 1) != 0))
    def _():
        for k in range(2):
            for cp in stage(k, k):
                cp.start()
        for k in range(EXPERTS_PER_GROUP):
            slot = k % 2
            for cp in stage(k, slot):
                cp.wait()
            b1[k] = s1[slot].astype(_bf16)
            b3[k] = s3[slot].astype(_bf16)
            b2[k] = s2[slot].astype(_bf16)
            if k + 2 < EXPERTS_PER_GROUP:
                for cp in stage(k + 2, slot):
                    cp.start()

    def run(nrows):
        xs = xs_ref[:nrows, :D_MODEL].astype(_bf16)
        gates = xs_ref[:nrows, D_MODEL:]

        def expert(e):
            hid = jax.nn.silu(_dot(xs, b1[e])) * _dot(xs, b3[e])
            return _dot(hid.astype(_bf16), b2[e])

        y_ref[:nrows, :] = (gates[:, 0:1] * expert(ea_ref[t] - first) + gates[:, 1:2] * expert(eb_ref[t] - first))

    half_only = (chg_ref[t] & 4) != 0
    half = y_ref.shape[0] // 2

    @pl.when(live & jnp.logical_not(half_only))
    def _():
        run(y_ref.shape[0])

    @pl.when(live & half_only)
    def _():
        run(half)
        y_ref[half:, :] = jnp.zeros((half, D_MODEL), _f32)

    @pl.when(jnp.logical_not(live))
    def _():
        y_ref[...] = jnp.zeros_like(y_ref)


def _experts(xs, tile_ea, tile_eb, chg, nlive, w1, w3, w2, layer, tile):
    ntiles = xs.shape[0] // tile
    hbm = pl.BlockSpec(memory_space=pl.ANY)
    up, down = (D_MODEL, D_EXPERT), (D_EXPERT, D_MODEL)
    grid_spec = pltpu.PrefetchScalarGridSpec(
        num_scalar_prefetch=4,
        grid=(ntiles,),
        in_specs=[pl.BlockSpec((tile, ROW_W), lambda t, ea, eb, cg, nl: (t, 0)), hbm, hbm, hbm],
        out_specs=pl.BlockSpec((tile, D_MODEL), lambda t, ea, eb, cg, nl: (t, 0)),
        scratch_shapes=[pltpu.VMEM((EXPERTS_PER_GROUP,) + up, _bf16), pltpu.VMEM((EXPERTS_PER_GROUP,) + up, _bf16),
                        pltpu.VMEM((EXPERTS_PER_GROUP,) + down, _bf16),
                        pltpu.VMEM((2,) + up, _f32), pltpu.VMEM((2,) + up, _f32), pltpu.VMEM((2,) + down, _f32),
                        pltpu.SemaphoreType.DMA((2,))],
    )
    return pl.pallas_call(
        functools.partial(_experts_kernel, layer=layer),
        grid_spec=grid_spec,
        out_shape=jax.ShapeDtypeStruct((xs.shape[0], D_MODEL), _f32),
        compiler_params=_params("arbitrary"),
        name="experts",
    )(tile_ea, tile_eb, chg, nlive, xs, w1, w3, w2)


def _dispatch_plan(cls, rank, counts, first_counts, later, tile):
    n = cls.shape[0]
    padded = -(-(n + N_CLASSES * (tile - 1)) // tile) * tile
    ntiles = padded // tile
    pcounts = (counts + tile - 1) // tile * tile
    pad_end = jnp.cumsum(pcounts)
    pad_off = pad_end - pcounts
    pos = pad_off[cls] + rank
    nlive = pad_end[-1] // tile
    tile_start = jnp.arange(ntiles, dtype=jnp.int32) * tile
    tile_cls = jnp.sum((pad_end[None, :] <= jnp.minimum(tile_start, pad_end[-1] - 1)[:, None]).astype(jnp.int32), axis=1)
    tile_cls = jnp.minimum(tile_cls, N_CLASSES - 1)
    grp = tile_cls // N_PAIRS
    pair = tile_cls % N_PAIRS
    tile_ea = grp * EXPERTS_PER_GROUP + jnp.asarray(PAIR_A, jnp.int32)[pair]
    tile_eb = grp * EXPERTS_PER_GROUP + jnp.asarray(PAIR_B, jnp.int32)[pair]
    new_group = jnp.concatenate([jnp.ones((1,), jnp.int32), (grp[1:] != grp[:-1]).astype(jnp.int32)])
    tokens_in_tile = (pad_off + counts)[tile_cls] - tile_start
    chg = new_group + 4 * (tokens_in_tile <= tile // 2).astype(jnp.int32)
    per_class = -(-(later + tile - 1) // tile) + 1
    uncovered = pad_off + first_counts
    tails = []
    for j in range(per_class):
        start = pad_end - (j + 1) * tile
        tails.append(jnp.where((start >= pad_off) & (start + tile > uncovered), start, -1))
    past = (nlive + jnp.arange(N_CLASSES, dtype=jnp.int32)) * tile
    fill = jnp.concatenate(tails + [jnp.where(past < padded, past, -1)]).astype(jnp.int32)
    return pos.astype(jnp.int32), padded, tile_ea, tile_eb, chg, nlive.astype(jnp.int32).reshape(1), fill


def _ffn(streams, w1, w3, w2, layer, tile):
    sizes = [s[0].shape[0] * s[0].shape[1] for s in streams]
    cls = [s[2][:, 0, :].reshape(n).astype(jnp.int32) for s, n in zip(streams, sizes)]
    cnts = [s[3][:N_CLASSES, 0].astype(jnp.int32) for s in streams]
    ranks, before = [], jnp.zeros((N_CLASSES,), jnp.int32)
    for s, n, c, cnt in zip(streams, sizes, cls, cnts):
        ranks.append(s[2][:, 1, :].reshape(n).astype(jnp.int32) + before[c])
        before = before + cnt
    pos, padded, tile_ea, tile_eb, chg, nlive, fill = _dispatch_plan(
        jnp.concatenate(cls), jnp.concatenate(ranks), before, cnts[0], sum(sizes[1:]), tile)
    starts = np.cumsum([0] + sizes)
    pos = [pos[a:b] for a, b in zip(starts[:-1], starts[1:])]
    xs = _dispatch(streams[0][1].reshape(sizes[0], ROW_W), pos[0], fill, padded, streams[0][5], tile)
    for s, n, p in zip(streams[1:], sizes[1:], pos[1:]):
        xs = _dispatch_more(xs, s[1].reshape(n, ROW_W), p, s[5])
    ys = _experts(xs, tile_ea, tile_eb, chg, nlive, w1, w3, w2, layer, tile)
    return [_combine(s[0], ys, p, s[4], s[5]) for s, p in zip(streams, pos)]


def kernel(x, c, ctx, c_ctx, w_ada, b_ada, w_in, q_gain, k_gain, rpb, sgu_ln, sgu_w, sgu_b, out_gain,
           w_out, rg_w, rg_b, re_w, re_b, w1, w3, w2):
    bsz, seq, _ = x.shape
    depth = w_ada.shape[0]
    rows = seq // GRID_W
    assert seq % Q_BLOCK == 0 and rows // Q_ROWS >= 3 and ctx.shape[1] % SGU_CHUNK == 0
    tm_x, tm_c = 512, ctx.shape[1]

    cond = jnp.zeros((SUBLANES, D_MODEL), _f32).at[:bsz].set(c).at[bsz].set(c_ctx)
    mods = _adaln(cond, w_ada, b_ada).reshape(depth, SUBLANES, N_MOD, D_MODEL)

    assert MXU_TILE % NA_HEAD_DIM == 0 and NA_W % MXU_TILE == 0
    head_id = np.arange(MXU_TILE) // NA_HEAD_DIM
    avg = jnp.asarray((head_id[:, None] == head_id[None, :]).astype(np.float32) / NA_HEAD_DIM, _bf16)
    win_plan = _window_plan(rows)
    tri = {tm: jnp.asarray(np.triu(np.ones((tm, tm), np.float32), 1), _bf16) for tm in {tm_x, tm_c}}
    sel_np = np.zeros((2 * SUBLANES, LANES), np.float32)
    for n in range(2):
        sel_np[n * GATE_PIECES:(n + 1) * GATE_PIECES, n] = 1.0
    sel = jnp.asarray(sel_np, _bf16)

    for i in range(depth):
        last = i == depth - 1
        mx = [mods[i, :bsz, j][:, None, :] for j in range(N_MOD)]
        mc = [jnp.broadcast_to(mods[i, bsz, j][None, None, :], (bsz, 1, D_MODEL)) for j in range(N_MOD)]
        w_in_b = w_in[i].astype(_bf16)
        qg = (jnp.tile(q_gain[i], NA_HEADS) * NA_HEAD_DIM ** -0.5)[None, :]
        kg = jnp.tile(k_gain[i], NA_HEADS)[None, :]
        ln_g = sgu_ln[i][None, :]
        ws_cat = jnp.concatenate([sgu_w[i, 0::2], sgu_w[i, 1::2]], axis=2).astype(_bf16)
        bs_tab = jnp.repeat(sgu_b[i].T, SGU_GROUP_DIM, axis=1)
        og = out_gain[i][None, :]
        w_out_b = w_out[i].astype(_bf16)
        pad_w = jnp.zeros((SUBLANES - EXPERTS_PER_GROUP, D_MODEL), _f32)
        pad_b = jnp.zeros((SUBLANES - EXPERTS_PER_GROUP,), _f32)
        w_r = jnp.concatenate([rg_w[i].T, pad_w] + [t for g in range(N_GROUPS) for t in (re_w[i, g].T, pad_w)]
                              + [jnp.zeros((ROUTER_ROWS - SUBLANES * (N_GROUPS + 1), D_MODEL), _f32)]).astype(_bf16)
        b_r = jnp.concatenate([rg_b[i], pad_b] + [t for g in range(N_GROUPS) for t in (re_b[i, g], pad_b)]
                              + [jnp.zeros((ROUTER_ROWS - SUBLANES * (N_GROUPS + 1),), _f32)])[:, None]
        score_bound = 1.02 * NA_HEAD_DIM ** 0.5 * jnp.max(jnp.abs(q_gain[i])) * jnp.max(jnp.abs(k_gain[i]))
        shift = (score_bound + jnp.maximum(jnp.max(rpb[i]), 0.0)).reshape(1)
        spread = 2.0 * score_bound + jnp.maximum(jnp.max(rpb[i]), 0.0) - jnp.minimum(jnp.min(rpb[i]), 0.0)

        q, k, v, gu, vn = _proj(x, mx[0], mx[1], w_in_b, avg, qg, kg, ln_g, tm_x)
        qc, kc, vc, guc, vnc = _proj(ctx, mc[0], mc[1], w_in_b, avg, qg, kg, ln_g, tm_c)
        o_na = lax.cond(
            spread < MAX_SHIFT_SPREAD,
            lambda: _attention(q, k, v, kc, vc, _window_col_table(rpb[i], shift[0]), win_plan, shift, True),
            lambda: _attention(q, k, v, kc, vc, _window_col_table(rpb[i], 0.0), win_plan, shift, False))
        x_new, xrows, meta, counts = _merge(x, o_na, gu, vn, mx[2], mx[3], mx[4], ws_cat, bs_tab, og,
                                            w_out_b, w_r, b_r, tri[tm_x], sel, tm_x)
        streams = [(x_new, xrows, meta, counts, mx[5], 1024)]
        if not last:
            oc_na = _ctx_attention(qc, kc, vc)
            c_new, crows, cmeta, ccounts = _merge(ctx, oc_na, guc, vnc, mc[2], mc[3], mc[4], ws_cat, bs_tab,
                                                  og, w_out_b, w_r, b_r, tri[tm_c], sel, tm_c)
            streams.append((c_new, crows, cmeta, ccounts, mc[5], tm_c))
        outs = _ffn(streams, w1, w3, w2, i, 512)
        x = outs[0]
        if not last:
            ctx = outs[1]
    return x
```

```python
import functools

import jax
import jax.numpy as jnp
import numpy as np
from jax import lax
from jax.experimental import pallas as pl
from jax.experimental.pallas import tpu as pltpu

D_MODEL = 1024
GRID_W = 64
NA_HEADS = 8
NA_HEAD_DIM = 64
NA_W = NA_HEADS * NA_HEAD_DIM
NA_KH = 8
NA_KW = 16
SGU_GROUPS = 8
SGU_W = D_MODEL - NA_W
SGU_GROUP_DIM = SGU_W // SGU_GROUPS
SGU_CHUNK = 128
IN_COLS = 3 * NA_W + 2 * SGU_W
N_GROUPS = 4
EXPERTS_PER_GROUP = 4
N_EXPERTS = N_GROUPS * EXPERTS_PER_GROUP
D_EXPERT = D_MODEL // 2
N_MOD = 6
EPS = 1e-6

LANES = 128
SUBLANES = 8
HEAD_PAIRS = NA_W // LANES
Q_ROWS = 8
Q_BLOCK = Q_ROWS * GRID_W
WIN_ROWS = Q_ROWS + NA_KH - 1
WIN_BLOCK = WIN_ROWS * GRID_W
PAIR_ROWS = 2
PAIR_BLOCK = PAIR_ROWS * GRID_W
PAIR_WIN_ROWS = PAIR_ROWS + NA_KH - 1
PAIR_WIN = PAIR_WIN_ROWS * GRID_W
PAIRS_IN_FLIGHT = 2
N_DR = 2 * NA_KH
N_PAIRS = 6
N_CLASSES = N_GROUPS * N_PAIRS
PAIR_A = (0, 0, 1, 1, 0, 2)
PAIR_B = (1, 2, 2, 3, 3, 3)
PAIR_OF_LEX = (0, 1, 4, 2, 3, 5)
CLASS_ROWS = 32
ROUTER_ROWS = 48
GATE_PIECES = 3
ROW_W = D_MODEL + LANES
MXU_TILE = 256
MASKED = -1e30
MAX_SHIFT_SPREAD = 60.0
VMEM_LIMIT_BYTES = 56 * 1024 * 1024

_f32 = jnp.float32
_bf16 = jnp.bfloat16


def _params(*semantics):
    return pltpu.CompilerParams(dimension_semantics=semantics, vmem_limit_bytes=VMEM_LIMIT_BYTES)


def _dot(a, b):
    return jnp.dot(a, b, preferred_element_type=_f32)


def _dot_nt(a, b):
    return lax.dot_general(a, b, (((1,), (1,)), ((), ())), preferred_element_type=_f32)


def _adaln_kernel(cond_ref, w_ref, b_ref, out_ref):
    cond = cond_ref[...]
    act = cond * jax.nn.sigmoid(cond)
    out_ref[...] = _dot(act.astype(_bf16), w_ref[...].astype(_bf16)) + b_ref[...]


def _adaln(cond, w_ada, b_ada):
    depth = w_ada.shape[0]
    cols = N_MOD * D_MODEL
    col_block = 1536
    return pl.pallas_call(
        _adaln_kernel,
        grid=(depth, cols // col_block),
        in_specs=[
            pl.BlockSpec((SUBLANES, D_MODEL), lambda d, j: (0, 0)),
            pl.BlockSpec((None, D_MODEL, col_block), lambda d, j: (d, 0, j)),
            pl.BlockSpec((None, 1, col_block), lambda d, j: (d, 0, j)),
        ],
        out_specs=pl.BlockSpec((None, SUBLANES, col_block), lambda d, j: (d, 0, j)),
        out_shape=jax.ShapeDtypeStruct((depth, SUBLANES, cols), _f32),
        compiler_params=_params("arbitrary", "arbitrary"),
        name="adaln",
    )(cond, w_ada, b_ada.reshape(depth, 1, cols))


def _proj_kernel(x_ref, sh_ref, sc_ref, w_ref, avg_ref, qg_ref, kg_ref, ln_ref,
                 q_ref, k_ref, v_ref, gu_ref, vn_ref):
    x = x_ref[...]
    h = x * lax.rsqrt(jnp.mean(x * x, -1, keepdims=True) + EPS) * (1.0 + sc_ref[...]) + sh_ref[...]
    p = _dot(h.astype(_bf16), w_ref[...])
    q = p[:, :NA_W]
    k = p[:, NA_W:2 * NA_W]
    v = p[:, 2 * NA_W:3 * NA_W]
    u = p[:, 3 * NA_W:3 * NA_W + SGU_W]
    vs = p[:, 3 * NA_W + SGU_W:]
    def head_mean(t):
        sq = (t * t).astype(_bf16)
        return jnp.concatenate([_dot(sq[:, c:c + MXU_TILE], avg_ref[...]) for c in range(0, NA_W, MXU_TILE)], axis=1)

    q_ms = head_mean(q)
    k_ms = head_mean(k)
    qn = (q * lax.rsqrt(q_ms + EPS) * qg_ref[...]).astype(_bf16)
    kn = (k * lax.rsqrt(k_ms + EPS) * kg_ref[...]).astype(_bf16)
    vb = v.astype(_bf16)
    for hp in range(HEAD_PAIRS):
        cols = slice(hp * LANES, (hp + 1) * LANES)
        q_ref[hp] = qn[:, cols]
        k_ref[hp] = kn[:, cols]
        v_ref[hp] = vb[:, cols]
    gu_ref[...] = jax.nn.gelu(u).astype(_bf16)
    gv = jax.nn.gelu(vs)
    mu = jnp.mean(gv, -1, keepdims=True)
    cen = gv - mu
    var = jnp.mean(cen * cen, -1, keepdims=True)
    vn_ref[...] = (cen * lax.rsqrt(var + EPS) * ln_ref[...]).astype(_bf16)


def _proj(x, sh, sc, w_in, avg, qg, kg, ln_g, tm):
    bsz, length, _ = x.shape
    vec = lambda n: pl.BlockSpec((1, n), lambda b, i: (0, 0))
    mod = pl.BlockSpec((None, 1, D_MODEL), lambda b, i: (b, 0, 0))
    tok = lambda n: pl.BlockSpec((None, tm, n), lambda b, i: (b, i, 0))
    hpm = pl.BlockSpec((None, HEAD_PAIRS, tm, LANES), lambda b, i: (b, 0, i, 0))
    hpm_shape = jax.ShapeDtypeStruct((bsz, HEAD_PAIRS, length, LANES), _bf16)
    sgu_shape = jax.ShapeDtypeStruct((bsz, length, SGU_W), _bf16)
    return pl.pallas_call(
        _proj_kernel,
        grid=(bsz, length // tm),
        in_specs=[tok(D_MODEL), mod, mod,
                  pl.BlockSpec((D_MODEL, IN_COLS), lambda b, i: (0, 0)),
                  pl.BlockSpec((MXU_TILE, MXU_TILE), lambda b, i: (0, 0)),
                  vec(NA_W), vec(NA_W), vec(SGU_W)],
        out_specs=[hpm, hpm, hpm, tok(SGU_W), tok(SGU_W)],
        out_shape=[hpm_shape, hpm_shape, hpm_shape, sgu_shape, sgu_shape],
        compiler_params=_params("arbitrary", "arbitrary"),
        name="proj",
    )(x, sh, sc, w_in, avg, qg, kg, ln_g)


def _attend(blocks, shift=None):
    add = lambda a, b: a + b
    m_rows = blocks[0][0].shape[0]
    lane = lax.broadcasted_iota(jnp.int32, blocks[0][0].shape, 1)
    low = lane < NA_HEAD_DIM
    stacked = [jnp.concatenate([jnp.where(low, q, jnp.zeros_like(q)), jnp.where(low, jnp.zeros_like(q), q)], axis=0)
               for q, _, _, _ in blocks]
    if shift is None:
        scores = [[_dot_nt(qs, key) if bias is None else _dot_nt(qs, key) + bias[...]
                   for key, bias in zip(keys, biases)] for qs, (_, keys, _, biases) in zip(stacked, blocks)]
        maxes = [functools.reduce(jnp.maximum, [jnp.max(s, -1, keepdims=True) for s in ss]) for ss in scores]
        probs = [[jnp.exp(s - m) for s in ss] for ss, m in zip(scores, maxes)]
    else:
        probs = [[jnp.exp(_dot_nt(qs, key) - shift) if bias is None else jnp.exp(_dot_nt(qs, key) + bias[...])
                  for key, bias in zip(keys, biases)] for qs, (_, keys, _, biases) in zip(stacked, blocks)]
    dens = [functools.reduce(add, [jnp.sum(p, -1, keepdims=True) for p in ps]) for ps in probs]
    accs = [functools.reduce(add, [_dot(p.astype(_bf16), val) for p, val in zip(ps, blk[2])])
            for ps, blk in zip(probs, blocks)]
    outs = [acc / den for acc, den in zip(accs, dens)]
    return [jnp.where(low, o2[:m_rows], o2[m_rows:]) for o2 in outs]


def _attn_kernel(off_ref, var_ref, shift_ref, q_ref, k_ref, v_ref, kc_ref, vc_ref, bycol_ref, o_ref, bias_ref,
                 *, pieces, shifted):
    i = pl.program_id(2)
    shift = shift_ref[0] if shifted else None

    @pl.when(i == 0)
    def _():
        for var, piece in enumerate(pieces):
            for hh in range(2):
                for j in range(PAIR_ROWS):
                    for kr in range(PAIR_WIN_ROWS):
                        lo = (kr % 2) * GRID_W
                        row = hh * PAIR_BLOCK + j * GRID_W
                        bias_ref[var, row:row + GRID_W, kr * GRID_W:(kr + 1) * GRID_W] = (
                            bycol_ref[hh, piece[j][kr], :, lo:lo + GRID_W])

    kc = kc_ref[...]
    vc = vc_ref[...]
    npair = Q_ROWS // PAIR_ROWS
    for g0 in range(0, npair, PAIRS_IN_FLIGHT):
        blocks = []
        for g in range(g0, g0 + PAIRS_IN_FLIGHT):
            start = pl.multiple_of(off_ref[i * npair + g] * GRID_W, GRID_W)
            var = var_ref[i * npair + g]
            blocks.append((q_ref[g * PAIR_BLOCK:(g + 1) * PAIR_BLOCK, :],
                           [k_ref[pl.ds(start, PAIR_WIN), :], kc],
                           [v_ref[pl.ds(start, PAIR_WIN), :], vc],
                           [bias_ref.at[var], None]))
        for g, out in zip(range(g0, g0 + PAIRS_IN_FLIGHT), _attend(blocks, shift)):
            o_ref[g * PAIR_BLOCK:(g + 1) * PAIR_BLOCK, :] = out.astype(o_ref.dtype)


def _attention(q, k, v, kc, vc, bycol, plan, shift, shifted):
    off_tab, var_tab, pieces = plan
    bsz, _, seq, _ = q.shape
    clen = kc.shape[2]
    nblk = seq // Q_BLOCK

    def win_start(i):
        return jnp.clip(i * Q_ROWS - NA_KH // 2, 0, seq // GRID_W - WIN_ROWS) * GRID_W

    qspec = pl.BlockSpec((None, None, Q_BLOCK, LANES), lambda hp, b, i, *_: (b, hp, i, 0))
    wspec = pl.BlockSpec((None, None, pl.Element(WIN_BLOCK), pl.Element(LANES)),
                         lambda hp, b, i, *_: (b, hp, win_start(i), 0))
    cspec = pl.BlockSpec((None, None, clen, LANES), lambda hp, b, i, *_: (b, hp, 0, 0))
    tspec = pl.BlockSpec((None, 2, N_DR, GRID_W, LANES), lambda hp, b, i, *_: (hp, 0, 0, 0, 0))
    grid_spec = pltpu.PrefetchScalarGridSpec(
        num_scalar_prefetch=3,
        grid=(HEAD_PAIRS, bsz, nblk),
        in_specs=[qspec, wspec, wspec, cspec, cspec, tspec],
        out_specs=qspec,
        scratch_shapes=[pltpu.VMEM((len(pieces), 2 * PAIR_BLOCK, PAIR_WIN), _f32)],
    )
    return pl.pallas_call(
        functools.partial(_attn_kernel, pieces=pieces, shifted=shifted),
        grid_spec=grid_spec,
        out_shape=jax.ShapeDtypeStruct(q.shape, _bf16),
        compiler_params=_params("arbitrary", "arbitrary", "arbitrary"),
        name="attention_shifted" if shifted else "attention",
    )(off_tab, var_tab, shift, q, k, v, kc, vc, bycol)


def _ctx_attn_kernel(q_ref, k_ref, v_ref, o_ref):
    o_ref[...] = _attend([(q_ref[...], [k_ref[...]], [v_ref[...]], [None])])[0].astype(o_ref.dtype)


def _ctx_attention(q, k, v):
    bsz, _, clen, _ = q.shape
    spec = pl.BlockSpec((None, None, clen, LANES), lambda b, hp: (b, hp, 0, 0))
    return pl.pallas_call(
        _ctx_attn_kernel,
        grid=(bsz, HEAD_PAIRS),
        in_specs=[spec, spec, spec],
        out_specs=spec,
        out_shape=jax.ShapeDtypeStruct(q.shape, _bf16),
        compiler_params=_params("arbitrary", "arbitrary"),
        name="ctx_attention",
    )(q, k, v)


def _window_plan(rows):
    patterns, off_tab, var_tab = {}, [], []
    for r in range(0, rows, PAIR_ROWS):
        ws = int(np.clip(r - NA_KH // 2, 0, rows - PAIR_WIN_ROWS))
        r0 = r // Q_ROWS * Q_ROWS
        block_ws = int(np.clip(r0 - NA_KH // 2, 0, rows - WIN_ROWS))
        assert 0 <= ws - block_ws <= WIN_ROWS - PAIR_WIN_ROWS
        piece = []
        for j in range(PAIR_ROWS):
            r_start = int(np.clip(r + j - NA_KH // 2, 0, rows - NA_KH))
            assert ws <= r_start and r_start + NA_KH <= ws + PAIR_WIN_ROWS
            piece.append(tuple(ws + kr - (r + j) + NA_KH - 1 if r_start <= ws + kr < r_start + NA_KH else N_DR - 1
                               for kr in range(PAIR_WIN_ROWS)))
        var_tab.append(patterns.setdefault(tuple(piece), len(patterns)))
        off_tab.append(ws - block_ws)
    return jnp.asarray(off_tab, jnp.int32), jnp.asarray(var_tab, jnp.int32), tuple(patterns)


def _window_col_table(rpb, shift):
    col = np.arange(GRID_W)
    col_start = np.clip(col - NA_KW // 2, 0, GRID_W - NA_KW)
    col_ok = (col[None, :] >= col_start[:, None]) & (col[None, :] < col_start[:, None] + NA_KW)
    dc = np.where(col_ok, col[None, :] - col[:, None] + NA_KW - 1, 0)
    onehot = np.zeros((2 * NA_KW - 1, GRID_W * GRID_W), np.float32)
    onehot[dc.reshape(-1), np.arange(GRID_W * GRID_W)] = col_ok.reshape(-1)
    by_col = jnp.dot(rpb.reshape(-1, 2 * NA_KW - 1), jnp.asarray(onehot), precision=lax.Precision.HIGHEST)
    by_col = by_col.reshape(NA_HEADS, 2 * NA_KH - 1, GRID_W, GRID_W)
    by_col = jnp.where(jnp.asarray(col_ok)[None, None], by_col - shift, MASKED)
    by_col = jnp.concatenate([by_col, jnp.full((NA_HEADS, 1, GRID_W, GRID_W), MASKED, _f32)], axis=1)
    by_col = jnp.concatenate([by_col, by_col], axis=-1)
    return by_col.reshape(HEAD_PAIRS, 2, N_DR, GRID_W, LANES)


def _merge_kernel(x_ref, o_ref, gu_ref, vn_ref, ga_ref, shf_ref, scf_ref, ws_ref, bs_ref, og_ref,
                  wout_ref, wr_ref, br_ref, tri_ref, sel_ref, row_ref, meta_ref, cnt_ref,
                  *, tm, mod_stride, mod_base):
    first = (pl.program_id(0) == 0) & (pl.program_id(1) == 0)

    @pl.when(first)
    def _():
        cnt_ref[...] = jnp.zeros_like(cnt_ref)

    nchunk = tm // SGU_CHUNK
    vn = vn_ref[...]
    lane = lax.broadcasted_iota(jnp.int32, (SGU_CHUNK, LANES), 1)
    low = lane < SGU_GROUP_DIM
    mixed_cols = []
    for gp in range(SGU_W // LANES):
        rhs_cols = []
        for n in range(nchunk):
            blk = vn[n * SGU_CHUNK:(n + 1) * SGU_CHUNK, gp * LANES:(gp + 1) * LANES]
            zero = jnp.zeros_like(blk)
            rhs_cols.append(jnp.concatenate([jnp.where(low, blk, zero), jnp.where(low, zero, blk)], axis=0))
        rhs = jnp.concatenate(rhs_cols, axis=1)
        mix = _dot(ws_ref[gp], rhs)
        mixed_cols.append(jnp.concatenate([mix[:, n * LANES:(n + 1) * LANES] for n in range(nchunk)], axis=0))
    mixed = jnp.concatenate(mixed_cols, axis=1)
    bias = jnp.concatenate([bs_ref[...]] * nchunk, axis=0)
    o_sg = gu_ref[...].astype(_f32) * (mixed + bias)
    o_na = jnp.concatenate([o_ref[hp].astype(_f32) for hp in range(HEAD_PAIRS)], axis=1)

    def rms(t):
        return t * lax.rsqrt(jnp.mean(t * t, -1, keepdims=True) + EPS)

    y = jnp.concatenate([rms(o_na), rms(o_sg)], axis=1) * og_ref[...]
    xn = x_ref[...] + ga_ref[...] * _dot(y.astype(_bf16), wout_ref[...])
    row_ref[:, :D_MODEL] = xn
    hf = rms(xn) * (1.0 + scf_ref[...]) + shf_ref[...]

    lg = _dot_nt(wr_ref[...], hf.astype(_bf16)) + br_ref[...]
    quad = lambda r: lg[r:r + EXPERTS_PER_GROUP, :]
    row4 = lax.broadcasted_iota(jnp.int32, (EXPERTS_PER_GROUP, tm), 0).astype(_f32)
    over = lambda fn, t: fn(t, 0, keepdims=True)
    gl = quad(0)
    gm = over(jnp.max, gl)
    p_top = 1.0 / over(jnp.sum, jnp.exp(gl - gm))
    g_idx = over(jnp.min, jnp.where(gl == gm, row4, float(N_GROUPS)))
    el = quad(SUBLANES * N_GROUPS)
    for g in range(N_GROUPS - 2, -1, -1):
        el = jnp.where(g_idx == float(g), quad(SUBLANES * (g + 1)), el)
    ee = jnp.exp(el - over(jnp.max, el))
    pe = ee / over(jnp.sum, ee)
    v1 = over(jnp.max, pe)
    i1 = over(jnp.min, jnp.where(pe == v1, row4, float(EXPERTS_PER_GROUP)))
    pe2 = jnp.where(row4 == i1, -1.0, pe)
    v2 = over(jnp.max, pe2)
    i2 = over(jnp.min, jnp.where(pe2 == v2, row4, float(EXPERTS_PER_GROUP)))
    den = v1 + v2
    w1 = v1 / den * p_top
    w2 = v2 / den * p_top
    first_low = i1 < i2
    a = jnp.minimum(i1, i2)
    b = jnp.maximum(i1, i2)
    lex = a * (7.0 - a) * 0.5 + (b - a - 1.0)
    pair = lex
    for src, dst in enumerate(PAIR_OF_LEX):
        if src != dst:
            pair = jnp.where(lex == float(src), float(dst), pair)
    cls = g_idx * float(N_PAIRS) + pair
    gate_a = jnp.where(first_low, w1, w2)
    gate_b = jnp.where(first_low, w2, w1)

    row_p = lax.broadcasted_iota(jnp.int32, (2 * SUBLANES, tm), 0)
    mod_id = (pl.program_id(0) * mod_stride + mod_base).astype(_f32)
    stacked = jnp.where(row_p == 2 * GATE_PIECES, mod_id, jnp.zeros((2 * SUBLANES, tm), _f32))
    for n, gate in enumerate((gate_a, gate_b)):
        rest = gate
        for k in range(GATE_PIECES):
            piece = rest.astype(_bf16).astype(_f32)
            stacked = jnp.where(row_p == n * GATE_PIECES + k, piece, stacked)
            rest = rest - piece
    row_ref[:, D_MODEL:] = lax.dot_general(stacked.astype(_bf16), sel_ref[...],
                                           (((0,), (0,)), ((), ())), preferred_element_type=_f32)

    row_c = lax.broadcasted_iota(jnp.int32, (CLASS_ROWS, tm), 0).astype(_f32)
    onehot = row_c == cls
    ones = jnp.where(onehot, 1.0, 0.0)
    before = _dot(ones.astype(_bf16), tri_ref[...])
    rank = over(jnp.sum, jnp.where(onehot, before + cnt_ref[:, 0:1], 0.0))
    cnt_ref[...] += jnp.sum(ones, 1, keepdims=True)
    meta_ref[...] = jnp.concatenate([cls, rank, jnp.zeros((SUBLANES - 2, tm), _f32)], axis=0)


def _merge(x, o_na, gu, vn, g_a, sh_f, sc_f, ws_cat, bs_tab, out_gain, w_out, w_r, b_r, tri, sel, tm,
           mod_stride, mod_base):
    bsz, length, _ = x.shape
    mod = pl.BlockSpec((None, 1, D_MODEL), lambda b, i: (b, 0, 0))
    tok = lambda n: pl.BlockSpec((None, tm, n), lambda b, i: (b, i, 0))
    const = lambda *shape: pl.BlockSpec(shape, lambda b, i: (0,) * len(shape))
    return pl.pallas_call(
        functools.partial(_merge_kernel, tm=tm, mod_stride=mod_stride, mod_base=mod_base),
        grid=(bsz, length // tm),
        in_specs=[tok(D_MODEL),
                  pl.BlockSpec((None, HEAD_PAIRS, tm, LANES), lambda b, i: (b, 0, i, 0)),
                  tok(SGU_W), tok(SGU_W), mod, mod, mod,
                  const(SGU_W // LANES, SGU_CHUNK, 2 * SGU_CHUNK),
                  const(SGU_CHUNK, SGU_W),
                  const(1, D_MODEL),
                  const(D_MODEL, D_MODEL),
                  const(ROUTER_ROWS, D_MODEL),
                  const(ROUTER_ROWS, 1),
                  const(tm, tm),
                  const(2 * SUBLANES, LANES)],
        out_specs=[tok(ROW_W),
                   pl.BlockSpec((None, SUBLANES, tm), lambda b, i: (b, 0, i)),
                   const(CLASS_ROWS, LANES)],
        out_shape=[jax.ShapeDtypeStruct((bsz, length, ROW_W), _f32),
                   jax.ShapeDtypeStruct((bsz, SUBLANES, length), _f32),
                   jax.ShapeDtypeStruct((CLASS_ROWS, LANES), _f32)],
        compiler_params=_params("arbitrary", "arbitrary"),
        name="merge",
    )(x, o_na, gu, vn, g_a, sh_f, sc_f, ws_cat, bs_tab, out_gain, w_out, w_r, b_r, tri, sel)


def _issue_row_copies(count, make_copy):
    for j in range(count):
        make_copy(j).start(priority=j % 2)


def _dispatch_kernel(fill_ref, pos_ref, rows_ref, xs_ref, zero_ref, sem, fill_sem, *, td, tile):
    @pl.when(pl.program_id(0) == 0)
    def _():
        zero_ref[...] = jnp.zeros_like(zero_ref)

        def fill(k):
            start = pl.multiple_of(jnp.maximum(fill_ref[k], 0), tile)
            return pltpu.make_async_copy(zero_ref, xs_ref.at[pl.ds(start, tile)], fill_sem)

        for k in range(fill_ref.shape[0]):
            @pl.when(fill_ref[k] >= 0)
            def _():
                fill(k).start()
        for k in range(fill_ref.shape[0]):
            @pl.when(fill_ref[k] >= 0)
            def _():
                fill(k).wait()

    _scatter_rows(pos_ref, rows_ref, xs_ref, sem, td)


def _scatter_rows(pos_ref, rows_ref, xs_ref, sem, td):
    _issue_row_copies(td, lambda j: pltpu.make_async_copy(
        rows_ref.at[pl.ds(j, 1)], xs_ref.at[pl.ds(pos_ref[0, j], 1)], sem))
    pltpu.make_async_copy(rows_ref, xs_ref.at[pl.ds(0, td)], sem).wait()


def _dispatch_more_kernel(pos_ref, rows_ref, base_ref, xs_ref, sem, *, td):
    del base_ref
    _scatter_rows(pos_ref, rows_ref, xs_ref, sem, td)


def _dispatch(rows, pos, fill, padded, td, tile):
    n = rows.shape[0]
    grid_spec = pltpu.PrefetchScalarGridSpec(
        num_scalar_prefetch=1,
        grid=(n // td,),
        in_specs=[pl.BlockSpec((None, 1, td), lambda i, fl: (i, 0, 0), memory_space=pltpu.SMEM),
                  pl.BlockSpec((td, ROW_W), lambda i, fl: (i, 0))],
        out_specs=pl.BlockSpec(memory_space=pl.ANY),
        scratch_shapes=[pltpu.VMEM((tile, ROW_W), _f32), pltpu.SemaphoreType.DMA(()), pltpu.SemaphoreType.DMA(())],
    )
    return pl.pallas_call(
        functools.partial(_dispatch_kernel, td=td, tile=tile),
        grid_spec=grid_spec,
        out_shape=jax.ShapeDtypeStruct((padded, ROW_W), _f32),
        compiler_params=_params("arbitrary"),
        name="dispatch",
    )(fill, pos.reshape(n // td, 1, td), rows)


def _dispatch_more(xs, rows, pos, td):
    n = rows.shape[0]
    return pl.pallas_call(
        functools.partial(_dispatch_more_kernel, td=td),
        grid=(n // td,),
        in_specs=[pl.BlockSpec((None, 1, td), lambda i: (i, 0, 0), memory_space=pltpu.SMEM),
                  pl.BlockSpec((td, ROW_W), lambda i: (i, 0)),
                  pl.BlockSpec(memory_space=pl.ANY)],
        out_specs=pl.BlockSpec(memory_space=pl.ANY),
        out_shape=jax.ShapeDtypeStruct(xs.shape, _f32),
        scratch_shapes=[pltpu.SemaphoreType.DMA(())],
        input_output_aliases={2: 0},
        compiler_params=_params("arbitrary"),
        name="dispatch_more",
    )(pos.reshape(n // td, 1, td), rows, xs)


def _combine_kernel(pos_ref, npos_ref, x_ref, g_ref, ys_ref, o_ref, buf0, buf1, sems, *, tc, nsteps):
    s = pl.program_id(0) * pl.num_programs(1) + pl.program_id(1)
    bufs = (buf0, buf1)

    def request(idx_ref, slot):
        _issue_row_copies(tc, lambda j: pltpu.make_async_copy(
            ys_ref.at[pl.ds(idx_ref[0, j], 1)], bufs[slot].at[pl.ds(j, 1)], sems.at[slot]))

    @pl.when(s == 0)
    def _():
        request(pos_ref, 0)

    for slot in range(2):
        @pl.when((s % 2 == slot) & (s + 1 < nsteps))
        def _():
            request(npos_ref, 1 - slot)

        @pl.when(s % 2 == slot)
        def _():
            pltpu.make_async_copy(ys_ref.at[pl.ds(0, tc)], bufs[slot], sems.at[slot]).wait()
            o_ref[...] = x_ref[...] + g_ref[...] * bufs[slot][...]


def _combine(x, ys, pos, g, tc):
    bsz, length, _ = x.shape
    per_batch = length // tc
    nsteps = bsz * per_batch
    tok = pl.BlockSpec((None, tc, D_MODEL), lambda b, i: (b, i, 0))
    pos3 = pos.reshape(nsteps, 1, tc)
    return pl.pallas_call(
        functools.partial(_combine_kernel, tc=tc, nsteps=nsteps),
        grid=(bsz, per_batch),
        in_specs=[pl.BlockSpec((None, 1, tc), lambda b, i: (b * per_batch + i, 0, 0), memory_space=pltpu.SMEM),
                  pl.BlockSpec((None, 1, tc), lambda b, i: (jnp.minimum(b * per_batch + i + 1, nsteps - 1), 0, 0),
                               memory_space=pltpu.SMEM),
                  tok,
                  pl.BlockSpec((None, 1, D_MODEL), lambda b, i: (b, 0, 0)),
                  pl.BlockSpec(memory_space=pl.ANY)],
        out_specs=tok,
        out_shape=jax.ShapeDtypeStruct((bsz, length, D_MODEL), _f32),
        scratch_shapes=[pltpu.VMEM((tc, D_MODEL), _f32), pltpu.VMEM((tc, D_MODEL), _f32),
                        pltpu.SemaphoreType.DMA((2,))],
        compiler_params=_params("arbitrary", "arbitrary"),
        name="combine",
    )(pos3, pos3, x, g, ys)


def _experts_kernel(ea_ref, eb_ref, chg_ref, nlive_ref, xs_ref, shf_ref, scf_ref, w1_ref, w3_ref, w2_ref, y_ref,
                    b1, b3, b2, s1, s3, s2, sems, *, layer, nmods):
    t = pl.program_id(0)
    live = t < nlive_ref[0]
    first = (ea_ref[t] // EXPERTS_PER_GROUP) * EXPERTS_PER_GROUP

    def stage(k, slot):
        return [pltpu.make_async_copy(w.at[layer, first + k], s.at[slot], sems.at[slot])
                for w, s in ((w1_ref, s1), (w3_ref, s3), (w2_ref, s2))]

    @pl.when(live & ((chg_ref[t] & 1) != 0))
    def _():
        for k in range(2):
            for cp in stage(k, k):
                cp.start()
        for k in range(EXPERTS_PER_GROUP):
            slot = k % 2
            for cp in stage(k, slot):
                cp.wait()
            b1[k] = s1[slot].astype(_bf16)
            b3[k] = s3[slot].astype(_bf16)
            b2[k] = s2[slot].astype(_bf16)
            if k + 2 < EXPERTS_PER_GROUP:
                for cp in stage(k + 2, slot):
                    cp.start()

    def run(nrows):
        x = xs_ref[:nrows, :D_MODEL]
        gates = xs_ref[:nrows, D_MODEL:]
        mod_id = gates[:, 2:3]
        shift, scale = shf_ref[0:1, :], scf_ref[0:1, :]
        for m in range(1, nmods):
            shift = jnp.where(mod_id == float(m), shf_ref[m:m + 1, :], shift)
            scale = jnp.where(mod_id == float(m), scf_ref[m:m + 1, :], scale)
        xs = (x * lax.rsqrt(jnp.mean(x * x, -1, keepdims=True) + EPS) * (1.0 + scale) + shift).astype(_bf16)

        def expert(e):
            hid = jax.nn.silu(_dot(xs, b1[e])) * _dot(xs, b3[e])
            return _dot(hid.astype(_bf16), b2[e])

        y_ref[:nrows, :] = (gates[:, 0:1] * expert(ea_ref[t] - first) + gates[:, 1:2] * expert(eb_ref[t] - first))

    half_only = (chg_ref[t] & 4) != 0
    half = y_ref.shape[0] // 2

    @pl.when(live & jnp.logical_not(half_only))
    def _():
        run(y_ref.shape[0])

    @pl.when(live & half_only)
    def _():
        run(half)
        y_ref[half:, :] = jnp.zeros((half, D_MODEL), _f32)

    @pl.when(jnp.logical_not(live))
    def _():
        y_ref[...] = jnp.zeros_like(y_ref)


def _experts(xs, sh_tab, sc_tab, nmods, tile_ea, tile_eb, chg, nlive, w1, w3, w2, layer, tile):
    ntiles = xs.shape[0] // tile
    hbm = pl.BlockSpec(memory_space=pl.ANY)
    tab = pl.BlockSpec((SUBLANES, D_MODEL), lambda t, ea, eb, cg, nl: (0, 0))
    up, down = (D_MODEL, D_EXPERT), (D_EXPERT, D_MODEL)
    grid_spec = pltpu.PrefetchScalarGridSpec(
        num_scalar_prefetch=4,
        grid=(ntiles,),
        in_specs=[pl.BlockSpec((tile, ROW_W), lambda t, ea, eb, cg, nl: (t, 0)), tab, tab, hbm, hbm, hbm],
        out_specs=pl.BlockSpec((tile, D_MODEL), lambda t, ea, eb, cg, nl: (t, 0)),
        scratch_shapes=[pltpu.VMEM((EXPERTS_PER_GROUP,) + up, _bf16), pltpu.VMEM((EXPERTS_PER_GROUP,) + up, _bf16),
                        pltpu.VMEM((EXPERTS_PER_GROUP,) + down, _bf16),
                        pltpu.VMEM((2,) + up, _f32), pltpu.VMEM((2,) + up, _f32), pltpu.VMEM((2,) + down, _f32),
                        pltpu.SemaphoreType.DMA((2,))],
    )
    return pl.pallas_call(
        functools.partial(_experts_kernel, layer=layer, nmods=nmods),
        grid_spec=grid_spec,
        out_shape=jax.ShapeDtypeStruct((xs.shape[0], D_MODEL), _f32),
        compiler_params=_params("arbitrary"),
        name="experts",
    )(tile_ea, tile_eb, chg, nlive, xs, sh_tab, sc_tab, w1, w3, w2)


def _dispatch_plan(cls, rank, counts, first_counts, later, tile):
    n = cls.shape[0]
    padded = -(-(n + N_CLASSES * (tile - 1)) // tile) * tile
    ntiles = padded // tile
    pcounts = (counts + tile - 1) // tile * tile
    pad_end = jnp.cumsum(pcounts)
    pad_off = pad_end - pcounts
    pos = pad_off[cls] + rank
    nlive = pad_end[-1] // tile
    tile_start = jnp.arange(ntiles, dtype=jnp.int32) * tile
    tile_cls = jnp.sum((pad_end[None, :] <= jnp.minimum(tile_start, pad_end[-1] - 1)[:, None]).astype(jnp.int32), axis=1)
    tile_cls = jnp.minimum(tile_cls, N_CLASSES - 1)
    grp = tile_cls // N_PAIRS
    pair = tile_cls % N_PAIRS
    tile_ea = grp * EXPERTS_PER_GROUP + jnp.asarray(PAIR_A, jnp.int32)[pair]
    tile_eb = grp * EXPERTS_PER_GROUP + jnp.asarray(PAIR_B, jnp.int32)[pair]
    new_group = jnp.concatenate([jnp.ones((1,), jnp.int32), (grp[1:] != grp[:-1]).astype(jnp.int32)])
    tokens_in_tile = (pad_off + counts)[tile_cls] - tile_start
    chg = new_group + 4 * (tokens_in_tile <= tile // 2).astype(jnp.int32)
    per_class = -(-(later + tile - 1) // tile) + 1
    uncovered = pad_off + first_counts
    tails = []
    for j in range(per_class):
        start = pad_end - (j + 1) * tile
        tails.append(jnp.where((start >= pad_off) & (start + tile > uncovered), start, -1))
    past = (nlive + jnp.arange(N_CLASSES, dtype=jnp.int32)) * tile
    fill = jnp.concatenate(tails + [jnp.where(past < padded, past, -1)]).astype(jnp.int32)
    return pos.astype(jnp.int32), padded, tile_ea, tile_eb, chg, nlive.astype(jnp.int32).reshape(1), fill


def _ffn(streams, sh_tab, sc_tab, nmods, w1, w3, w2, layer, tile):
    sizes = [s[0].shape[0] * s[0].shape[1] for s in streams]
    cls = [s[1][:, 0, :].reshape(n).astype(jnp.int32) for s, n in zip(streams, sizes)]
    cnts = [s[2][:N_CLASSES, 0].astype(jnp.int32) for s in streams]
    ranks, before = [], jnp.zeros((N_CLASSES,), jnp.int32)
    for s, n, c, cnt in zip(streams, sizes, cls, cnts):
        ranks.append(s[1][:, 1, :].reshape(n).astype(jnp.int32) + before[c])
        before = before + cnt
    pos, padded, tile_ea, tile_eb, chg, nlive, fill = _dispatch_plan(
        jnp.concatenate(cls), jnp.concatenate(ranks), before, cnts[0], sum(sizes[1:]), tile)
    starts = np.cumsum([0] + sizes)
    pos = [pos[a:b] for a, b in zip(starts[:-1], starts[1:])]
    xs = _dispatch(streams[0][0].reshape(sizes[0], ROW_W), pos[0], fill, padded, streams[0][4], tile)
    for s, n, p in zip(streams[1:], sizes[1:], pos[1:]):
        xs = _dispatch_more(xs, s[0].reshape(n, ROW_W), p, s[4])
    ys = _experts(xs, sh_tab, sc_tab, nmods, tile_ea, tile_eb, chg, nlive, w1, w3, w2, layer, tile)
    return [_combine(s[0], ys, p, s[3], s[4]) for s, p in zip(streams, pos)]


def kernel(x, c, ctx, c_ctx, w_ada, b_ada, w_in, q_gain, k_gain, rpb, sgu_ln, sgu_w, sgu_b, out_gain,
           w_out, rg_w, rg_b, re_w, re_b, w1, w3, w2):
    bsz, seq, _ = x.shape
    depth = w_ada.shape[0]
    rows = seq // GRID_W
    assert seq % Q_BLOCK == 0 and rows // Q_ROWS >= 3 and ctx.shape[1] % SGU_CHUNK == 0 and bsz < SUBLANES
    tm_x, tm_c = 512, ctx.shape[1]

    cond = jnp.zeros((SUBLANES, D_MODEL), _f32).at[:bsz].set(c).at[bsz].set(c_ctx)
    mods = _adaln(cond, w_ada, b_ada).reshape(depth, SUBLANES, N_MOD, D_MODEL)

    assert MXU_TILE % NA_HEAD_DIM == 0 and NA_W % MXU_TILE == 0
    head_id = np.arange(MXU_TILE) // NA_HEAD_DIM
    avg = jnp.asarray((head_id[:, None] == head_id[None, :]).astype(np.float32) / NA_HEAD_DIM, _bf16)
    win_plan = _window_plan(rows)
    tri = {tm: jnp.asarray(np.triu(np.ones((tm, tm), np.float32), 1), _bf16) for tm in {tm_x, tm_c}}
    sel_np = np.zeros((2 * SUBLANES, LANES), np.float32)
    for n in range(2):
        sel_np[n * GATE_PIECES:(n + 1) * GATE_PIECES, n] = 1.0
    sel_np[2 * GATE_PIECES, 2] = 1.0
    sel = jnp.asarray(sel_np, _bf16)

    for i in range(depth):
        last = i == depth - 1
        mx = [mods[i, :bsz, j][:, None, :] for j in range(N_MOD)]
        mc = [jnp.broadcast_to(mods[i, bsz, j][None, None, :], (bsz, 1, D_MODEL)) for j in range(N_MOD)]
        w_in_b = w_in[i].astype(_bf16)
        qg = (jnp.tile(q_gain[i], NA_HEADS) * NA_HEAD_DIM ** -0.5)[None, :]
        kg = jnp.tile(k_gain[i], NA_HEADS)[None, :]
        ln_g = sgu_ln[i][None, :]
        ws_cat = jnp.concatenate([sgu_w[i, 0::2], sgu_w[i, 1::2]], axis=2).astype(_bf16)
        bs_tab = jnp.repeat(sgu_b[i].T, SGU_GROUP_DIM, axis=1)
        og = out_gain[i][None, :]
        w_out_b = w_out[i].astype(_bf16)
        pad_w = jnp.zeros((SUBLANES - EXPERTS_PER_GROUP, D_MODEL), _f32)
        pad_b = jnp.zeros((SUBLANES - EXPERTS_PER_GROUP,), _f32)
        w_r = jnp.concatenate([rg_w[i].T, pad_w] + [t for g in range(N_GROUPS) for t in (re_w[i, g].T, pad_w)]
                              + [jnp.zeros((ROUTER_ROWS - SUBLANES * (N_GROUPS + 1), D_MODEL), _f32)]).astype(_bf16)
        b_r = jnp.concatenate([rg_b[i], pad_b] + [t for g in range(N_GROUPS) for t in (re_b[i, g], pad_b)]
                              + [jnp.zeros((ROUTER_ROWS - SUBLANES * (N_GROUPS + 1),), _f32)])[:, None]
        score_bound = 1.02 * NA_HEAD_DIM ** 0.5 * jnp.max(jnp.abs(q_gain[i])) * jnp.max(jnp.abs(k_gain[i]))
        shift = (score_bound + jnp.maximum(jnp.max(rpb[i]), 0.0)).reshape(1)
        spread = 2.0 * score_bound + jnp.maximum(jnp.max(rpb[i]), 0.0) - jnp.minimum(jnp.min(rpb[i]), 0.0)

        q, k, v, gu, vn = _proj(x, mx[0], mx[1], w_in_b, avg, qg, kg, ln_g, tm_x)
        qc, kc, vc, guc, vnc = _proj(ctx, mc[0], mc[1], w_in_b, avg, qg, kg, ln_g, tm_c)
        o_na = lax.cond(
            spread < MAX_SHIFT_SPREAD,
            lambda: _attention(q, k, v, kc, vc, _window_col_table(rpb[i], shift[0]), win_plan, shift, True),
            lambda: _attention(q, k, v, kc, vc, _window_col_table(rpb[i], 0.0), win_plan, shift, False))
        xrows, meta, counts = _merge(x, o_na, gu, vn, mx[2], mx[3], mx[4], ws_cat, bs_tab, og,
                                     w_out_b, w_r, b_r, tri[tm_x], sel, tm_x, 1, 0)
        streams = [(xrows, meta, counts, mx[5], 1024)]
        if not last:
            oc_na = _ctx_attention(qc, kc, vc)
            crows, cmeta, ccounts = _merge(ctx, oc_na, guc, vnc, mc[2], mc[3], mc[4], ws_cat, bs_tab,
                                           og, w_out_b, w_r, b_r, tri[tm_c], sel, tm_c, 0, bsz)
            streams.append((crows, cmeta, ccounts, mc[5], tm_c))
        outs = _ffn(streams, mods[i, :, 3, :], mods[i, :, 4, :], bsz + 1, w1, w3, w2, i, 512)
        x = outs[0]
        if not last:
            ctx = outs[1]
    return x
```

```python
import functools

import jax
import jax.numpy as jnp
import numpy as np
from jax import lax
from jax.experimental import pallas as pl
from jax.experimental.pallas import tpu as pltpu

D_MODEL = 1024
GRID_W = 64
NA_HEADS = 8
NA_HEAD_DIM = 64
NA_W = NA_HEADS * NA_HEAD_DIM
NA_KH = 8
NA_KW = 16
SGU_GROUPS = 8
SGU_W = D_MODEL - NA_W
SGU_GROUP_DIM = SGU_W // SGU_GROUPS
SGU_CHUNK = 128
IN_COLS = 3 * NA_W + 2 * SGU_W
N_GROUPS = 4
EXPERTS_PER_GROUP = 4
N_EXPERTS = N_GROUPS * EXPERTS_PER_GROUP
D_EXPERT = D_MODEL // 2
N_MOD = 6
EPS = 1e-6

LANES = 128
SUBLANES = 8
HEAD_PAIRS = NA_W // LANES
Q_ROWS = 8
Q_BLOCK = Q_ROWS * GRID_W
WIN_ROWS = Q_ROWS + NA_KH - 1
WIN_BLOCK = WIN_ROWS * GRID_W
PAIR_ROWS = 2
PAIR_BLOCK = PAIR_ROWS * GRID_W
PAIR_WIN_ROWS = PAIR_ROWS + NA_KH - 1
PAIR_WIN = PAIR_WIN_ROWS * GRID_W
PAIRS_IN_FLIGHT = 2
N_DR = 2 * NA_KH
N_PAIRS = 6
N_CLASSES = N_GROUPS * N_PAIRS
PAIR_A = (0, 0, 1, 1, 0, 2)
PAIR_B = (1, 2, 2, 3, 3, 3)
PAIR_OF_LEX = (0, 1, 4, 2, 3, 5)
CLASS_ROWS = 32
ROUTER_ROWS = 48
GATE_PIECES = 3
ROW_W = D_MODEL + LANES
MXU_TILE = 256
MASKED = -1e30
MAX_SHIFT_SPREAD = 60.0
VMEM_LIMIT_BYTES = 56 * 1024 * 1024

_f32 = jnp.float32
_bf16 = jnp.bfloat16


def _params(*semantics):
    return pltpu.CompilerParams(dimension_semantics=semantics, vmem_limit_bytes=VMEM_LIMIT_BYTES)


def _dot(a, b):
    return jnp.dot(a, b, preferred_element_type=_f32)


def _dot_nt(a, b):
    return lax.dot_general(a, b, (((1,), (1,)), ((), ())), preferred_element_type=_f32)


def _adaln_kernel(cond_ref, w_ref, b_ref, out_ref):
    cond = cond_ref[...]
    act = cond * jax.nn.sigmoid(cond)
    out_ref[...] = _dot(act.astype(_bf16), w_ref[...].astype(_bf16)) + b_ref[...]


def _adaln(cond, w_ada, b_ada):
    depth = w_ada.shape[0]
    cols = N_MOD * D_MODEL
    col_block = 1536
    return pl.pallas_call(
        _adaln_kernel,
        grid=(depth, cols // col_block),
        in_specs=[
            pl.BlockSpec((SUBLANES, D_MODEL), lambda d, j: (0, 0)),
            pl.BlockSpec((None, D_MODEL, col_block), lambda d, j: (d, 0, j)),
            pl.BlockSpec((None, 1, col_block), lambda d, j: (d, 0, j)),
        ],
        out_specs=pl.BlockSpec((None, SUBLANES, col_block), lambda d, j: (d, 0, j)),
        out_shape=jax.ShapeDtypeStruct((depth, SUBLANES, cols), _f32),
        compiler_params=_params("arbitrary", "arbitrary"),
        name="adaln",
    )(cond, w_ada, b_ada.reshape(depth, 1, cols))


def _proj_kernel(x_ref, sh_ref, sc_ref, w_ref, avg_ref, qg_ref, kg_ref, ln_ref,
                 q_ref, k_ref, v_ref, gu_ref, vn_ref):
    x = x_ref[...]
    h = x * lax.rsqrt(jnp.mean(x * x, -1, keepdims=True) + EPS) * (1.0 + sc_ref[...]) + sh_ref[...]
    p = _dot(h.astype(_bf16), w_ref[...])
    u = p[:, :SGU_W]
    vs = p[:, SGU_W:2 * SGU_W]
    q = p[:, 2 * SGU_W:2 * SGU_W + NA_W]
    k = p[:, 2 * SGU_W + NA_W:2 * SGU_W + 2 * NA_W]
    v = p[:, 2 * SGU_W + 2 * NA_W:]
    def head_mean(t):
        sq = (t * t).astype(_bf16)
        return jnp.concatenate([_dot(sq[:, c:c + MXU_TILE], avg_ref[...]) for c in range(0, NA_W, MXU_TILE)], axis=1)

    q_ms = head_mean(q)
    k_ms = head_mean(k)
    qn = (q * lax.rsqrt(q_ms + EPS) * qg_ref[...]).astype(_bf16)
    kn = (k * lax.rsqrt(k_ms + EPS) * kg_ref[...]).astype(_bf16)
    vb = v.astype(_bf16)
    for hp in range(HEAD_PAIRS):
        cols = slice(hp * LANES, (hp + 1) * LANES)
        q_ref[hp] = qn[:, cols]
        k_ref[hp] = kn[:, cols]
        v_ref[hp] = vb[:, cols]
    gu_ref[...] = jax.nn.gelu(u).astype(_bf16)
    gv = jax.nn.gelu(vs)
    mu = jnp.mean(gv, -1, keepdims=True)
    cen = gv - mu
    var = jnp.mean(cen * cen, -1, keepdims=True)
    vn_ref[...] = (cen * lax.rsqrt(var + EPS) * ln_ref[...]).astype(_bf16)


def _proj(x, sh, sc, w_in, avg, qg, kg, ln_g, tm):
    bsz, length, _ = x.shape
    vec = lambda n: pl.BlockSpec((1, n), lambda b, i: (0, 0))
    mod = pl.BlockSpec((None, 1, D_MODEL), lambda b, i: (b, 0, 0))
    tok = lambda n: pl.BlockSpec((None, tm, n), lambda b, i: (b, i, 0))
    hpm = pl.BlockSpec((None, HEAD_PAIRS, tm, LANES), lambda b, i: (b, 0, i, 0))
    hpm_shape = jax.ShapeDtypeStruct((bsz, HEAD_PAIRS, length, LANES), _bf16)
    sgu_shape = jax.ShapeDtypeStruct((bsz, length, SGU_W), _bf16)
    return pl.pallas_call(
        _proj_kernel,
        grid=(bsz, length // tm),
        in_specs=[tok(D_MODEL), mod, mod,
                  pl.BlockSpec((D_MODEL, IN_COLS), lambda b, i: (0, 0)),
                  pl.BlockSpec((MXU_TILE, MXU_TILE), lambda b, i: (0, 0)),
                  vec(NA_W), vec(NA_W), vec(SGU_W)],
        out_specs=[hpm, hpm, hpm, tok(SGU_W), tok(SGU_W)],
        out_shape=[hpm_shape, hpm_shape, hpm_shape, sgu_shape, sgu_shape],
        compiler_params=_params("arbitrary", "arbitrary"),
        name="proj",
    )(x, sh, sc, w_in, avg, qg, kg, ln_g)


def _attend(blocks, shift=None):
    add = lambda a, b: a + b
    m_rows = blocks[0][0].shape[0]
    lane = lax.broadcasted_iota(jnp.int32, blocks[0][0].shape, 1)
    low = lane < NA_HEAD_DIM
    stacked = [jnp.concatenate([jnp.where(low, q, jnp.zeros_like(q)), jnp.where(low, jnp.zeros_like(q), q)], axis=0)
               for q, _, _, _ in blocks]
    if shift is None:
        scores = [[_dot_nt(qs, key) if bias is None else _dot_nt(qs, key) + bias[...]
                   for key, bias in zip(keys, biases)] for qs, (_, keys, _, biases) in zip(stacked, blocks)]
        maxes = [functools.reduce(jnp.maximum, [jnp.max(s, -1, keepdims=True) for s in ss]) for ss in scores]
        probs = [[jnp.exp(s - m) for s in ss] for ss, m in zip(scores, maxes)]
    else:
        probs = [[jnp.exp(_dot_nt(qs, key) - shift) if bias is None else jnp.exp(_dot_nt(qs, key) + bias[...])
                  for key, bias in zip(keys, biases)] for qs, (_, keys, _, biases) in zip(stacked, blocks)]
    dens = [functools.reduce(add, [jnp.sum(p, -1, keepdims=True) for p in ps]) for ps in probs]
    accs = [functools.reduce(add, [_dot(p.astype(_bf16), val) for p, val in zip(ps, blk[2])])
            for ps, blk in zip(probs, blocks)]
    outs = [acc / den for acc, den in zip(accs, dens)]
    return [jnp.where(low, o2[:m_rows], o2[m_rows:]) for o2 in outs]


def _attn_kernel(off_ref, var_ref, shift_ref, q_ref, k_ref, v_ref, kc_ref, vc_ref, bycol_ref, o_ref, bias_ref,
                 *, pieces, shifted):
    i = pl.program_id(2)
    shift = shift_ref[0] if shifted else None

    @pl.when(i == 0)
    def _():
        for var, piece in enumerate(pieces):
            for hh in range(2):
                for j in range(PAIR_ROWS):
                    for kr in range(PAIR_WIN_ROWS):
                        lo = (kr % 2) * GRID_W
                        row = hh * PAIR_BLOCK + j * GRID_W
                        bias_ref[var, row:row + GRID_W, kr * GRID_W:(kr + 1) * GRID_W] = (
                            bycol_ref[hh, piece[j][kr], :, lo:lo + GRID_W])

    kc = kc_ref[...]
    vc = vc_ref[...]
    npair = Q_ROWS // PAIR_ROWS
    for g0 in range(0, npair, PAIRS_IN_FLIGHT):
        blocks = []
        for g in range(g0, g0 + PAIRS_IN_FLIGHT):
            start = pl.multiple_of(off_ref[i * npair + g] * GRID_W, GRID_W)
            var = var_ref[i * npair + g]
            blocks.append((q_ref[g * PAIR_BLOCK:(g + 1) * PAIR_BLOCK, :],
                           [k_ref[pl.ds(start, PAIR_WIN), :], kc],
                           [v_ref[pl.ds(start, PAIR_WIN), :], vc],
                           [bias_ref.at[var], None]))
        for g, out in zip(range(g0, g0 + PAIRS_IN_FLIGHT), _attend(blocks, shift)):
            o_ref[g * PAIR_BLOCK:(g + 1) * PAIR_BLOCK, :] = out.astype(o_ref.dtype)


def _attention(q, k, v, kc, vc, bycol, plan, shift, shifted):
    off_tab, var_tab, pieces = plan
    bsz, _, seq, _ = q.shape
    clen = kc.shape[2]
    nblk = seq // Q_BLOCK

    def win_start(i):
        return jnp.clip(i * Q_ROWS - NA_KH // 2, 0, seq // GRID_W - WIN_ROWS) * GRID_W

    qspec = pl.BlockSpec((None, None, Q_BLOCK, LANES), lambda hp, b, i, *_: (b, hp, i, 0))
    wspec = pl.BlockSpec((None, None, pl.Element(WIN_BLOCK), pl.Element(LANES)),
                         lambda hp, b, i, *_: (b, hp, win_start(i), 0))
    cspec = pl.BlockSpec((None, None, clen, LANES), lambda hp, b, i, *_: (b, hp, 0, 0))
    tspec = pl.BlockSpec((None, 2, N_DR, GRID_W, LANES), lambda hp, b, i, *_: (hp, 0, 0, 0, 0))
    grid_spec = pltpu.PrefetchScalarGridSpec(
        num_scalar_prefetch=3,
        grid=(HEAD_PAIRS, bsz, nblk),
        in_specs=[qspec, wspec, wspec, cspec, cspec, tspec],
        out_specs=qspec,
        scratch_shapes=[pltpu.VMEM((len(pieces), 2 * PAIR_BLOCK, PAIR_WIN), _f32)],
    )
    return pl.pallas_call(
        functools.partial(_attn_kernel, pieces=pieces, shifted=shifted),
        grid_spec=grid_spec,
        out_shape=jax.ShapeDtypeStruct(q.shape, _bf16),
        compiler_params=_params("arbitrary", "arbitrary", "arbitrary"),
        name="attention_shifted" if shifted else "attention",
    )(off_tab, var_tab, shift, q, k, v, kc, vc, bycol)


def _ctx_attn_kernel(q_ref, k_ref, v_ref, o_ref):
    o_ref[...] = _attend([(q_ref[...], [k_ref[...]], [v_ref[...]], [None])])[0].astype(o_ref.dtype)


def _ctx_attention(q, k, v):
    bsz, _, clen, _ = q.shape
    spec = pl.BlockSpec((None, None, clen, LANES), lambda b, hp: (b, hp, 0, 0))
    return pl.pallas_call(
        _ctx_attn_kernel,
        grid=(bsz, HEAD_PAIRS),
        in_specs=[spec, spec, spec],
        out_specs=spec,
        out_shape=jax.ShapeDtypeStruct(q.shape, _bf16),
        compiler_params=_params("arbitrary", "arbitrary"),
        name="ctx_attention",
    )(q, k, v)


def _window_plan(rows):
    patterns, off_tab, var_tab = {}, [], []
    for r in range(0, rows, PAIR_ROWS):
        ws = int(np.clip(r - NA_KH // 2, 0, rows - PAIR_WIN_ROWS))
        r0 = r // Q_ROWS * Q_ROWS
        block_ws = int(np.clip(r0 - NA_KH // 2, 0, rows - WIN_ROWS))
        assert 0 <= ws - block_ws <= WIN_ROWS - PAIR_WIN_ROWS
        piece = []
        for j in range(PAIR_ROWS):
            r_start = int(np.clip(r + j - NA_KH // 2, 0, rows - NA_KH))
            assert ws <= r_start and r_start + NA_KH <= ws + PAIR_WIN_ROWS
            piece.append(tuple(ws + kr - (r + j) + NA_KH - 1 if r_start <= ws + kr < r_start + NA_KH else N_DR - 1
                               for kr in range(PAIR_WIN_ROWS)))
        var_tab.append(patterns.setdefault(tuple(piece), len(patterns)))
        off_tab.append(ws - block_ws)
    return jnp.asarray(off_tab, jnp.int32), jnp.asarray(var_tab, jnp.int32), tuple(patterns)


def _window_col_table(rpb, shift):
    col = np.arange(GRID_W)
    col_start = np.clip(col - NA_KW // 2, 0, GRID_W - NA_KW)
    col_ok = (col[None, :] >= col_start[:, None]) & (col[None, :] < col_start[:, None] + NA_KW)
    dc = np.where(col_ok, col[None, :] - col[:, None] + NA_KW - 1, 0)
    onehot = np.zeros((2 * NA_KW - 1, GRID_W * GRID_W), np.float32)
    onehot[dc.reshape(-1), np.arange(GRID_W * GRID_W)] = col_ok.reshape(-1)
    by_col = jnp.dot(rpb.reshape(-1, 2 * NA_KW - 1), jnp.asarray(onehot), precision=lax.Precision.HIGHEST)
    by_col = by_col.reshape(NA_HEADS, 2 * NA_KH - 1, GRID_W, GRID_W)
    by_col = jnp.where(jnp.asarray(col_ok)[None, None], by_col - shift, MASKED)
    by_col = jnp.concatenate([by_col, jnp.full((NA_HEADS, 1, GRID_W, GRID_W), MASKED, _f32)], axis=1)
    by_col = jnp.concatenate([by_col, by_col], axis=-1)
    return by_col.reshape(HEAD_PAIRS, 2, N_DR, GRID_W, LANES)


def _merge_kernel(x_ref, o_ref, gu_ref, vn_ref, ga_ref, shf_ref, scf_ref, ws_ref, bs_ref, og_ref,
                  wout_ref, wr_ref, br_ref, tri_ref, sel_ref, xn_ref, row_ref, meta_ref, cnt_ref, *, tm):
    first = (pl.program_id(0) == 0) & (pl.program_id(1) == 0)

    @pl.when(first)
    def _():
        cnt_ref[...] = jnp.zeros_like(cnt_ref)

    nchunk = tm // SGU_CHUNK
    vn = vn_ref[...]
    lane = lax.broadcasted_iota(jnp.int32, (SGU_CHUNK, LANES), 1)
    low = lane < SGU_GROUP_DIM
    mixed_cols = []
    for gp in range(SGU_W // LANES):
        rhs_cols = []
        for n in range(nchunk):
            blk = vn[n * SGU_CHUNK:(n + 1) * SGU_CHUNK, gp * LANES:(gp + 1) * LANES]
            zero = jnp.zeros_like(blk)
            rhs_cols.append(jnp.concatenate([jnp.where(low, blk, zero), jnp.where(low, zero, blk)], axis=0))
        rhs = jnp.concatenate(rhs_cols, axis=1)
        mix = _dot(ws_ref[gp], rhs)
        mixed_cols.append(jnp.concatenate([mix[:, n * LANES:(n + 1) * LANES] for n in range(nchunk)], axis=0))
    mixed = jnp.concatenate(mixed_cols, axis=1)
    bias = jnp.concatenate([bs_ref[...]] * nchunk, axis=0)
    o_sg = gu_ref[...].astype(_f32) * (mixed + bias)
    o_na = jnp.concatenate([o_ref[hp].astype(_f32) for hp in range(HEAD_PAIRS)], axis=1)

    def rms(t):
        return t * lax.rsqrt(jnp.mean(t * t, -1, keepdims=True) + EPS)

    y = jnp.concatenate([rms(o_na), rms(o_sg)], axis=1) * og_ref[...]
    xn = x_ref[...] + ga_ref[...] * _dot(y.astype(_bf16), wout_ref[...])
    xn_ref[...] = xn
    hf = rms(xn) * (1.0 + scf_ref[...]) + shf_ref[...]
    row_ref[:, :D_MODEL] = hf

    lg = _dot_nt(wr_ref[...], hf.astype(_bf16)) + br_ref[...]
    quad = lambda r: lg[r:r + EXPERTS_PER_GROUP, :]
    row4 = lax.broadcasted_iota(jnp.int32, (EXPERTS_PER_GROUP, tm), 0).astype(_f32)
    over = lambda fn, t: fn(t, 0, keepdims=True)
    gl = quad(0)
    gm = over(jnp.max, gl)
    p_top = 1.0 / over(jnp.sum, jnp.exp(gl - gm))
    g_idx = over(jnp.min, jnp.where(gl == gm, row4, float(N_GROUPS)))
    el = quad(SUBLANES * N_GROUPS)
    for g in range(N_GROUPS - 2, -1, -1):
        el = jnp.where(g_idx == float(g), quad(SUBLANES * (g + 1)), el)
    ee = jnp.exp(el - over(jnp.max, el))
    pe = ee / over(jnp.sum, ee)
    v1 = over(jnp.max, pe)
    i1 = over(jnp.min, jnp.where(pe == v1, row4, float(EXPERTS_PER_GROUP)))
    pe2 = jnp.where(row4 == i1, -1.0, pe)
    v2 = over(jnp.max, pe2)
    i2 = over(jnp.min, jnp.where(pe2 == v2, row4, float(EXPERTS_PER_GROUP)))
    den = v1 + v2
    w1 = v1 / den * p_top
    w2 = v2 / den * p_top
    first_low = i1 < i2
    a = jnp.minimum(i1, i2)
    b = jnp.maximum(i1, i2)
    lex = a * (7.0 - a) * 0.5 + (b - a - 1.0)
    pair = lex
    for src, dst in enumerate(PAIR_OF_LEX):
        if src != dst:
            pair = jnp.where(lex == float(src), float(dst), pair)
    cls = g_idx * float(N_PAIRS) + pair
    gate_a = jnp.where(first_low, w1, w2)
    gate_b = jnp.where(first_low, w2, w1)

    row_p = lax.broadcasted_iota(jnp.int32, (2 * SUBLANES, tm), 0)
    stacked = jnp.zeros((2 * SUBLANES, tm), _f32)
    for n, gate in enumerate((gate_a, gate_b)):
        rest = gate
        for k in range(GATE_PIECES):
            piece = rest.astype(_bf16).astype(_f32)
            stacked = jnp.where(row_p == n * GATE_PIECES + k, piece, stacked)
            rest = rest - piece
    row_ref[:, D_MODEL:] = lax.dot_general(stacked.astype(_bf16), sel_ref[...],
                                           (((0,), (0,)), ((), ())), preferred_element_type=_f32)

    row_c = lax.broadcasted_iota(jnp.int32, (CLASS_ROWS, tm), 0).astype(_f32)
    onehot = row_c == cls
    ones = jnp.where(onehot, 1.0, 0.0)
    before = _dot(ones.astype(_bf16), tri_ref[...])
    rank = over(jnp.sum, jnp.where(onehot, before + cnt_ref[:, 0:1], 0.0))
    cnt_ref[...] += jnp.sum(ones, 1, keepdims=True)
    meta_ref[...] = jnp.concatenate([cls, rank, jnp.zeros((SUBLANES - 2, tm), _f32)], axis=0)


def _merge(x, o_na, gu, vn, g_a, sh_f, sc_f, ws_cat, bs_tab, out_gain, w_out, w_r, b_r, tri, sel, tm):
    bsz, length, _ = x.shape
    mod = pl.BlockSpec((None, 1, D_MODEL), lambda b, i: (b, 0, 0))
    tok = lambda n: pl.BlockSpec((None, tm, n), lambda b, i: (b, i, 0))
    const = lambda *shape: pl.BlockSpec(shape, lambda b, i: (0,) * len(shape))
    return pl.pallas_call(
        functools.partial(_merge_kernel, tm=tm),
        grid=(bsz, length // tm),
        in_specs=[tok(D_MODEL),
                  pl.BlockSpec((None, HEAD_PAIRS, tm, LANES), lambda b, i: (b, 0, i, 0)),
                  tok(SGU_W), tok(SGU_W), mod, mod, mod,
                  const(SGU_W // LANES, SGU_CHUNK, 2 * SGU_CHUNK),
                  const(SGU_CHUNK, SGU_W),
                  const(1, D_MODEL),
                  const(D_MODEL, D_MODEL),
                  const(ROUTER_ROWS, D_MODEL),
                  const(ROUTER_ROWS, 1),
                  const(tm, tm),
                  const(2 * SUBLANES, LANES)],
        out_specs=[tok(D_MODEL), tok(ROW_W),
                   pl.BlockSpec((None, SUBLANES, tm), lambda b, i: (b, 0, i)),
                   const(CLASS_ROWS, LANES)],
        out_shape=[jax.ShapeDtypeStruct((bsz, length, D_MODEL), _f32),
                   jax.ShapeDtypeStruct((bsz, length, ROW_W), _f32),
                   jax.ShapeDtypeStruct((bsz, SUBLANES, length), _f32),
                   jax.ShapeDtypeStruct((CLASS_ROWS, LANES), _f32)],
        compiler_params=_params("arbitrary", "arbitrary"),
        name="merge",
    )(x, o_na, gu, vn, g_a, sh_f, sc_f, ws_cat, bs_tab, out_gain, w_out, w_r, b_r, tri, sel)


def _issue_row_copies(count, make_copy):
    for j in range(count):
        make_copy(j).start(priority=j % 2)


def _dispatch_kernel(fill_ref, pos_ref, rows_ref, xs_ref, zero_ref, sem, fill_sem, *, td, tile):
    @pl.when(pl.program_id(0) == 0)
    def _():
        zero_ref[...] = jnp.zeros_like(zero_ref)

        def fill(k):
            start = pl.multiple_of(jnp.maximum(fill_ref[k], 0), tile)
            return pltpu.make_async_copy(zero_ref, xs_ref.at[pl.ds(start, tile)], fill_sem)

        for k in range(fill_ref.shape[0]):
            @pl.when(fill_ref[k] >= 0)
            def _():
                fill(k).start()
        for k in range(fill_ref.shape[0]):
            @pl.when(fill_ref[k] >= 0)
            def _():
                fill(k).wait()

    _scatter_rows(pos_ref, rows_ref, xs_ref, sem, td)


def _scatter_rows(pos_ref, rows_ref, xs_ref, sem, td):
    _issue_row_copies(td, lambda j: pltpu.make_async_copy(
        rows_ref.at[pl.ds(j, 1)], xs_ref.at[pl.ds(pos_ref[0, j], 1)], sem))
    pltpu.make_async_copy(rows_ref, xs_ref.at[pl.ds(0, td)], sem).wait()


def _dispatch_more_kernel(pos_ref, rows_ref, base_ref, xs_ref, sem, *, td):
    del base_ref
    _scatter_rows(pos_ref, rows_ref, xs_ref, sem, td)


def _dispatch(rows, pos, fill, padded, td, tile):
    n = rows.shape[0]
    grid_spec = pltpu.PrefetchScalarGridSpec(
        num_scalar_prefetch=1,
        grid=(n // td,),
        in_specs=[pl.BlockSpec((None, 1, td), lambda i, fl: (i, 0, 0), memory_space=pltpu.SMEM),
                  pl.BlockSpec((td, ROW_W), lambda i, fl: (i, 0))],
        out_specs=pl.BlockSpec(memory_space=pl.ANY),
        scratch_shapes=[pltpu.VMEM((tile, ROW_W), _f32), pltpu.SemaphoreType.DMA(()), pltpu.SemaphoreType.DMA(())],
    )
    return pl.pallas_call(
        functools.partial(_dispatch_kernel, td=td, tile=tile),
        grid_spec=grid_spec,
        out_shape=jax.ShapeDtypeStruct((padded, ROW_W), _f32),
        compiler_params=_params("arbitrary"),
        name="dispatch",
    )(fill, pos.reshape(n // td, 1, td), rows)


def _dispatch_more(xs, rows, pos, td):
    n = rows.shape[0]
    return pl.pallas_call(
        functools.partial(_dispatch_more_kernel, td=td),
        grid=(n // td,),
        in_specs=[pl.BlockSpec((None, 1, td), lambda i: (i, 0, 0), memory_space=pltpu.SMEM),
                  pl.BlockSpec((td, ROW_W), lambda i: (i, 0)),
                  pl.BlockSpec(memory_space=pl.ANY)],
        out_specs=pl.BlockSpec(memory_space=pl.ANY),
        out_shape=jax.ShapeDtypeStruct(xs.shape, _f32),
        scratch_shapes=[pltpu.SemaphoreType.DMA(())],
        input_output_aliases={2: 0},
        compiler_params=_params("arbitrary"),
        name="dispatch_more",
    )(pos.reshape(n // td, 1, td), rows, xs)


def _combine_kernel(pos_ref, npos_ref, x_ref, g_ref, ys_ref, o_ref, buf0, buf1, sems, *, tc, nsteps):
    s = pl.program_id(0) * pl.num_programs(1) + pl.program_id(1)
    bufs = (buf0, buf1)

    def request(idx_ref, slot):
        _issue_row_copies(tc, lambda j: pltpu.make_async_copy(
            ys_ref.at[pl.ds(idx_ref[0, j], 1)], bufs[slot].at[pl.ds(j, 1)], sems.at[slot]))

    @pl.when(s == 0)
    def _():
        request(pos_ref, 0)

    for slot in range(2):
        @pl.when((s % 2 == slot) & (s + 1 < nsteps))
        def _():
            request(npos_ref, 1 - slot)

        @pl.when(s % 2 == slot)
        def _():
            pltpu.make_async_copy(ys_ref.at[pl.ds(0, tc)], bufs[slot], sems.at[slot]).wait()
            o_ref[...] = x_ref[...] + g_ref[...] * bufs[slot][...]


def _combine(x, ys, pos, g, tc):
    bsz, length, _ = x.shape
    per_batch = length // tc
    nsteps = bsz * per_batch
    tok = pl.BlockSpec((None, tc, D_MODEL), lambda b, i: (b, i, 0))
    pos3 = pos.reshape(nsteps, 1, tc)
    return pl.pallas_call(
        functools.partial(_combine_kernel, tc=tc, nsteps=nsteps),
        grid=(bsz, per_batch),
        in_specs=[pl.BlockSpec((None, 1, tc), lambda b, i: (b * per_batch + i, 0, 0), memory_space=pltpu.SMEM),
                  pl.BlockSpec((None, 1, tc), lambda b, i: (jnp.minimum(b * per_batch + i + 1, nsteps - 1), 0, 0),
                               memory_space=pltpu.SMEM),
                  tok,
                  pl.BlockSpec((None, 1, D_MODEL), lambda b, i: (b, 0, 0)),
                  pl.BlockSpec(memory_space=pl.ANY)],
        out_specs=tok,
        out_shape=jax.ShapeDtypeStruct(x.shape, _f32),
        scratch_shapes=[pltpu.VMEM((tc, D_MODEL), _f32), pltpu.VMEM((tc, D_MODEL), _f32),
                        pltpu.SemaphoreType.DMA((2,))],
        compiler_params=_params("arbitrary", "arbitrary"),
        name="combine",
    )(pos3, pos3, x, g, ys)


def _experts_kernel(ea_ref, eb_ref, chg_ref, nlive_ref, xs_ref, w1_ref, w3_ref, w2_ref, y_ref,
                    b1, b3, b2, s1, s3, s2, sems, *, layer):
    t = pl.program_id(0)
    live = t < nlive_ref[0]
    first = (ea_ref[t] // EXPERTS_PER_GROUP) * EXPERTS_PER_GROUP

    def stage(k, slot):
        return [pltpu.make_async_copy(w.at[layer, first + k], s.at[slot], sems.at[slot])
                for w, s in ((w1_ref, s1), (w3_ref, s3), (w2_ref, s2))]

    @pl.when(live & ((chg_ref[t] & 1) != 0))
    def _():
        for k in range(2):
            for cp in stage(k, k):
                cp.start()
        for k in range(EXPERTS_PER_GROUP):
            slot = k % 2
            for cp in stage(k, slot):
                cp.wait()
            b1[k] = s1[slot].astype(_bf16)
            b3[k] = s3[slot].astype(_bf16)
            b2[k] = s2[slot].astype(_bf16)
            if k + 2 < EXPERTS_PER_GROUP:
                for cp in stage(k + 2, slot):
                    cp.start()

    def run(nrows):
        xs = xs_ref[:nrows, :D_MODEL].astype(_bf16)
        gates = xs_ref[:nrows, D_MODEL:]

        def expert(e):
            hid = jax.nn.silu(_dot(xs, b1[e])) * _dot(xs, b3[e])
            return _dot(hid.astype(_bf16), b2[e])

        y_ref[:nrows, :] = (gates[:, 0:1] * expert(ea_ref[t] - first) + gates[:, 1:2] * expert(eb_ref[t] - first))

    half_only = (chg_ref[t] & 4) != 0
    half = y_ref.shape[0] // 2

    @pl.when(live & jnp.logical_not(half_only))
    def _():
        run(y_ref.shape[0])

    @pl.when(live & half_only)
    def _():
        run(half)
        y_ref[half:, :] = jnp.zeros((half, D_MODEL), _f32)

    @pl.when(jnp.logical_not(live))
    def _():
        y_ref[...] = jnp.zeros_like(y_ref)


def _experts(xs, tile_ea, tile_eb, chg, nlive, w1, w3, w2, layer, tile):
    ntiles = xs.shape[0] // tile
    hbm = pl.BlockSpec(memory_space=pl.ANY)
    up, down = (D_MODEL, D_EXPERT), (D_EXPERT, D_MODEL)
    grid_spec = pltpu.PrefetchScalarGridSpec(
        num_scalar_prefetch=4,
        grid=(ntiles,),
        in_specs=[pl.BlockSpec((tile, ROW_W), lambda t, ea, eb, cg, nl: (t, 0)), hbm, hbm, hbm],
        out_specs=pl.BlockSpec((tile, D_MODEL), lambda t, ea, eb, cg, nl: (t, 0)),
        scratch_shapes=[pltpu.VMEM((EXPERTS_PER_GROUP,) + up, _bf16), pltpu.VMEM((EXPERTS_PER_GROUP,) + up, _bf16),
                        pltpu.VMEM((EXPERTS_PER_GROUP,) + down, _bf16),
                        pltpu.VMEM((2,) + up, _f32), pltpu.VMEM((2,) + up, _f32), pltpu.VMEM((2,) + down, _f32),
                        pltpu.SemaphoreType.DMA((2,))],
    )
    return pl.pallas_call(
        functools.partial(_experts_kernel, layer=layer),
        grid_spec=grid_spec,
        out_shape=jax.ShapeDtypeStruct((xs.shape[0], D_MODEL), _f32),
        compiler_params=_params("arbitrary"),
        name="experts",
    )(tile_ea, tile_eb, chg, nlive, xs, w1, w3, w2)


def _dispatch_plan(cls, rank, counts, first_counts, later, tile):
    n = cls.shape[0]
    padded = -(-(n + N_CLASSES * (tile - 1)) // tile) * tile
    ntiles = padded // tile
    pcounts = (counts + tile - 1) // tile * tile
    pad_end = jnp.cumsum(pcounts)
    pad_off = pad_end - pcounts
    pos = pad_off[cls] + rank
    nlive = pad_end[-1] // tile
    tile_start = jnp.arange(ntiles, dtype=jnp.int32) * tile
    tile_cls = jnp.sum((pad_end[None, :] <= jnp.minimum(tile_start, pad_end[-1] - 1)[:, None]).astype(jnp.int32), axis=1)
    tile_cls = jnp.minimum(tile_cls, N_CLASSES - 1)
    grp = tile_cls // N_PAIRS
    pair = tile_cls % N_PAIRS
    tile_ea = grp * EXPERTS_PER_GROUP + jnp.asarray(PAIR_A, jnp.int32)[pair]
    tile_eb = grp * EXPERTS_PER_GROUP + jnp.asarray(PAIR_B, jnp.int32)[pair]
    new_group = jnp.concatenate([jnp.ones((1,), jnp.int32), (grp[1:] != grp[:-1]).astype(jnp.int32)])
    tokens_in_tile = (pad_off + counts)[tile_cls] - tile_start
    chg = new_group + 4 * (tokens_in_tile <= tile // 2).astype(jnp.int32)
    per_class = -(-(later + tile - 1) // tile) + 1
    uncovered = pad_off + first_counts
    tails = []
    for j in range(per_class):
        start = pad_end - (j + 1) * tile
        tails.append(jnp.where((start >= pad_off) & (start + tile > uncovered), start, -1))
    past = (nlive + jnp.arange(N_CLASSES, dtype=jnp.int32)) * tile
    fill = jnp.concatenate(tails + [jnp.where(past < padded, past, -1)]).astype(jnp.int32)
    return pos.astype(jnp.int32), padded, tile_ea, tile_eb, chg, nlive.astype(jnp.int32).reshape(1), fill


def _ffn(streams, w1, w3, w2, layer, tile):
    sizes = [s[0].shape[0] * s[0].shape[1] for s in streams]
    cls = [s[2][:, 0, :].reshape(n).astype(jnp.int32) for s, n in zip(streams, sizes)]
    cnts = [s[3][:N_CLASSES, 0].astype(jnp.int32) for s in streams]
    ranks, before = [], jnp.zeros((N_CLASSES,), jnp.int32)
    for s, n, c, cnt in zip(streams, sizes, cls, cnts):
        ranks.append(s[2][:, 1, :].reshape(n).astype(jnp.int32) + before[c])
        before = before + cnt
    pos, padded, tile_ea, tile_eb, chg, nlive, fill = _dispatch_plan(
        jnp.concatenate(cls), jnp.concatenate(ranks), before, cnts[0], sum(sizes[1:]), tile)
    starts = np.cumsum([0] + sizes)
    pos = [pos[a:b] for a, b in zip(starts[:-1], starts[1:])]
    xs = _dispatch(streams[0][1].reshape(sizes[0], ROW_W), pos[0], fill, padded, streams[0][5], tile)
    for s, n, p in zip(streams[1:], sizes[1:], pos[1:]):
        xs = _dispatch_more(xs, s[1].reshape(n, ROW_W), p, s[5])
    ys = _experts(xs, tile_ea, tile_eb, chg, nlive, w1, w3, w2, layer, tile)
    return [_combine(s[0], ys, p, s[4], s[5]) for s, p in zip(streams, pos)]


def kernel(x, c, ctx, c_ctx, w_ada, b_ada, w_in, q_gain, k_gain, rpb, sgu_ln, sgu_w, sgu_b, out_gain,
           w_out, rg_w, rg_b, re_w, re_b, w1, w3, w2):
    bsz, seq, _ = x.shape
    depth = w_ada.shape[0]
    rows = seq // GRID_W
    assert seq % Q_BLOCK == 0 and rows // Q_ROWS >= 3 and ctx.shape[1] % SGU_CHUNK == 0
    tm_x, tm_c = 512, ctx.shape[1]

    cond = jnp.zeros((SUBLANES, D_MODEL), _f32).at[:bsz].set(c).at[bsz].set(c_ctx)
    mods = _adaln(cond, w_ada, b_ada).reshape(depth, SUBLANES, N_MOD, D_MODEL)

    assert MXU_TILE % NA_HEAD_DIM == 0 and NA_W % MXU_TILE == 0
    head_id = np.arange(MXU_TILE) // NA_HEAD_DIM
    avg = jnp.asarray((head_id[:, None] == head_id[None, :]).astype(np.float32) / NA_HEAD_DIM, _bf16)
    win_plan = _window_plan(rows)
    tri = {tm: jnp.asarray(np.triu(np.ones((tm, tm), np.float32), 1), _bf16) for tm in {tm_x, tm_c}}
    sel_np = np.zeros((2 * SUBLANES, LANES), np.float32)
    for n in range(2):
        sel_np[n * GATE_PIECES:(n + 1) * GATE_PIECES, n] = 1.0
    sel = jnp.asarray(sel_np, _bf16)

    for i in range(depth):
        last = i == depth - 1
        mx = [mods[i, :bsz, j][:, None, :] for j in range(N_MOD)]
        mc = [jnp.broadcast_to(mods[i, bsz, j][None, None, :], (bsz, 1, D_MODEL)) for j in range(N_MOD)]
        w_in_b = jnp.concatenate([w_in[i, :, 3 * NA_W:], w_in[i, :, :3 * NA_W]], axis=1).astype(_bf16)
        qg = (jnp.tile(q_gain[i], NA_HEADS) * NA_HEAD_DIM ** -0.5)[None, :]
        kg = jnp.tile(k_gain[i], NA_HEADS)[None, :]
        ln_g = sgu_ln[i][None, :]
        ws_cat = jnp.concatenate([sgu_w[i, 0::2], sgu_w[i, 1::2]], axis=2).astype(_bf16)
        bs_tab = jnp.repeat(sgu_b[i].T, SGU_GROUP_DIM, axis=1)
        og = out_gain[i][None, :]
        w_out_b = w_out[i].astype(_bf16)
        pad_w = jnp.zeros((SUBLANES - EXPERTS_PER_GROUP, D_MODEL), _f32)
        pad_b = jnp.zeros((SUBLANES - EXPERTS_PER_GROUP,), _f32)
        w_r = jnp.concatenate([rg_w[i].T, pad_w] + [t for g in range(N_GROUPS) for t in (re_w[i, g].T, pad_w)]
                              + [jnp.zeros((ROUTER_ROWS - SUBLANES * (N_GROUPS + 1), D_MODEL), _f32)]).astype(_bf16)
        b_r = jnp.concatenate([rg_b[i], pad_b] + [t for g in range(N_GROUPS) for t in (re_b[i, g], pad_b)]
                              + [jnp.zeros((ROUTER_ROWS - SUBLANES * (N_GROUPS + 1),), _f32)])[:, None]
        score_bound = 1.02 * NA_HEAD_DIM ** 0.5 * jnp.max(jnp.abs(q_gain[i])) * jnp.max(jnp.abs(k_gain[i]))
        shift = (score_bound + jnp.maximum(jnp.max(rpb[i]), 0.0)).reshape(1)
        spread = 2.0 * score_bound + jnp.maximum(jnp.max(rpb[i]), 0.0) - jnp.minimum(jnp.min(rpb[i]), 0.0)

        q, k, v, gu, vn = _proj(x, mx[0], mx[1], w_in_b, avg, qg, kg, ln_g, tm_x)
        qc, kc, vc, guc, vnc = _proj(ctx, mc[0], mc[1], w_in_b, avg, qg, kg, ln_g, tm_c)
        o_na = lax.cond(
            spread < MAX_SHIFT_SPREAD,
            lambda: _attention(q, k, v, kc, vc, _window_col_table(rpb[i], shift[0]), win_plan, shift, True),
            lambda: _attention(q, k, v, kc, vc, _window_col_table(rpb[i], 0.0), win_plan, shift, False))
        x_new, xrows, meta, counts = _merge(x, o_na, gu, vn, mx[2], mx[3], mx[4], ws_cat, bs_tab, og,
                                            w_out_b, w_r, b_r, tri[tm_x], sel, tm_x)
        streams = [(x_new, xrows, meta, counts, mx[5], 1024)]
        if not last:
            oc_na = _ctx_attention(qc, kc, vc)
            c_new, crows, cmeta, ccounts = _merge(ctx, oc_na, guc, vnc, mc[2], mc[3], mc[4], ws_cat, bs_tab,
                                                  og, w_out_b, w_r, b_r, tri[tm_c], sel, tm_c)
            streams.append((c_new, crows, cmeta, ccounts, mc[5], tm_c))
        outs = _ffn(streams, w1, w3, w2, i, 512)
        x = outs[0]
        if not last:
            ctx = outs[1]
    return x
```

```python
import functools

import jax
import jax.numpy as jnp
import numpy as np
from jax import lax
from jax.experimental import pallas as pl
from jax.experimental.pallas import tpu as pltpu

D_MODEL = 1024
GRID_W = 64
NA_HEADS = 8
NA_HEAD_DIM = 64
NA_W = NA_HEADS * NA_HEAD_DIM
NA_KH = 8
NA_KW = 16
SGU_GROUPS = 8
SGU_W = D_MODEL - NA_W
SGU_GROUP_DIM = SGU_W // SGU_GROUPS
SGU_CHUNK = 128
IN_COLS = 3 * NA_W + 2 * SGU_W
N_GROUPS = 4
EXPERTS_PER_GROUP = 4
N_EXPERTS = N_GROUPS * EXPERTS_PER_GROUP
D_EXPERT = D_MODEL // 2
N_MOD = 6
EPS = 1e-6

LANES = 128
SUBLANES = 8
HEAD_PAIRS = NA_W // LANES
Q_ROWS = 8
Q_BLOCK = Q_ROWS * GRID_W
WIN_ROWS = Q_ROWS + NA_KH - 1
WIN_BLOCK = WIN_ROWS * GRID_W
PAIR_ROWS = 2
PAIR_BLOCK = PAIR_ROWS * GRID_W
PAIR_WIN_ROWS = PAIR_ROWS + NA_KH - 1
PAIR_WIN = PAIR_WIN_ROWS * GRID_W
PAIRS_IN_FLIGHT = 2
N_DR = 2 * NA_KH
N_PAIRS = 6
N_CLASSES = N_GROUPS * N_PAIRS
PAIR_A = (0, 0, 1, 1, 0, 2)
PAIR_B = (1, 2, 2, 3, 3, 3)
PAIR_OF_LEX = (0, 1, 4, 2, 3, 5)
CLASS_ROWS = 32
ROUTER_ROWS = 48
GATE_PIECES = 3
ROW_W = D_MODEL + LANES
MXU_TILE = 256
MASKED = -1e30
MAX_SHIFT_SPREAD = 60.0
VMEM_LIMIT_BYTES = 56 * 1024 * 1024

_f32 = jnp.float32
_bf16 = jnp.bfloat16


def _params(*semantics):
    return pltpu.CompilerParams(dimension_semantics=semantics, vmem_limit_bytes=VMEM_LIMIT_BYTES)


def _dot(a, b):
    return jnp.dot(a, b, preferred_element_type=_f32)


def _dot_nt(a, b):
    return lax.dot_general(a, b, (((1,), (1,)), ((), ())), preferred_element_type=_f32)


def _adaln_kernel(cond_ref, w_ref, b_ref, out_ref):
    cond = cond_ref[...]
    act = cond * jax.nn.sigmoid(cond)
    out_ref[...] = _dot(act.astype(_bf16), w_ref[...].astype(_bf16)) + b_ref[...]


def _adaln(cond, w_ada, b_ada):
    depth = w_ada.shape[0]
    cols = N_MOD * D_MODEL
    col_block = 1536
    return pl.pallas_call(
        _adaln_kernel,
        grid=(depth, cols // col_block),
        in_specs=[
            pl.BlockSpec((SUBLANES, D_MODEL), lambda d, j: (0, 0)),
            pl.BlockSpec((None, D_MODEL, col_block), lambda d, j: (d, 0, j)),
            pl.BlockSpec((None, 1, col_block), lambda d, j: (d, 0, j)),
        ],
        out_specs=pl.BlockSpec((None, SUBLANES, col_block), lambda d, j: (d, 0, j)),
        out_shape=jax.ShapeDtypeStruct((depth, SUBLANES, cols), _f32),
        compiler_params=_params("arbitrary", "arbitrary"),
        name="adaln",
    )(cond, w_ada, b_ada.reshape(depth, 1, cols))


def _proj_kernel(x_ref, sh_ref, sc_ref, w_ref, avg_ref, qg_ref, kg_ref, ln_ref,
                 q_ref, k_ref, v_ref, gu_ref, vn_ref):
    x = x_ref[...]
    h = x * lax.rsqrt(jnp.mean(x * x, -1, keepdims=True) + EPS) * (1.0 + sc_ref[...]) + sh_ref[...]
    p = _dot(h.astype(_bf16), w_ref[...])
    u = p[:, :SGU_W]
    vs = p[:, SGU_W:2 * SGU_W]
    q = p[:, 2 * SGU_W:2 * SGU_W + NA_W]
    k = p[:, 2 * SGU_W + NA_W:2 * SGU_W + 2 * NA_W]
    v = p[:, 2 * SGU_W + 2 * NA_W:]
    def head_mean(t):
        sq = (t * t).astype(_bf16)
        return jnp.concatenate([_dot(sq[:, c:c + MXU_TILE], avg_ref[...]) for c in range(0, NA_W, MXU_TILE)], axis=1)

    q_ms = head_mean(q)
    k_ms = head_mean(k)
    qn = (q * lax.rsqrt(q_ms + EPS) * qg_ref[...]).astype(_bf16)
    kn = (k * lax.rsqrt(k_ms + EPS) * kg_ref[...]).astype(_bf16)
    vb = v.astype(_bf16)
    for hp in range(HEAD_PAIRS):
        cols = slice(hp * LANES, (hp + 1) * LANES)
        q_ref[hp] = qn[:, cols]
        k_ref[hp] = kn[:, cols]
        v_ref[hp] = vb[:, cols]
    gu_ref[...] = jax.nn.gelu(u).astype(_bf16)
    gv = jax.nn.gelu(vs)
    mu = jnp.mean(gv, -1, keepdims=True)
    cen = gv - mu
    var = jnp.mean(cen * cen, -1, keepdims=True)
    vn_ref[...] = (cen * lax.rsqrt(var + EPS) * ln_ref[...]).astype(_bf16)


def _proj(x, sh, sc, w_in, avg, qg, kg, ln_g, tm):
    bsz, length, _ = x.shape
    vec = lambda n: pl.BlockSpec((1, n), lambda b, i: (0, 0))
    mod = pl.BlockSpec((None, 1, D_MODEL), lambda b, i: (b, 0, 0))
    tok = lambda n: pl.BlockSpec((None, tm, n), lambda b, i: (b, i, 0))
    hpm = pl.BlockSpec((None, HEAD_PAIRS, tm, LANES), lambda b, i: (b, 0, i, 0))
    hpm_shape = jax.ShapeDtypeStruct((bsz, HEAD_PAIRS, length, LANES), _bf16)
    sgu_shape = jax.ShapeDtypeStruct((bsz, length, SGU_W), _bf16)
    return pl.pallas_call(
        _proj_kernel,
        grid=(bsz, length // tm),
        in_specs=[tok(D_MODEL), mod, mod,
                  pl.BlockSpec((D_MODEL, IN_COLS), lambda b, i: (0, 0)),
                  pl.BlockSpec((MXU_TILE, MXU_TILE), lambda b, i: (0, 0)),
                  vec(NA_W), vec(NA_W), vec(SGU_W)],
        out_specs=[hpm, hpm, hpm, tok(SGU_W), tok(SGU_W)],
        out_shape=[hpm_shape, hpm_shape, hpm_shape, sgu_shape, sgu_shape],
        compiler_params=_params("arbitrary", "arbitrary"),
        name="proj",
    )(x, sh, sc, w_in, avg, qg, kg, ln_g)


def _attend(blocks, shift=None):
    add = lambda a, b: a + b
    m_rows = blocks[0][0].shape[0]
    lane = lax.broadcasted_iota(jnp.int32, blocks[0][0].shape, 1)
    low = lane < NA_HEAD_DIM
    stacked = [jnp.concatenate([jnp.where(low, q, jnp.zeros_like(q)), jnp.where(low, jnp.zeros_like(q), q)], axis=0)
               for q, _, _, _ in blocks]
    if shift is None:
        scores = [[_dot_nt(qs, key) if bias is None else _dot_nt(qs, key) + bias[...]
                   for key, bias in zip(keys, biases)] for qs, (_, keys, _, biases) in zip(stacked, blocks)]
        maxes = [functools.reduce(jnp.maximum, [jnp.max(s, -1, keepdims=True) for s in ss]) for ss in scores]
        probs = [[jnp.exp(s - m) for s in ss] for ss, m in zip(scores, maxes)]
    else:
        probs = [[jnp.exp(_dot_nt(qs, key) - shift) if bias is None else jnp.exp(_dot_nt(qs, key) + bias[...])
                  for key, bias in zip(keys, biases)] for qs, (_, keys, _, biases) in zip(stacked, blocks)]
    dens = [functools.reduce(add, [jnp.sum(p, -1, keepdims=True) for p in ps]) for ps in probs]
    accs = [functools.reduce(add, [_dot(p.astype(_bf16), val) for p, val in zip(ps, blk[2])])
            for ps, blk in zip(probs, blocks)]
    outs = [acc / den for acc, den in zip(accs, dens)]
    return [jnp.where(low, o2[:m_rows], o2[m_rows:]) for o2 in outs]


def _attn_kernel(off_ref, var_ref, shift_ref, q_ref, k_ref, v_ref, kc_ref, vc_ref, bycol_ref, o_ref, bias_ref,
                 *, pieces, shifted):
    i = pl.program_id(2)
    shift = shift_ref[0] if shifted else None

    @pl.when(i == 0)
    def _():
        for var, piece in enumerate(pieces):
            for hh in range(2):
                for j in range(PAIR_ROWS):
                    for kr in range(PAIR_WIN_ROWS):
                        lo = (kr % 2) * GRID_W
                        row = hh * PAIR_BLOCK + j * GRID_W
                        bias_ref[var, row:row + GRID_W, kr * GRID_W:(kr + 1) * GRID_W] = (
                            bycol_ref[hh, piece[j][kr], :, lo:lo + GRID_W])

    kc = kc_ref[...]
    vc = vc_ref[...]
    npair = Q_ROWS // PAIR_ROWS
    for g0 in range(0, npair, PAIRS_IN_FLIGHT):
        blocks = []
        for g in range(g0, g0 + PAIRS_IN_FLIGHT):
            start = pl.multiple_of(off_ref[i * npair + g] * GRID_W, GRID_W)
            var = var_ref[i * npair + g]
            blocks.append((q_ref[g * PAIR_BLOCK:(g + 1) * PAIR_BLOCK, :],
                           [k_ref[pl.ds(start, PAIR_WIN), :], kc],
                           [v_ref[pl.ds(start, PAIR_WIN), :], vc],
                           [bias_ref.at[var], None]))
        for g, out in zip(range(g0, g0 + PAIRS_IN_FLIGHT), _attend(blocks, shift)):
            o_ref[g * PAIR_BLOCK:(g + 1) * PAIR_BLOCK, :] = out.astype(o_ref.dtype)


def _attention(q, k, v, kc, vc, bycol, plan, shift, shifted):
    off_tab, var_tab, pieces = plan
    bsz, _, seq, _ = q.shape
    clen = kc.shape[2]
    nblk = seq // Q_BLOCK

    def win_start(i):
        return jnp.clip(i * Q_ROWS - NA_KH // 2, 0, seq // GRID_W - WIN_ROWS) * GRID_W

    qspec = pl.BlockSpec((None, None, Q_BLOCK, LANES), lambda hp, b, i, *_: (b, hp, i, 0))
    wspec = pl.BlockSpec((None, None, pl.Element(WIN_BLOCK), pl.Element(LANES)),
                         lambda hp, b, i, *_: (b, hp, win_start(i), 0))
    cspec = pl.BlockSpec((None, None, clen, LANES), lambda hp, b, i, *_: (b, hp, 0, 0))
    tspec = pl.BlockSpec((None, 2, N_DR, GRID_W, LANES), lambda hp, b, i, *_: (hp, 0, 0, 0, 0))
    grid_spec = pltpu.PrefetchScalarGridSpec(
        num_scalar_prefetch=3,
        grid=(HEAD_PAIRS, bsz, nblk),
        in_specs=[qspec, wspec, wspec, cspec, cspec, tspec],
        out_specs=qspec,
        scratch_shapes=[pltpu.VMEM((len(pieces), 2 * PAIR_BLOCK, PAIR_WIN), _f32)],
    )
    return pl.pallas_call(
        functools.partial(_attn_kernel, pieces=pieces, shifted=shifted),
        grid_spec=grid_spec,
        out_shape=jax.ShapeDtypeStruct(q.shape, _bf16),
        compiler_params=_params("arbitrary", "arbitrary", "arbitrary"),
        name="attention_shifted" if shifted else "attention",
    )(off_tab, var_tab, shift, q, k, v, kc, vc, bycol)


def _ctx_attn_kernel(q_ref, k_ref, v_ref, o_ref):
    o_ref[...] = _attend([(q_ref[...], [k_ref[...]], [v_ref[...]], [None])])[0].astype(o_ref.dtype)


def _ctx_attention(q, k, v):
    bsz, _, clen, _ = q.shape
    spec = pl.BlockSpec((None, None, clen, LANES), lambda b, hp: (b, hp, 0, 0))
    return pl.pallas_call(
        _ctx_attn_kernel,
        grid=(bsz, HEAD_PAIRS),
        in_specs=[spec, spec, spec],
        out_specs=spec,
        out_shape=jax.ShapeDtypeStruct(q.shape, _bf16),
        compiler_params=_params("arbitrary", "arbitrary"),
        name="ctx_attention",
    )(q, k, v)


def _window_plan(rows):
    patterns, off_tab, var_tab = {}, [], []
    for r in range(0, rows, PAIR_ROWS):
        ws = int(np.clip(r - NA_KH // 2, 0, rows - PAIR_WIN_ROWS))
        r0 = r // Q_ROWS * Q_ROWS
        block_ws = int(np.clip(r0 - NA_KH // 2, 0, rows - WIN_ROWS))
        assert 0 <= ws - block_ws <= WIN_ROWS - PAIR_WIN_ROWS
        piece = []
        for j in range(PAIR_ROWS):
            r_start = int(np.clip(r + j - NA_KH // 2, 0, rows - NA_KH))
            assert ws <= r_start and r_start + NA_KH <= ws + PAIR_WIN_ROWS
            piece.append(tuple(ws + kr - (r + j) + NA_KH - 1 if r_start <= ws + kr < r_start + NA_KH else N_DR - 1
                               for kr in range(PAIR_WIN_ROWS)))
        var_tab.append(patterns.setdefault(tuple(piece), len(patterns)))
        off_tab.append(ws - block_ws)
    return jnp.asarray(off_tab, jnp.int32), jnp.asarray(var_tab, jnp.int32), tuple(patterns)


def _window_col_table(rpb, shift):
    col = np.arange(GRID_W)
    col_start = np.clip(col - NA_KW // 2, 0, GRID_W - NA_KW)
    col_ok = (col[None, :] >= col_start[:, None]) & (col[None, :] < col_start[:, None] + NA_KW)
    dc = np.where(col_ok, col[None, :] - col[:, None] + NA_KW - 1, 0)
    onehot = np.zeros((2 * NA_KW - 1, GRID_W * GRID_W), np.float32)
    onehot[dc.reshape(-1), np.arange(GRID_W * GRID_W)] = col_ok.reshape(-1)
    by_col = jnp.dot(rpb.reshape(-1, 2 * NA_KW - 1), jnp.asarray(onehot), precision=lax.Precision.HIGHEST)
    by_col = by_col.reshape(NA_HEADS, 2 * NA_KH - 1, GRID_W, GRID_W)
    by_col = jnp.where(jnp.asarray(col_ok)[None, None], by_col - shift, MASKED)
    by_col = jnp.concatenate([by_col, jnp.full((NA_HEADS, 1, GRID_W, GRID_W), MASKED, _f32)], axis=1)
    by_col = jnp.concatenate([by_col, by_col], axis=-1)
    return by_col.reshape(HEAD_PAIRS, 2, N_DR, GRID_W, LANES)


def _merge_kernel(x_ref, o_ref, gu_ref, vn_ref, ga_ref, shf_ref, scf_ref, ws_ref, bs_ref, og_ref,
                  wout_ref, wr_ref, br_ref, tri_ref, sel_ref, xn_ref, row_ref, meta_ref, cnt_ref, *, tm):
    first = (pl.program_id(0) == 0) & (pl.program_id(1) == 0)

    @pl.when(first)
    def _():
        cnt_ref[...] = jnp.zeros_like(cnt_ref)

    nchunk = tm // SGU_CHUNK
    vn = vn_ref[...]
    lane = lax.broadcasted_iota(jnp.int32, (SGU_CHUNK, LANES), 1)
    low = lane < SGU_GROUP_DIM
    mixed_cols = []
    for gp in range(SGU_W // LANES):
        rhs_cols = []
        for n in range(nchunk):
            blk = vn[n * SGU_CHUNK:(n + 1) * SGU_CHUNK, gp * LANES:(gp + 1) * LANES]
            zero = jnp.zeros_like(blk)
            rhs_cols.append(jnp.concatenate([jnp.where(low, blk, zero), jnp.where(low, zero, blk)], axis=0))
        rhs = jnp.concatenate(rhs_cols, axis=1)
        mix = _dot(ws_ref[gp], rhs)
        mixed_cols.append(jnp.concatenate([mix[:, n * LANES:(n + 1) * LANES] for n in range(nchunk)], axis=0))
    mixed = jnp.concatenate(mixed_cols, axis=1)
    bias = jnp.concatenate([bs_ref[...]] * nchunk, axis=0)
    o_sg = gu_ref[...].astype(_f32) * (mixed + bias)
    o_na = jnp.concatenate([o_ref[hp].astype(_f32) for hp in range(HEAD_PAIRS)], axis=1)

    def rms(t):
        return t * lax.rsqrt(jnp.mean(t * t, -1, keepdims=True) + EPS)

    y = jnp.concatenate([rms(o_na), rms(o_sg)], axis=1) * og_ref[...]
    xn = x_ref[...] + ga_ref[...] * _dot(y.astype(_bf16), wout_ref[...])
    xn_ref[...] = xn
    hf = rms(xn) * (1.0 + scf_ref[...]) + shf_ref[...]
    row_ref[:, :D_MODEL] = hf

    lg = _dot_nt(wr_ref[...], hf.astype(_bf16)) + br_ref[...]
    quad = lambda r: lg[r:r + EXPERTS_PER_GROUP, :]
    row4 = lax.broadcasted_iota(jnp.int32, (EXPERTS_PER_GROUP, tm), 0).astype(_f32)
    over = lambda fn, t: fn(t, 0, keepdims=True)
    gl = quad(0)
    gm = over(jnp.max, gl)
    p_top = 1.0 / over(jnp.sum, jnp.exp(gl - gm))
    g_idx = over(jnp.min, jnp.where(gl == gm, row4, float(N_GROUPS)))
    el = quad(SUBLANES * N_GROUPS)
    for g in range(N_GROUPS - 2, -1, -1):
        el = jnp.where(g_idx == float(g), quad(SUBLANES * (g + 1)), el)
    ee = jnp.exp(el - over(jnp.max, el))
    pe = ee / over(jnp.sum, ee)
    v1 = over(jnp.max, pe)
    i1 = over(jnp.min, jnp.where(pe == v1, row4, float(EXPERTS_PER_GROUP)))
    pe2 = jnp.where(row4 == i1, -1.0, pe)
    v2 = over(jnp.max, pe2)
    i2 = over(jnp.min, jnp.where(pe2 == v2, row4, float(EXPERTS_PER_GROUP)))
    den = v1 + v2
    w1 = v1 / den * p_top
    w2 = v2 / den * p_top
    first_low = i1 < i2
    a = jnp.minimum(i1, i2)
    b = jnp.maximum(i1, i2)
    lex = a * (7.0 - a) * 0.5 + (b - a - 1.0)
    pair = lex
    for src, dst in enumerate(PAIR_OF_LEX):
        if src != dst:
            pair = jnp.where(lex == float(src), float(dst), pair)
    cls = g_idx * float(N_PAIRS) + pair
    gate_a = jnp.where(first_low, w1, w2)
    gate_b = jnp.where(first_low, w2, w1)

    row_p = lax.broadcasted_iota(jnp.int32, (2 * SUBLANES, tm), 0)
    stacked = jnp.zeros((2 * SUBLANES, tm), _f32)
    for n, gate in enumerate((gate_a, gate_b)):
        rest = gate
        for k in range(GATE_PIECES):
            piece = rest.astype(_bf16).astype(_f32)
            stacked = jnp.where(row_p == n * GATE_PIECES + k, piece, stacked)
            rest = rest - piece
    row_ref[:, D_MODEL:] = lax.dot_general(stacked.astype(_bf16), sel_ref[...],
                                           (((0,), (0,)), ((), ())), preferred_element_type=_f32)

    row_c = lax.broadcasted_iota(jnp.int32, (CLASS_ROWS, tm), 0).astype(_f32)
    onehot = row_c == cls
    ones = jnp.where(onehot, 1.0, 0.0)
    before = _dot(ones.astype(_bf16), tri_ref[...])
    rank = over(jnp.sum, jnp.where(onehot, before + cnt_ref[:, 0:1], 0.0))
    cnt_ref[...] += jnp.sum(ones, 1, keepdims=True)
    meta_ref[...] = jnp.concatenate([cls, rank, jnp.zeros((SUBLANES - 2, tm), _f32)], axis=0)


def _merge(x, o_na, gu, vn, g_a, sh_f, sc_f, ws_cat, bs_tab, out_gain, w_out, w_r, b_r, tri, sel, tm):
    bsz, length, _ = x.shape
    mod = pl.BlockSpec((None, 1, D_MODEL), lambda b, i: (b, 0, 0))
    tok = lambda n: pl.BlockSpec((None, tm, n), lambda b, i: (b, i, 0))
    const = lambda *shape: pl.BlockSpec(shape, lambda b, i: (0,) * len(shape))
    return pl.pallas_call(
        functools.partial(_merge_kernel, tm=tm),
        grid=(bsz, length // tm),
        in_specs=[tok(D_MODEL),
                  pl.BlockSpec((None, HEAD_PAIRS, tm, LANES), lambda b, i: (b, 0, i, 0)),
                  tok(SGU_W), tok(SGU_W), mod, mod, mod,
                  const(SGU_W // LANES, SGU_CHUNK, 2 * SGU_CHUNK),
                  const(SGU_CHUNK, SGU_W),
                  const(1, D_MODEL),
                  const(D_MODEL, D_MODEL),
                  const(ROUTER_ROWS, D_MODEL),
                  const(ROUTER_ROWS, 1),
                  const(tm, tm),
                  const(2 * SUBLANES, LANES)],
        out_specs=[tok(D_MODEL), tok(ROW_W),
                   pl.BlockSpec((None, SUBLANES, tm), lambda b, i: (b, 0, i)),
                   const(CLASS_ROWS, LANES)],
        out_shape=[jax.ShapeDtypeStruct((bsz, length, D_MODEL), _f32),
                   jax.ShapeDtypeStruct((bsz, length, ROW_W), _f32),
                   jax.ShapeDtypeStruct((bsz, SUBLANES, length), _f32),
                   jax.ShapeDtypeStruct((CLASS_ROWS, LANES), _f32)],
        compiler_params=_params("arbitrary", "arbitrary"),
        name="merge",
    )(x, o_na, gu, vn, g_a, sh_f, sc_f, ws_cat, bs_tab, out_gain, w_out, w_r, b_r, tri, sel)


def _issue_row_copies(count, make_copy):
    for j in range(count):
        make_copy(j).start(priority=j % 2)


def _dispatch_kernel(fill_ref, pos_ref, rows_ref, xs_ref, zero_ref, sem, fill_sem, *, td, tile):
    @pl.when(pl.program_id(0) == 0)
    def _():
        zero_ref[...] = jnp.zeros_like(zero_ref)

        def fill(k):
            start = pl.multiple_of(jnp.maximum(fill_ref[k], 0), tile)
            return pltpu.make_async_copy(zero_ref, xs_ref.at[pl.ds(start, tile)], fill_sem)

        for k in range(fill_ref.shape[0]):
            @pl.when(fill_ref[k] >= 0)
            def _():
                fill(k).start()
        for k in range(fill_ref.shape[0]):
            @pl.when(fill_ref[k] >= 0)
            def _():
                fill(k).wait()

    _scatter_rows(pos_ref, rows_ref, xs_ref, sem, td)


def _scatter_rows(pos_ref, rows_ref, xs_ref, sem, td):
    _issue_row_copies(td, lambda j: pltpu.make_async_copy(
        rows_ref.at[pl.ds(j, 1)], xs_ref.at[pl.ds(pos_ref[0, j], 1)], sem))
    pltpu.make_async_copy(rows_ref, xs_ref.at[pl.ds(0, td)], sem).wait()


def _dispatch_more_kernel(pos_ref, rows_ref, base_ref, xs_ref, sem, *, td):
    del base_ref
    _scatter_rows(pos_ref, rows_ref, xs_ref, sem, td)


def _dispatch(rows, pos, fill, padded, td, tile):
    n = rows.shape[0]
    grid_spec = pltpu.PrefetchScalarGridSpec(
        num_scalar_prefetch=1,
        grid=(n // td,),
        in_specs=[pl.BlockSpec((None, 1, td), lambda i, fl: (i, 0, 0), memory_space=pltpu.SMEM),
                  pl.BlockSpec((td, ROW_W), lambda i, fl: (i, 0))],
        out_specs=pl.BlockSpec(memory_space=pl.ANY),
        scratch_shapes=[pltpu.VMEM((tile, ROW_W), _f32), pltpu.SemaphoreType.DMA(()), pltpu.SemaphoreType.DMA(())],
    )
    return pl.pallas_call(
        functools.partial(_dispatch_kernel, td=td, tile=tile),
        grid_spec=grid_spec,
        out_shape=jax.ShapeDtypeStruct((padded, ROW_W), _f32),
        compiler_params=_params("arbitrary"),
        name="dispatch",
    )(fill, pos.reshape(n // td, 1, td), rows)


def _dispatch_more(xs, rows, pos, td):
    n = rows.shape[0]
    return pl.pallas_call(
        functools.partial(_dispatch_more_kernel, td=td),
        grid=(n // td,),
        in_specs=[pl.BlockSpec((None, 1, td), lambda i: (i, 0, 0), memory_space=pltpu.SMEM),
                  pl.BlockSpec((td, ROW_W), lambda i: (i, 0)),
                  pl.BlockSpec(memory_space=pl.ANY)],
        out_specs=pl.BlockSpec(memory_space=pl.ANY),
        out_shape=jax.ShapeDtypeStruct(xs.shape, _f32),
        scratch_shapes=[pltpu.SemaphoreType.DMA(())],
        input_output_aliases={2: 0},
        compiler_params=_params("arbitrary"),
        name="dispatch_more",
    )(pos.reshape(n // td, 1, td), rows, xs)


def _combine_kernel(pos_ref, npos_ref, x_ref, g_ref, ys_ref, o_ref, buf0, buf1, sems, *, tc, nsteps):
    s = pl.program_id(0) * pl.num_programs(1) + pl.program_id(1)
    bufs = (buf0, buf1)

    def request(idx_ref, slot):
        _issue_row_copies(tc, lambda j: pltpu.make_async_copy(
            ys_ref.at[pl.ds(idx_ref[0, j], 1)], bufs[slot].at[pl.ds(j, 1)], sems.at[slot]))

    @pl.when(s == 0)
    def _():
        request(pos_ref, 0)

    for slot in range(2):
        @pl.when((s % 2 == slot) & (s + 1 < nsteps))
        def _():
            request(npos_ref, 1 - slot)

        @pl.when(s % 2 == slot)
        def _():
            pltpu.make_async_copy(ys_ref.at[pl.ds(0, tc)], bufs[slot], sems.at[slot]).wait()
            o_ref[...] = x_ref[...] + g_ref[...] * bufs[slot][...]


def _combine(x, ys, pos, g, tc):
    bsz, length, _ = x.shape
    per_batch = length // tc
    nsteps = bsz * per_batch
    tok = pl.BlockSpec((None, tc, D_MODEL), lambda b, i: (b, i, 0))
    pos3 = pos.reshape(nsteps, 1, tc)
    return pl.pallas_call(
        functools.partial(_combine_kernel, tc=tc, nsteps=nsteps),
        grid=(bsz, per_batch),
        in_specs=[pl.BlockSpec((None, 1, tc), lambda b, i: (b * per_batch + i, 0, 0), memory_space=pltpu.SMEM),
                  pl.BlockSpec((None, 1, tc), lambda b, i: (jnp.minimum(b * per_batch + i + 1, nsteps - 1), 0, 0),
                               memory_space=pltpu.SMEM),
                  tok,
                  pl.BlockSpec((None, 1, D_MODEL), lambda b, i: (b, 0, 0)),
                  pl.BlockSpec(memory_space=pl.ANY)],
        out_specs=tok,
        out_shape=jax.ShapeDtypeStruct(x.shape, _f32),
        scratch_shapes=[pltpu.VMEM((tc, D_MODEL), _f32), pltpu.VMEM((tc, D_MODEL), _f32),
                        pltpu.SemaphoreType.DMA((2,))],
        compiler_params=_params("arbitrary", "arbitrary"),
        name="combine",
    )(pos3, pos3, x, g, ys)


def _experts_kernel(ea_ref, eb_ref, chg_ref, nxt_ref, nlive_ref, xs_ref, w1_ref, w3_ref, w2_ref, y_ref,
                    b1, b3, b2, s1, s3, s2, sems, *, layer):
    t = pl.program_id(0)
    live = t < nlive_ref[0]
    first = (ea_ref[t] // EXPERTS_PER_GROUP) * EXPERTS_PER_GROUP

    def stage(expert, slot):
        return [pltpu.make_async_copy(w.at[layer, expert], s.at[slot], sems.at[slot])
                for w, s in ((w1_ref, s1), (w3_ref, s3), (w2_ref, s2))]

    @pl.when(live & ((chg_ref[t] & 3) == 1))
    def _():
        for k in range(2):
            for cp in stage(first + k, k):
                cp.start()

    @pl.when(live & ((chg_ref[t] & 1) != 0))
    def _():
        for k in range(EXPERTS_PER_GROUP):
            slot = k % 2
            for cp in stage(first + k, slot):
                cp.wait()
            b1[k] = s1[slot].astype(_bf16)
            b3[k] = s3[slot].astype(_bf16)
            b2[k] = s2[slot].astype(_bf16)
            if k + 2 < EXPERTS_PER_GROUP:
                for cp in stage(first + k + 2, slot):
                    cp.start()

        @pl.when(nxt_ref[t] >= 0)
        def _():
            for k in range(2):
                for cp in stage(nxt_ref[t] + k, k):
                    cp.start()

    def run(nrows):
        xs = xs_ref[:nrows, :D_MODEL].astype(_bf16)
        gates = xs_ref[:nrows, D_MODEL:]

        def expert(e):
            hid = jax.nn.silu(_dot(xs, b1[e])) * _dot(xs, b3[e])
            return _dot(hid.astype(_bf16), b2[e])

        y_ref[:nrows, :] = (gates[:, 0:1] * expert(ea_ref[t] - first) + gates[:, 1:2] * expert(eb_ref[t] - first))

    half_only = (chg_ref[t] & 4) != 0
    half = y_ref.shape[0] // 2

    @pl.when(live & jnp.logical_not(half_only))
    def _():
        run(y_ref.shape[0])

    @pl.when(live & half_only)
    def _():
        run(half)
        y_ref[half:, :] = jnp.zeros((half, D_MODEL), _f32)

    @pl.when(jnp.logical_not(live))
    def _():
        y_ref[...] = jnp.zeros_like(y_ref)


def _experts(xs, tile_ea, tile_eb, chg, nxt, nlive, w1, w3, w2, layer, tile):
    ntiles = xs.shape[0] // tile
    hbm = pl.BlockSpec(memory_space=pl.ANY)
    up, down = (D_MODEL, D_EXPERT), (D_EXPERT, D_MODEL)
    grid_spec = pltpu.PrefetchScalarGridSpec(
        num_scalar_prefetch=5,
        grid=(ntiles,),
        in_specs=[pl.BlockSpec((tile, ROW_W), lambda t, *_: (t, 0)), hbm, hbm, hbm],
        out_specs=pl.BlockSpec((tile, D_MODEL), lambda t, *_: (t, 0)),
        scratch_shapes=[pltpu.VMEM((EXPERTS_PER_GROUP,) + up, _bf16), pltpu.VMEM((EXPERTS_PER_GROUP,) + up, _bf16),
                        pltpu.VMEM((EXPERTS_PER_GROUP,) + down, _bf16),
                        pltpu.VMEM((2,) + up, _f32), pltpu.VMEM((2,) + up, _f32), pltpu.VMEM((2,) + down, _f32),
                        pltpu.SemaphoreType.DMA((2,))],
    )
    return pl.pallas_call(
        functools.partial(_experts_kernel, layer=layer),
        grid_spec=grid_spec,
        out_shape=jax.ShapeDtypeStruct((xs.shape[0], D_MODEL), _f32),
        compiler_params=_params("arbitrary"),
        name="experts",
    )(tile_ea, tile_eb, chg, nxt, nlive, xs, w1, w3, w2)


def _dispatch_plan(cls, rank, counts, first_counts, later, tile):
    n = cls.shape[0]
    padded = -(-(n + N_CLASSES * (tile - 1)) // tile) * tile
    ntiles = padded // tile
    pcounts = (counts + tile - 1) // tile * tile
    pad_end = jnp.cumsum(pcounts)
    pad_off = pad_end - pcounts
    pos = pad_off[cls] + rank
    nlive = pad_end[-1] // tile
    tile_start = jnp.arange(ntiles, dtype=jnp.int32) * tile
    tile_cls = jnp.sum((pad_end[None, :] <= jnp.minimum(tile_start, pad_end[-1] - 1)[:, None]).astype(jnp.int32), axis=1)
    tile_cls = jnp.minimum(tile_cls, N_CLASSES - 1)
    grp = tile_cls // N_PAIRS
    pair = tile_cls % N_PAIRS
    tile_ea = grp * EXPERTS_PER_GROUP + jnp.asarray(PAIR_A, jnp.int32)[pair]
    tile_eb = grp * EXPERTS_PER_GROUP + jnp.asarray(PAIR_B, jnp.int32)[pair]
    tile_id = jnp.arange(ntiles, dtype=jnp.int32)
    new_group = jnp.concatenate([jnp.ones((1,), jnp.int32), (grp[1:] != grp[:-1]).astype(jnp.int32)])
    new_group = new_group * (tile_id < nlive).astype(jnp.int32)
    tokens_in_tile = (pad_off + counts)[tile_cls] - tile_start
    chg = new_group * (1 + 2 * (tile_id > 0).astype(jnp.int32)) + 4 * (tokens_in_tile <= tile // 2).astype(jnp.int32)
    start_at = jnp.where(new_group > 0, tile_id, ntiles)
    next_start = jnp.flip(lax.cummin(jnp.flip(jnp.concatenate([start_at[1:], jnp.full((1,), ntiles, jnp.int32)]))))
    nxt = jnp.where(next_start < ntiles, (grp * EXPERTS_PER_GROUP)[jnp.minimum(next_start, ntiles - 1)], -1)
    per_class = -(-(later + tile - 1) // tile) + 1
    uncovered = pad_off + first_counts
    tails = []
    for j in range(per_class):
        start = pad_end - (j + 1) * tile
        tails.append(jnp.where((start >= pad_off) & (start + tile > uncovered), start, -1))
    past = (nlive + jnp.arange(N_CLASSES, dtype=jnp.int32)) * tile
    fill = jnp.concatenate(tails + [jnp.where(past < padded, past, -1)]).astype(jnp.int32)
    return (pos.astype(jnp.int32), padded, tile_ea, tile_eb, chg, nxt.astype(jnp.int32),
            nlive.astype(jnp.int32).reshape(1), fill)


def _ffn(streams, w1, w3, w2, layer, tile):
    sizes = [s[0].shape[0] * s[0].shape[1] for s in streams]
    cls = [s[2][:, 0, :].reshape(n).astype(jnp.int32) for s, n in zip(streams, sizes)]
    cnts = [s[3][:N_CLASSES, 0].astype(jnp.int32) for s in streams]
    ranks, before = [], jnp.zeros((N_CLASSES,), jnp.int32)
    for s, n, c, cnt in zip(streams, sizes, cls, cnts):
        ranks.append(s[2][:, 1, :].reshape(n).astype(jnp.int32) + before[c])
        before = before + cnt
    pos, padded, tile_ea, tile_eb, chg, nxt, nlive, fill = _dispatch_plan(
        jnp.concatenate(cls), jnp.concatenate(ranks), before, cnts[0], sum(sizes[1:]), tile)
    starts = np.cumsum([0] + sizes)
    pos = [pos[a:b] for a, b in zip(starts[:-1], starts[1:])]
    xs = _dispatch(streams[0][1].reshape(sizes[0], ROW_W), pos[0], fill, padded, streams[0][5], tile)
    for s, n, p in zip(streams[1:], sizes[1:], pos[1:]):
        xs = _dispatch_more(xs, s[1].reshape(n, ROW_W), p, s[5])
    ys = _experts(xs, tile_ea, tile_eb, chg, nxt, nlive, w1, w3, w2, layer, tile)
    return [_combine(s[0], ys, p, s[4], s[5]) for s, p in zip(streams, pos)]


def kernel(x, c, ctx, c_ctx, w_ada, b_ada, w_in, q_gain, k_gain, rpb, sgu_ln, sgu_w, sgu_b, out_gain,
           w_out, rg_w, rg_b, re_w, re_b, w1, w3, w2):
    bsz, seq, _ = x.shape
    depth = w_ada.shape[0]
    rows = seq // GRID_W
    assert seq % Q_BLOCK == 0 and rows // Q_ROWS >= 3 and ctx.shape[1] % SGU_CHUNK == 0
    tm_x, tm_c = 512, ctx.shape[1]

    cond = jnp.zeros((SUBLANES, D_MODEL), _f32).at[:bsz].set(c).at[bsz].set(c_ctx)
    mods = _adaln(cond, w_ada, b_ada).reshape(depth, SUBLANES, N_MOD, D_MODEL)

    assert MXU_TILE % NA_HEAD_DIM == 0 and NA_W % MXU_TILE == 0
    head_id = np.arange(MXU_TILE) // NA_HEAD_DIM
    avg = jnp.asarray((head_id[:, None] == head_id[None, :]).astype(np.float32) / NA_HEAD_DIM, _bf16)
    win_plan = _window_plan(rows)
    tri = {tm: jnp.asarray(np.triu(np.ones((tm, tm), np.float32), 1), _bf16) for tm in {tm_x, tm_c}}
    sel_np = np.zeros((2 * SUBLANES, LANES), np.float32)
    for n in range(2):
        sel_np[n * GATE_PIECES:(n + 1) * GATE_PIECES, n] = 1.0
    sel = jnp.asarray(sel_np, _bf16)

    for i in range(depth):
        last = i == depth - 1
        mx = [mods[i, :bsz, j][:, None, :] for j in range(N_MOD)]
        mc = [jnp.broadcast_to(mods[i, bsz, j][None, None, :], (bsz, 1, D_MODEL)) for j in range(N_MOD)]
        w_in_b = jnp.concatenate([w_in[i, :, 3 * NA_W:], w_in[i, :, :3 * NA_W]], axis=1).astype(_bf16)
        qg = (jnp.tile(q_gain[i], NA_HEADS) * NA_HEAD_DIM ** -0.5)[None, :]
        kg = jnp.tile(k_gain[i], NA_HEADS)[None, :]
        ln_g = sgu_ln[i][None, :]
        ws_cat = jnp.concatenate([sgu_w[i, 0::2], sgu_w[i, 1::2]], axis=2).astype(_bf16)
        bs_tab = jnp.repeat(sgu_b[i].T, SGU_GROUP_DIM, axis=1)
        og = out_gain[i][None, :]
        w_out_b = w_out[i].astype(_bf16)
        pad_w = jnp.zeros((SUBLANES - EXPERTS_PER_GROUP, D_MODEL), _f32)
        pad_b = jnp.zeros((SUBLANES - EXPERTS_PER_GROUP,), _f32)
        w_r = jnp.concatenate([rg_w[i].T, pad_w] + [t for g in range(N_GROUPS) for t in (re_w[i, g].T, pad_w)]
                              + [jnp.zeros((ROUTER_ROWS - SUBLANES * (N_GROUPS + 1), D_MODEL), _f32)]).astype(_bf16)
        b_r = jnp.concatenate([rg_b[i], pad_b] + [t for g in range(N_GROUPS) for t in (re_b[i, g], pad_b)]
                              + [jnp.zeros((ROUTER_ROWS - SUBLANES * (N_GROUPS + 1),), _f32)])[:, None]
        score_bound = 1.02 * NA_HEAD_DIM ** 0.5 * jnp.max(jnp.abs(q_gain[i])) * jnp.max(jnp.abs(k_gain[i]))
        shift = (score_bound + jnp.maximum(jnp.max(rpb[i]), 0.0)).reshape(1)
        spread = 2.0 * score_bound + jnp.maximum(jnp.max(rpb[i]), 0.0) - jnp.minimum(jnp.min(rpb[i]), 0.0)

        q, k, v, gu, vn = _proj(x, mx[0], mx[1], w_in_b, avg, qg, kg, ln_g, tm_x)
        qc, kc, vc, guc, vnc = _proj(ctx, mc[0], mc[1], w_in_b, avg, qg, kg, ln_g, tm_c)
        o_na = lax.cond(
            spread < MAX_SHIFT_SPREAD,
            lambda: _attention(q, k, v, kc, vc, _window_col_table(rpb[i], shift[0]), win_plan, shift, True),
            lambda: _attention(q, k, v, kc, vc, _window_col_table(rpb[i], 0.0), win_plan, shift, False))
        x_new, xrows, meta, counts = _merge(x, o_na, gu, vn, mx[2], mx[3], mx[4], ws_cat, bs_tab, og,
                                            w_out_b, w_r, b_r, tri[tm_x], sel, tm_x)
        streams = [(x_new, xrows, meta, counts, mx[5], 1024)]
        if not last:
            oc_na = _ctx_attention(qc, kc, vc)
            c_new, crows, cmeta, ccounts = _merge(ctx, oc_na, guc, vnc, mc[2], mc[3], mc[4], ws_cat, bs_tab,
                                                  og, w_out_b, w_r, b_r, tri[tm_c], sel, tm_c)
            streams.append((c_new, crows, cmeta, ccounts, mc[5], tm_c))
        outs = _ffn(streams, w1, w3, w2, i, 512)
        x = outs[0]
        if not last:
            ctx = outs[1]
    return x
```

```python
import functools

import jax
import jax.numpy as jnp
import numpy as np
from jax import lax
from jax.experimental import pallas as pl
from jax.experimental.pallas import tpu as pltpu

D_MODEL = 1024
GRID_W = 64
NA_HEADS = 8
NA_HEAD_DIM = 64
NA_W = NA_HEADS * NA_HEAD_DIM
NA_KH = 8
NA_KW = 16
SGU_GROUPS = 8
SGU_W = D_MODEL - NA_W
SGU_GROUP_DIM = SGU_W // SGU_GROUPS
SGU_CHUNK = 128
IN_COLS = 3 * NA_W + 2 * SGU_W
N_GROUPS = 4
EXPERTS_PER_GROUP = 4
N_EXPERTS = N_GROUPS * EXPERTS_PER_GROUP
D_EXPERT = D_MODEL // 2
N_MOD = 6
EPS = 1e-6

LANES = 128
SUBLANES = 8
HEAD_PAIRS = NA_W // LANES
Q_ROWS = 8
Q_BLOCK = Q_ROWS * GRID_W
WIN_ROWS = Q_ROWS + NA_KH - 1
WIN_BLOCK = WIN_ROWS * GRID_W
PAIR_ROWS = 1
PAIR_BLOCK = PAIR_ROWS * GRID_W
PAIR_WIN_ROWS = PAIR_ROWS + NA_KH - 1
PAIR_WIN = PAIR_WIN_ROWS * GRID_W
PAIRS_IN_FLIGHT = 4
N_DR = 2 * NA_KH
N_PAIRS = 6
N_CLASSES = N_GROUPS * N_PAIRS
PAIR_A = (0, 0, 1, 1, 0, 2)
PAIR_B = (1, 2, 2, 3, 3, 3)
PAIR_OF_LEX = (0, 1, 4, 2, 3, 5)
CLASS_ROWS = 32
ROUTER_ROWS = 48
GATE_PIECES = 3
ROW_W = D_MODEL + LANES
MXU_TILE = 256
MASKED = -1e30
MAX_SHIFT_SPREAD = 60.0
VMEM_LIMIT_BYTES = 56 * 1024 * 1024

_f32 = jnp.float32
_bf16 = jnp.bfloat16


def _params(*semantics):
    return pltpu.CompilerParams(dimension_semantics=semantics, vmem_limit_bytes=VMEM_LIMIT_BYTES)


def _dot(a, b):
    return jnp.dot(a, b, preferred_element_type=_f32)


def _dot_nt(a, b):
    return lax.dot_general(a, b, (((1,), (1,)), ((), ())), preferred_element_type=_f32)


def _adaln_kernel(cond_ref, w_ref, b_ref, out_ref):
    cond = cond_ref[...]
    act = cond * jax.nn.sigmoid(cond)
    out_ref[...] = _dot(act.astype(_bf16), w_ref[...].astype(_bf16)) + b_ref[...]


def _adaln(cond, w_ada, b_ada):
    depth = w_ada.shape[0]
    cols = N_MOD * D_MODEL
    col_block = 1536
    return pl.pallas_call(
        _adaln_kernel,
        grid=(depth, cols // col_block),
        in_specs=[
            pl.BlockSpec((SUBLANES, D_MODEL), lambda d, j: (0, 0)),
            pl.BlockSpec((None, D_MODEL, col_block), lambda d, j: (d, 0, j)),
            pl.BlockSpec((None, 1, col_block), lambda d, j: (d, 0, j)),
        ],
        out_specs=pl.BlockSpec((None, SUBLANES, col_block), lambda d, j: (d, 0, j)),
        out_shape=jax.ShapeDtypeStruct((depth, SUBLANES, cols), _f32),
        compiler_params=_params("arbitrary", "arbitrary"),
        name="adaln",
    )(cond, w_ada, b_ada.reshape(depth, 1, cols))


def _proj_kernel(x_ref, sh_ref, sc_ref, w_ref, avg_ref, qg_ref, kg_ref, ln_ref,
                 q_ref, k_ref, v_ref, gu_ref, vn_ref):
    x = x_ref[...]
    h = x * lax.rsqrt(jnp.mean(x * x, -1, keepdims=True) + EPS) * (1.0 + sc_ref[...]) + sh_ref[...]
    p = _dot(h.astype(_bf16), w_ref[...])
    u = p[:, :SGU_W]
    vs = p[:, SGU_W:2 * SGU_W]
    q = p[:, 2 * SGU_W:2 * SGU_W + NA_W]
    k = p[:, 2 * SGU_W + NA_W:2 * SGU_W + 2 * NA_W]
    v = p[:, 2 * SGU_W + 2 * NA_W:]
    def head_mean(t):
        sq = (t * t).astype(_bf16)
        return jnp.concatenate([_dot(sq[:, c:c + MXU_TILE], avg_ref[...]) for c in range(0, NA_W, MXU_TILE)], axis=1)

    q_ms = head_mean(q)
    k_ms = head_mean(k)
    qn = (q * lax.rsqrt(q_ms + EPS) * qg_ref[...]).astype(_bf16)
    kn = (k * lax.rsqrt(k_ms + EPS) * kg_ref[...]).astype(_bf16)
    vb = v.astype(_bf16)
    for hp in range(HEAD_PAIRS):
        cols = slice(hp * LANES, (hp + 1) * LANES)
        q_ref[hp] = qn[:, cols]
        k_ref[hp] = kn[:, cols]
        v_ref[hp] = vb[:, cols]
    gu_ref[...] = jax.nn.gelu(u).astype(_bf16)
    gv = jax.nn.gelu(vs)
    mu = jnp.mean(gv, -1, keepdims=True)
    cen = gv - mu
    var = jnp.mean(cen * cen, -1, keepdims=True)
    vn_ref[...] = (cen * lax.rsqrt(var + EPS) * ln_ref[...]).astype(_bf16)


def _proj(x, sh, sc, w_in, avg, qg, kg, ln_g, tm):
    bsz, length, _ = x.shape
    vec = lambda n: pl.BlockSpec((1, n), lambda b, i: (0, 0))
    mod = pl.BlockSpec((None, 1, D_MODEL), lambda b, i: (b, 0, 0))
    tok = lambda n: pl.BlockSpec((None, tm, n), lambda b, i: (b, i, 0))
    hpm = pl.BlockSpec((None, HEAD_PAIRS, tm, LANES), lambda b, i: (b, 0, i, 0))
    hpm_shape = jax.ShapeDtypeStruct((bsz, HEAD_PAIRS, length, LANES), _bf16)
    sgu_shape = jax.ShapeDtypeStruct((bsz, length, SGU_W), _bf16)
    return pl.pallas_call(
        _proj_kernel,
        grid=(bsz, length // tm),
        in_specs=[tok(D_MODEL), mod, mod,
                  pl.BlockSpec((D_MODEL, IN_COLS), lambda b, i: (0, 0)),
                  pl.BlockSpec((MXU_TILE, MXU_TILE), lambda b, i: (0, 0)),
                  vec(NA_W), vec(NA_W), vec(SGU_W)],
        out_specs=[hpm, hpm, hpm, tok(SGU_W), tok(SGU_W)],
        out_shape=[hpm_shape, hpm_shape, hpm_shape, sgu_shape, sgu_shape],
        compiler_params=_params("arbitrary", "arbitrary"),
        name="proj",
    )(x, sh, sc, w_in, avg, qg, kg, ln_g)


def _attend(blocks, shift=None):
    add = lambda a, b: a + b
    m_rows = blocks[0][0].shape[0]
    lane = lax.broadcasted_iota(jnp.int32, blocks[0][0].shape, 1)
    low = lane < NA_HEAD_DIM
    stacked = [jnp.concatenate([jnp.where(low, q, jnp.zeros_like(q)), jnp.where(low, jnp.zeros_like(q), q)], axis=0)
               for q, _, _, _ in blocks]
    if shift is None:
        scores = [[_dot_nt(qs, key) if bias is None else _dot_nt(qs, key) + bias[...]
                   for key, bias in zip(keys, biases)] for qs, (_, keys, _, biases) in zip(stacked, blocks)]
        maxes = [functools.reduce(jnp.maximum, [jnp.max(s, -1, keepdims=True) for s in ss]) for ss in scores]
        probs = [[jnp.exp(s - m) for s in ss] for ss, m in zip(scores, maxes)]
    else:
        probs = [[jnp.exp(_dot_nt(qs, key) - shift) if bias is None else jnp.exp(_dot_nt(qs, key) + bias[...])
                  for key, bias in zip(keys, biases)] for qs, (_, keys, _, biases) in zip(stacked, blocks)]
    dens = [functools.reduce(add, [jnp.sum(p, -1, keepdims=True) for p in ps]) for ps in probs]
    accs = [functools.reduce(add, [_dot(p.astype(_bf16), val) for p, val in zip(ps, blk[2])])
            for ps, blk in zip(probs, blocks)]
    outs = [acc / den for acc, den in zip(accs, dens)]
    return [jnp.where(low, o2[:m_rows], o2[m_rows:]) for o2 in outs]


def _attn_kernel(off_ref, var_ref, shift_ref, q_ref, k_ref, v_ref, kc_ref, vc_ref, bycol_ref, o_ref, bias_ref,
                 *, pieces, shifted):
    i = pl.program_id(2)
    shift = shift_ref[0] if shifted else None

    @pl.when(i == 0)
    def _():
        for var, piece in enumerate(pieces):
            for hh in range(2):
                for j in range(PAIR_ROWS):
                    for kr in range(PAIR_WIN_ROWS):
                        lo = (kr % 2) * GRID_W
                        row = hh * PAIR_BLOCK + j * GRID_W
                        bias_ref[var, row:row + GRID_W, kr * GRID_W:(kr + 1) * GRID_W] = (
                            bycol_ref[hh, piece[j][kr], :, lo:lo + GRID_W])

    kc = kc_ref[...]
    vc = vc_ref[...]
    npair = Q_ROWS // PAIR_ROWS
    for g0 in range(0, npair, PAIRS_IN_FLIGHT):
        blocks = []
        for g in range(g0, g0 + PAIRS_IN_FLIGHT):
            start = pl.multiple_of(off_ref[i * npair + g] * GRID_W, GRID_W)
            var = var_ref[i * npair + g]
            blocks.append((q_ref[g * PAIR_BLOCK:(g + 1) * PAIR_BLOCK, :],
                           [k_ref[pl.ds(start, PAIR_WIN), :], kc],
                           [v_ref[pl.ds(start, PAIR_WIN), :], vc],
                           [bias_ref.at[var], None]))
        for g, out in zip(range(g0, g0 + PAIRS_IN_FLIGHT), _attend(blocks, shift)):
            o_ref[g * PAIR_BLOCK:(g + 1) * PAIR_BLOCK, :] = out.astype(o_ref.dtype)


def _attention(q, k, v, kc, vc, bycol, plan, shift, shifted):
    off_tab, var_tab, pieces = plan
    bsz, _, seq, _ = q.shape
    clen = kc.shape[2]
    nblk = seq // Q_BLOCK

    def win_start(i):
        return jnp.clip(i * Q_ROWS - NA_KH // 2, 0, seq // GRID_W - WIN_ROWS) * GRID_W

    qspec = pl.BlockSpec((None, None, Q_BLOCK, LANES), lambda hp, b, i, *_: (b, hp, i, 0))
    wspec = pl.BlockSpec((None, None, pl.Element(WIN_BLOCK), pl.Element(LANES)),
                         lambda hp, b, i, *_: (b, hp, win_start(i), 0))
    cspec = pl.BlockSpec((None, None, clen, LANES), lambda hp, b, i, *_: (b, hp, 0, 0))
    tspec = pl.BlockSpec((None, 2, N_DR, GRID_W, LANES), lambda hp, b, i, *_: (hp, 0, 0, 0, 0))
    grid_spec = pltpu.PrefetchScalarGridSpec(
        num_scalar_prefetch=3,
        grid=(HEAD_PAIRS, bsz, nblk),
        in_specs=[qspec, wspec, wspec, cspec, cspec, tspec],
        out_specs=qspec,
        scratch_shapes=[pltpu.VMEM((len(pieces), 2 * PAIR_BLOCK, PAIR_WIN), _f32)],
    )
    return pl.pallas_call(
        functools.partial(_attn_kernel, pieces=pieces, shifted=shifted),
        grid_spec=grid_spec,
        out_shape=jax.ShapeDtypeStruct(q.shape, _bf16),
        compiler_params=_params("arbitrary", "arbitrary", "arbitrary"),
        name="attention_shifted" if shifted else "attention",
    )(off_tab, var_tab, shift, q, k, v, kc, vc, bycol)


def _ctx_attn_kernel(q_ref, k_ref, v_ref, o_ref):
    o_ref[...] = _attend([(q_ref[...], [k_ref[...]], [v_ref[...]], [None])])[0].astype(o_ref.dtype)


def _ctx_attention(q, k, v):
    bsz, _, clen, _ = q.shape
    spec = pl.BlockSpec((None, None, clen, LANES), lambda b, hp: (b, hp, 0, 0))
    return pl.pallas_call(
        _ctx_attn_kernel,
        grid=(bsz, HEAD_PAIRS),
        in_specs=[spec, spec, spec],
        out_specs=spec,
        out_shape=jax.ShapeDtypeStruct(q.shape, _bf16),
        compiler_params=_params("arbitrary", "arbitrary"),
        name="ctx_attention",
    )(q, k, v)


def _window_plan(rows):
    patterns, off_tab, var_tab = {}, [], []
    for r in range(0, rows, PAIR_ROWS):
        ws = int(np.clip(r - NA_KH // 2, 0, rows - PAIR_WIN_ROWS))
        r0 = r // Q_ROWS * Q_ROWS
        block_ws = int(np.clip(r0 - NA_KH // 2, 0, rows - WIN_ROWS))
        assert 0 <= ws - block_ws <= WIN_ROWS - PAIR_WIN_ROWS
        piece = []
        for j in range(PAIR_ROWS):
            r_start = int(np.clip(r + j - NA_KH // 2, 0, rows - NA_KH))
            assert ws <= r_start and r_start + NA_KH <= ws + PAIR_WIN_ROWS
            piece.append(tuple(ws + kr - (r + j) + NA_KH - 1 if r_start <= ws + kr < r_start + NA_KH else N_DR - 1
                               for kr in range(PAIR_WIN_ROWS)))
        var_tab.append(patterns.setdefault(tuple(piece), len(patterns)))
        off_tab.append(ws - block_ws)
    return jnp.asarray(off_tab, jnp.int32), jnp.asarray(var_tab, jnp.int32), tuple(patterns)


def _window_col_table(rpb, shift):
    col = np.arange(GRID_W)
    col_start = np.clip(col - NA_KW // 2, 0, GRID_W - NA_KW)
    col_ok = (col[None, :] >= col_start[:, None]) & (col[None, :] < col_start[:, None] + NA_KW)
    dc = np.where(col_ok, col[None, :] - col[:, None] + NA_KW - 1, 0)
    onehot = np.zeros((2 * NA_KW - 1, GRID_W * GRID_W), np.float32)
    onehot[dc.reshape(-1), np.arange(GRID_W * GRID_W)] = col_ok.reshape(-1)
    by_col = jnp.dot(rpb.reshape(-1, 2 * NA_KW - 1), jnp.asarray(onehot), precision=lax.Precision.HIGHEST)
    by_col = by_col.reshape(NA_HEADS, 2 * NA_KH - 1, GRID_W, GRID_W)
    by_col = jnp.where(jnp.asarray(col_ok)[None, None], by_col - shift, MASKED)
    by_col = jnp.concatenate([by_col, jnp.full((NA_HEADS, 1, GRID_W, GRID_W), MASKED, _f32)], axis=1)
    by_col = jnp.concatenate([by_col, by_col], axis=-1)
    return by_col.reshape(HEAD_PAIRS, 2, N_DR, GRID_W, LANES)


def _merge_kernel(x_ref, o_ref, gu_ref, vn_ref, ga_ref, shf_ref, scf_ref, ws_ref, bs_ref, og_ref,
                  wout_ref, wr_ref, br_ref, tri_ref, sel_ref, xn_ref, row_ref, meta_ref, cnt_ref, *, tm):
    first = (pl.program_id(0) == 0) & (pl.program_id(1) == 0)

    @pl.when(first)
    def _():
        cnt_ref[...] = jnp.zeros_like(cnt_ref)

    nchunk = tm // SGU_CHUNK
    vn = vn_ref[...]
    lane = lax.broadcasted_iota(jnp.int32, (SGU_CHUNK, LANES), 1)
    low = lane < SGU_GROUP_DIM
    mixed_cols = []
    for gp in range(SGU_W // LANES):
        rhs_cols = []
        for n in range(nchunk):
            blk = vn[n * SGU_CHUNK:(n + 1) * SGU_CHUNK, gp * LANES:(gp + 1) * LANES]
            zero = jnp.zeros_like(blk)
            rhs_cols.append(jnp.concatenate([jnp.where(low, blk, zero), jnp.where(low, zero, blk)], axis=0))
        rhs = jnp.concatenate(rhs_cols, axis=1)
        mix = _dot(ws_ref[gp], rhs)
        mixed_cols.append(jnp.concatenate([mix[:, n * LANES:(n + 1) * LANES] for n in range(nchunk)], axis=0))
    mixed = jnp.concatenate(mixed_cols, axis=1)
    bias = jnp.concatenate([bs_ref[...]] * nchunk, axis=0)
    o_sg = gu_ref[...].astype(_f32) * (mixed + bias)
    o_na = jnp.concatenate([o_ref[hp].astype(_f32) for hp in range(HEAD_PAIRS)], axis=1)

    def rms(t):
        return t * lax.rsqrt(jnp.mean(t * t, -1, keepdims=True) + EPS)

    y = jnp.concatenate([rms(o_na), rms(o_sg)], axis=1) * og_ref[...]
    xn = x_ref[...] + ga_ref[...] * _dot(y.astype(_bf16), wout_ref[...])
    xn_ref[...] = xn
    hf = rms(xn) * (1.0 + scf_ref[...]) + shf_ref[...]
    row_ref[:, :D_MODEL] = hf

    lg = _dot_nt(wr_ref[...], hf.astype(_bf16)) + br_ref[...]
    quad = lambda r: lg[r:r + EXPERTS_PER_GROUP, :]
    row4 = lax.broadcasted_iota(jnp.int32, (EXPERTS_PER_GROUP, tm), 0).astype(_f32)
    over = lambda fn, t: fn(t, 0, keepdims=True)
    gl = quad(0)
    gm = over(jnp.max, gl)
    p_top = 1.0 / over(jnp.sum, jnp.exp(gl - gm))
    g_idx = over(jnp.min, jnp.where(gl == gm, row4, float(N_GROUPS)))
    el = quad(SUBLANES * N_GROUPS)
    for g in range(N_GROUPS - 2, -1, -1):
        el = jnp.where(g_idx == float(g), quad(SUBLANES * (g + 1)), el)
    ee = jnp.exp(el - over(jnp.max, el))
    pe = ee / over(jnp.sum, ee)
    v1 = over(jnp.max, pe)
    i1 = over(jnp.min, jnp.where(pe == v1, row4, float(EXPERTS_PER_GROUP)))
    pe2 = jnp.where(row4 == i1, -1.0, pe)
    v2 = over(jnp.max, pe2)
    i2 = over(jnp.min, jnp.where(pe2 == v2, row4, float(EXPERTS_PER_GROUP)))
    den = v1 + v2
    w1 = v1 / den * p_top
    w2 = v2 / den * p_top
    first_low = i1 < i2
    a = jnp.minimum(i1, i2)
    b = jnp.maximum(i1, i2)
    lex = a * (7.0 - a) * 0.5 + (b - a - 1.0)
    pair = lex
    for src, dst in enumerate(PAIR_OF_LEX):
        if src != dst:
            pair = jnp.where(lex == float(src), float(dst), pair)
    cls = g_idx * float(N_PAIRS) + pair
    gate_a = jnp.where(first_low, w1, w2)
    gate_b = jnp.where(first_low, w2, w1)

    row_p = lax.broadcasted_iota(jnp.int32, (2 * SUBLANES, tm), 0)
    stacked = jnp.zeros((2 * SUBLANES, tm), _f32)
    for n, gate in enumerate((gate_a, gate_b)):
        rest = gate
        for k in range(GATE_PIECES):
            piece = rest.astype(_bf16).astype(_f32)
            stacked = jnp.where(row_p == n * GATE_PIECES + k, piece, stacked)
            rest = rest - piece
    row_ref[:, D_MODEL:] = lax.dot_general(stacked.astype(_bf16), sel_ref[...],
                                           (((0,), (0,)), ((), ())), preferred_element_type=_f32)

    row_c = lax.broadcasted_iota(jnp.int32, (CLASS_ROWS, tm), 0).astype(_f32)
    onehot = row_c == cls
    ones = jnp.where(onehot, 1.0, 0.0)
    before = _dot(ones.astype(_bf16), tri_ref[...])
    rank = over(jnp.sum, jnp.where(onehot, before + cnt_ref[:, 0:1], 0.0))
    cnt_ref[...] += jnp.sum(ones, 1, keepdims=True)
    meta_ref[...] = jnp.concatenate([cls, rank, jnp.zeros((SUBLANES - 2, tm), _f32)], axis=0)


def _merge(x, o_na, gu, vn, g_a, sh_f, sc_f, ws_cat, bs_tab, out_gain, w_out, w_r, b_r, tri, sel, tm):
    bsz, length, _ = x.shape
    mod = pl.BlockSpec((None, 1, D_MODEL), lambda b, i: (b, 0, 0))
    tok = lambda n: pl.BlockSpec((None, tm, n), lambda b, i: (b, i, 0))
    const = lambda *shape: pl.BlockSpec(shape, lambda b, i: (0,) * len(shape))
    return pl.pallas_call(
        functools.partial(_merge_kernel, tm=tm),
        grid=(bsz, length // tm),
        in_specs=[tok(D_MODEL),
                  pl.BlockSpec((None, HEAD_PAIRS, tm, LANES), lambda b, i: (b, 0, i, 0)),
                  tok(SGU_W), tok(SGU_W), mod, mod, mod,
                  const(SGU_W // LANES, SGU_CHUNK, 2 * SGU_CHUNK),
                  const(SGU_CHUNK, SGU_W),
                  const(1, D_MODEL),
                  const(D_MODEL, D_MODEL),
                  const(ROUTER_ROWS, D_MODEL),
                  const(ROUTER_ROWS, 1),
                  const(tm, tm),
                  const(2 * SUBLANES, LANES)],
        out_specs=[tok(D_MODEL), tok(ROW_W),
                   pl.BlockSpec((None, SUBLANES, tm), lambda b, i: (b, 0, i)),
                   const(CLASS_ROWS, LANES)],
        out_shape=[jax.ShapeDtypeStruct((bsz, length, D_MODEL), _f32),
                   jax.ShapeDtypeStruct((bsz, length, ROW_W), _f32),
                   jax.ShapeDtypeStruct((bsz, SUBLANES, length), _f32),
                   jax.ShapeDtypeStruct((CLASS_ROWS, LANES), _f32)],
        compiler_params=_params("arbitrary", "arbitrary"),
        name="merge",
    )(x, o_na, gu, vn, g_a, sh_f, sc_f, ws_cat, bs_tab, out_gain, w_out, w_r, b_r, tri, sel)


def _issue_row_copies(count, make_copy):
    for j in range(count):
        make_copy(j).start(priority=j % 2)


def _dispatch_kernel(fill_ref, pos_ref, rows_ref, xs_ref, zero_ref, sem, fill_sem, *, td, tile):
    @pl.when(pl.program_id(0) == 0)
    def _():
        zero_ref[...] = jnp.zeros_like(zero_ref)

        def fill(k):
            start = pl.multiple_of(jnp.maximum(fill_ref[k], 0), tile)
            return pltpu.make_async_copy(zero_ref, xs_ref.at[pl.ds(start, tile)], fill_sem)

        for k in range(fill_ref.shape[0]):
            @pl.when(fill_ref[k] >= 0)
            def _():
                fill(k).start()
        for k in range(fill_ref.shape[0]):
            @pl.when(fill_ref[k] >= 0)
            def _():
                fill(k).wait()

    _scatter_rows(pos_ref, rows_ref, xs_ref, sem, td)


def _scatter_rows(pos_ref, rows_ref, xs_ref, sem, td):
    _issue_row_copies(td, lambda j: pltpu.make_async_copy(
        rows_ref.at[pl.ds(j, 1)], xs_ref.at[pl.ds(pos_ref[0, j], 1)], sem))
    pltpu.make_async_copy(rows_ref, xs_ref.at[pl.ds(0, td)], sem).wait()


def _dispatch_more_kernel(pos_ref, rows_ref, base_ref, xs_ref, sem, *, td):
    del base_ref
    _scatter_rows(pos_ref, rows_ref, xs_ref, sem, td)


def _dispatch(rows, pos, fill, padded, td, tile):
    n = rows.shape[0]
    grid_spec = pltpu.PrefetchScalarGridSpec(
        num_scalar_prefetch=1,
        grid=(n // td,),
        in_specs=[pl.BlockSpec((None, 1, td), lambda i, fl: (i, 0, 0), memory_space=pltpu.SMEM),
                  pl.BlockSpec((td, ROW_W), lambda i, fl: (i, 0))],
        out_specs=pl.BlockSpec(memory_space=pl.ANY),
        scratch_shapes=[pltpu.VMEM((tile, ROW_W), _f32), pltpu.SemaphoreType.DMA(()), pltpu.SemaphoreType.DMA(())],
    )
    return pl.pallas_call(
        functools.partial(_dispatch_kernel, td=td, tile=tile),
        grid_spec=grid_spec,
        out_shape=jax.ShapeDtypeStruct((padded, ROW_W), _f32),
        compiler_params=_params("arbitrary"),
        name="dispatch",
    )(fill, pos.reshape(n // td, 1, td), rows)


def _dispatch_more(xs, rows, pos, td):
    n = rows.shape[0]
    return pl.pallas_call(
        functools.partial(_dispatch_more_kernel, td=td),
        grid=(n // td,),
        in_specs=[pl.BlockSpec((None, 1, td), lambda i: (i, 0, 0), memory_space=pltpu.SMEM),
                  pl.BlockSpec((td, ROW_W), lambda i: (i, 0)),
                  pl.BlockSpec(memory_space=pl.ANY)],
        out_specs=pl.BlockSpec(memory_space=pl.ANY),
        out_shape=jax.ShapeDtypeStruct(xs.shape, _f32),
        scratch_shapes=[pltpu.SemaphoreType.DMA(())],
        input_output_aliases={2: 0},
        compiler_params=_params("arbitrary"),
        name="dispatch_more",
    )(pos.reshape(n // td, 1, td), rows, xs)


def _combine_kernel(pos_ref, npos_ref, x_ref, g_ref, ys_ref, o_ref, buf0, buf1, sems, *, tc, nsteps):
    s = pl.program_id(0) * pl.num_programs(1) + pl.program_id(1)
    bufs = (buf0, buf1)

    def request(idx_ref, slot):
        _issue_row_copies(tc, lambda j: pltpu.make_async_copy(
            ys_ref.at[pl.ds(idx_ref[0, j], 1)], bufs[slot].at[pl.ds(j, 1)], sems.at[slot]))

    @pl.when(s == 0)
    def _():
        request(pos_ref, 0)

    for slot in range(2):
        @pl.when((s % 2 == slot) & (s + 1 < nsteps))
        def _():
            request(npos_ref, 1 - slot)

        @pl.when(s % 2 == slot)
        def _():
            pltpu.make_async_copy(ys_ref.at[pl.ds(0, tc)], bufs[slot], sems.at[slot]).wait()
            o_ref[...] = x_ref[...] + g_ref[...] * bufs[slot][...]


def _combine(x, ys, pos, g, tc):
    bsz, length, _ = x.shape
    per_batch = length // tc
    nsteps = bsz * per_batch
    tok = pl.BlockSpec((None, tc, D_MODEL), lambda b, i: (b, i, 0))
    pos3 = pos.reshape(nsteps, 1, tc)
    return pl.pallas_call(
        functools.partial(_combine_kernel, tc=tc, nsteps=nsteps),
        grid=(bsz, per_batch),
        in_specs=[pl.BlockSpec((None, 1, tc), lambda b, i: (b * per_batch + i, 0, 0), memory_space=pltpu.SMEM),
                  pl.BlockSpec((None, 1, tc), lambda b, i: (jnp.minimum(b * per_batch + i + 1, nsteps - 1), 0, 0),
                               memory_space=pltpu.SMEM),
                  tok,
                  pl.BlockSpec((None, 1, D_MODEL), lambda b, i: (b, 0, 0)),
                  pl.BlockSpec(memory_space=pl.ANY)],
        out_specs=tok,
        out_shape=jax.ShapeDtypeStruct(x.shape, _f32),
        scratch_shapes=[pltpu.VMEM((tc, D_MODEL), _f32), pltpu.VMEM((tc, D_MODEL), _f32),
                        pltpu.SemaphoreType.DMA((2,))],
        compiler_params=_params("arbitrary", "arbitrary"),
        name="combine",
    )(pos3, pos3, x, g, ys)


def _experts_kernel(ea_ref, eb_ref, chg_ref, nlive_ref, xs_ref, w1_ref, w3_ref, w2_ref, y_ref,
                    b1, b3, b2, s1, s3, s2, sems, *, layer):
    t = pl.program_id(0)
    live = t < nlive_ref[0]
    first = (ea_ref[t] // EXPERTS_PER_GROUP) * EXPERTS_PER_GROUP

    def stage(k, slot):
        return [pltpu.make_async_copy(w.at[layer, first + k], s.at[slot], sems.at[slot])
                for w, s in ((w1_ref, s1), (w3_ref, s3), (w2_ref, s2))]

    @pl.when(live & ((chg_ref[t] & 1) != 0))
    def _():
        for k in range(2):
            for cp in stage(k, k):
                cp.start()
        for k in range(EXPERTS_PER_GROUP):
            slot = k % 2
            for cp in stage(k, slot):
                cp.wait()
            b1[k] = s1[slot].astype(_bf16)
            b3[k] = s3[slot].astype(_bf16)
            b2[k] = s2[slot].astype(_bf16)
            if k + 2 < EXPERTS_PER_GROUP:
                for cp in stage(k + 2, slot):
                    cp.start()

    def run(nrows):
        xs = xs_ref[:nrows, :D_MODEL].astype(_bf16)
        gates = xs_ref[:nrows, D_MODEL:]

        def expert(e):
            hid = jax.nn.silu(_dot(xs, b1[e])) * _dot(xs, b3[e])
            return _dot(hid.astype(_bf16), b2[e])

        y_ref[:nrows, :] = (gates[:, 0:1] * expert(ea_ref[t] - first) + gates[:, 1:2] * expert(eb_ref[t] - first))

    half_only = (chg_ref[t] & 4) != 0
    half = y_ref.shape[0] // 2

    @pl.when(live & jnp.logical_not(half_only))
    def _():
        run(y_ref.shape[0])

    @pl.when(live & half_only)
    def _():
        run(half)
        y_ref[half:, :] = jnp.zeros((half, D_MODEL), _f32)

    @pl.when(jnp.logical_not(live))
    def _():
        y_ref[...] = jnp.zeros_like(y_ref)


def _experts(xs, tile_ea, tile_eb, chg, nlive, w1, w3, w2, layer, tile):
    ntiles = xs.shape[0] // tile
    hbm = pl.BlockSpec(memory_space=pl.ANY)
    up, down = (D_MODEL, D_EXPERT), (D_EXPERT, D_MODEL)
    grid_spec = pltpu.PrefetchScalarGridSpec(
        num_scalar_prefetch=4,
        grid=(ntiles,),
        in_specs=[pl.BlockSpec((tile, ROW_W), lambda t, ea, eb, cg, nl: (t, 0)), hbm, hbm, hbm],
        out_specs=pl.BlockSpec((tile, D_MODEL), lambda t, ea, eb, cg, nl: (t, 0)),
        scratch_shapes=[pltpu.VMEM((EXPERTS_PER_GROUP,) + up, _bf16), pltpu.VMEM((EXPERTS_PER_GROUP,) + up, _bf16),
                        pltpu.VMEM((EXPERTS_PER_GROUP,) + down, _bf16),
                        pltpu.VMEM((2,) + up, _f32), pltpu.VMEM((2,) + up, _f32), pltpu.VMEM((2,) + down, _f32),
                        pltpu.SemaphoreType.DMA((2,))],
    )
    return pl.pallas_call(
        functools.partial(_experts_kernel, layer=layer),
        grid_spec=grid_spec,
        out_shape=jax.ShapeDtypeStruct((xs.shape[0], D_MODEL), _f32),
        compiler_params=_params("arbitrary"),
        name="experts",
    )(tile_ea, tile_eb, chg, nlive, xs, w1, w3, w2)


def _dispatch_plan(cls, rank, counts, first_counts, later, tile):
    n = cls.shape[0]
    padded = -(-(n + N_CLASSES * (tile - 1)) // tile) * tile
    ntiles = padded // tile
    pcounts = (counts + tile - 1) // tile * tile
    pad_end = jnp.cumsum(pcounts)
    pad_off = pad_end - pcounts
    pos = pad_off[cls] + rank
    nlive = pad_end[-1] // tile
    tile_start = jnp.arange(ntiles, dtype=jnp.int32) * tile
    tile_cls = jnp.sum((pad_end[None, :] <= jnp.minimum(tile_start, pad_end[-1] - 1)[:, None]).astype(jnp.int32), axis=1)
    tile_cls = jnp.minimum(tile_cls, N_CLASSES - 1)
    grp = tile_cls // N_PAIRS
    pair = tile_cls % N_PAIRS
    tile_ea = grp * EXPERTS_PER_GROUP + jnp.asarray(PAIR_A, jnp.int32)[pair]
    tile_eb = grp * EXPERTS_PER_GROUP + jnp.asarray(PAIR_B, jnp.int32)[pair]
    new_group = jnp.concatenate([jnp.ones((1,), jnp.int32), (grp[1:] != grp[:-1]).astype(jnp.int32)])
    tokens_in_tile = (pad_off + counts)[tile_cls] - tile_start
    chg = new_group + 4 * (tokens_in_tile <= tile // 2).astype(jnp.int32)
    per_class = -(-(later + tile - 1) // tile) + 1
    uncovered = pad_off + first_counts
    tails = []
    for j in range(per_class):
        start = pad_end - (j + 1) * tile
        tails.append(jnp.where((start >= pad_off) & (start + tile > uncovered), start, -1))
    past = (nlive + jnp.arange(N_CLASSES, dtype=jnp.int32)) * tile
    fill = jnp.concatenate(tails + [jnp.where(past < padded, past, -1)]).astype(jnp.int32)
    return pos.astype(jnp.int32), padded, tile_ea, tile_eb, chg, nlive.astype(jnp.int32).reshape(1), fill


def _ffn(streams, w1, w3, w2, layer, tile):
    sizes = [s[0].shape[0] * s[0].shape[1] for s in streams]
    cls = [s[2][:, 0, :].reshape(n).astype(jnp.int32) for s, n in zip(streams, sizes)]
    cnts = [s[3][:N_CLASSES, 0].astype(jnp.int32) for s in streams]
    ranks, before = [], jnp.zeros((N_CLASSES,), jnp.int32)
    for s, n, c, cnt in zip(streams, sizes, cls, cnts):
        ranks.append(s[2][:, 1, :].reshape(n).astype(jnp.int32) + before[c])
        before = before + cnt
    pos, padded, tile_ea, tile_eb, chg, nlive, fill = _dispatch_plan(
        jnp.concatenate(cls), jnp.concatenate(ranks), before, cnts[0], sum(sizes[1:]), tile)
    starts = np.cumsum([0] + sizes)
    pos = [pos[a:b] for a, b in zip(starts[:-1], starts[1:])]
    xs = _dispatch(streams[0][1].reshape(sizes[0], ROW_W), pos[0], fill, padded, streams[0][5], tile)
    for s, n, p in zip(streams[1:], sizes[1:], pos[1:]):
        xs = _dispatch_more(xs, s[1].reshape(n, ROW_W), p, s[5])
    ys = _experts(xs, tile_ea, tile_eb, chg, nlive, w1, w3, w2, layer, tile)
    return [_combine(s[0], ys, p, s[4], s[5]) for s, p in zip(streams, pos)]


def kernel(x, c, ctx, c_ctx, w_ada, b_ada, w_in, q_gain, k_gain, rpb, sgu_ln, sgu_w, sgu_b, out_gain,
           w_out, rg_w, rg_b, re_w, re_b, w1, w3, w2):
    bsz, seq, _ = x.shape
    depth = w_ada.shape[0]
    rows = seq // GRID_W
    assert seq % Q_BLOCK == 0 and rows // Q_ROWS >= 3 and ctx.shape[1] % SGU_CHUNK == 0
    tm_x, tm_c = 512, ctx.shape[1]

    cond = jnp.zeros((SUBLANES, D_MODEL), _f32).at[:bsz].set(c).at[bsz].set(c_ctx)
    mods = _adaln(cond, w_ada, b_ada).reshape(depth, SUBLANES, N_MOD, D_MODEL)

    assert MXU_TILE % NA_HEAD_DIM == 0 and NA_W % MXU_TILE == 0
    head_id = np.arange(MXU_TILE) // NA_HEAD_DIM
    avg = jnp.asarray((head_id[:, None] == head_id[None, :]).astype(np.float32) / NA_HEAD_DIM, _bf16)
    win_plan = _window_plan(rows)
    tri = {tm: jnp.asarray(np.triu(np.ones((tm, tm), np.float32), 1), _bf16) for tm in {tm_x, tm_c}}
    sel_np = np.zeros((2 * SUBLANES, LANES), np.float32)
    for n in range(2):
        sel_np[n * GATE_PIECES:(n + 1) * GATE_PIECES, n] = 1.0
    sel = jnp.asarray(sel_np, _bf16)

    for i in range(depth):
        last = i == depth - 1
        mx = [mods[i, :bsz, j][:, None, :] for j in range(N_MOD)]
        mc = [jnp.broadcast_to(mods[i, bsz, j][None, None, :], (bsz, 1, D_MODEL)) for j in range(N_MOD)]
        w_in_b = jnp.concatenate([w_in[i, :, 3 * NA_W:], w_in[i, :, :3 * NA_W]], axis=1).astype(_bf16)
        qg = (jnp.tile(q_gain[i], NA_HEADS) * NA_HEAD_DIM ** -0.5)[None, :]
        kg = jnp.tile(k_gain[i], NA_HEADS)[None, :]
        ln_g = sgu_ln[i][None, :]
        ws_cat = jnp.concatenate([sgu_w[i, 0::2], sgu_w[i, 1::2]], axis=2).astype(_bf16)
        bs_tab = jnp.repeat(sgu_b[i].T, SGU_GROUP_DIM, axis=1)
        og = out_gain[i][None, :]
        w_out_b = w_out[i].astype(_bf16)
        pad_w = jnp.zeros((SUBLANES - EXPERTS_PER_GROUP, D_MODEL), _f32)
        pad_b = jnp.zeros((SUBLANES - EXPERTS_PER_GROUP,), _f32)
        w_r = jnp.concatenate([rg_w[i].T, pad_w] + [t for g in range(N_GROUPS) for t in (re_w[i, g].T, pad_w)]
                              + [jnp.zeros((ROUTER_ROWS - SUBLANES * (N_GROUPS + 1), D_MODEL), _f32)]).astype(_bf16)
        b_r = jnp.concatenate([rg_b[i], pad_b] + [t for g in range(N_GROUPS) for t in (re_b[i, g], pad_b)]
                              + [jnp.zeros((ROUTER_ROWS - SUBLANES * (N_GROUPS + 1),), _f32)])[:, None]
        score_bound = 1.02 * NA_HEAD_DIM ** 0.5 * jnp.max(jnp.abs(q_gain[i])) * jnp.max(jnp.abs(k_gain[i]))
        shift = (score_bound + jnp.maximum(jnp.max(rpb[i]), 0.0)).reshape(1)
        spread = 2.0 * score_bound + jnp.maximum(jnp.max(rpb[i]), 0.0) - jnp.minimum(jnp.min(rpb[i]), 0.0)

        q, k, v, gu, vn = _proj(x, mx[0], mx[1], w_in_b, avg, qg, kg, ln_g, tm_x)
        qc, kc, vc, guc, vnc = _proj(ctx, mc[0], mc[1], w_in_b, avg, qg, kg, ln_g, tm_c)
        o_na = lax.cond(
            spread < MAX_SHIFT_SPREAD,
            lambda: _attention(q, k, v, kc, vc, _window_col_table(rpb[i], shift[0]), win_plan, shift, True),
            lambda: _attention(q, k, v, kc, vc, _window_col_table(rpb[i], 0.0), win_plan, shift, False))
        x_new, xrows, meta, counts = _merge(x, o_na, gu, vn, mx[2], mx[3], mx[4], ws_cat, bs_tab, og,
                                            w_out_b, w_r, b_r, tri[tm_x], sel, tm_x)
        streams = [(x_new, xrows, meta, counts, mx[5], 1024)]
        if not last:
            oc_na = _ctx_attention(qc, kc, vc)
            c_new, crows, cmeta, ccounts = _merge(ctx, oc_na, guc, vnc, mc[2], mc[3], mc[4], ws_cat, bs_tab,
                                                  og, w_out_b, w_r, b_r, tri[tm_c], sel, tm_c)
            streams.append((c_new, crows, cmeta, ccounts, mc[5], tm_c))
        outs = _ffn(streams, w1, w3, w2, i, 512)
        x = outs[0]
        if not last:
            ctx = outs[1]
    return x
```
